```python
import functools
import jax, jax.numpy as jnp
from jax import lax
import numpy as np

D_MODEL = 1024
BATCH = 32
SEQ = 256
DEPTH = 4
DEC_BATCH = 8
DEC_SEQ = 2048
PAST_LEN = 512

GRID_W = 64
HEAD_DIM = 64
NA_HEADS = 8
NA_WIN_R = 8
NA_WIN_C = 16
GQA_HEADS = 8
GQA_KV_HEADS = 2
BRANCH_W = 512
N_BRANCH = 4
CONV_K = 3
POOL_WINDOWS = (2, 4, 8, 16)
POOL_GROUPS = 4
POOL_GW = BRANCH_W // POOL_GROUPS
N_EXPERTS = 16
EXPERT_FF = 1024
EC_CAPACITY = 2
Q_BLOCK = 128
ROPE_THETA = 10000.0
LN_EPS = 1e-6
RMS_EPS = 1e-6
N_MOD = 6
ADA_INIT = 0.5
DEEPNORM_ALPHA = (2 * DEPTH) ** 0.25
DEEPNORM_BETA = (8 * DEPTH) ** -0.25
NA_W = NA_HEADS * HEAD_DIM
GQA_QW = GQA_HEADS * HEAD_DIM
GQA_KW = GQA_KV_HEADS * HEAD_DIM
IN_SIZES = (NA_W, NA_W, NA_W, GQA_QW, GQA_KW, GQA_KW, BRANCH_W, BRANCH_W, BRANCH_W, BRANCH_W, N_BRANCH * D_MODEL)
IN_OFFSETS = tuple(int(v) for v in np.cumsum(IN_SIZES)[:-1])
IN_W = int(sum(IN_SIZES))

kernel_name = "hybrid_diffusion_gated_branch_trunk_step"


def layer_norm(x, g, b):
    xf = x.astype(jnp.float32)
    mu = jnp.mean(xf, -1, keepdims=True)
    var = jnp.mean(jnp.square(xf - mu), -1, keepdims=True)
    return ((xf - mu) * lax.rsqrt(var + LN_EPS) * g + b).astype(x.dtype)


def rms_norm(x, g):
    xf = x.astype(jnp.float32)
    return (xf * lax.rsqrt(jnp.mean(xf * xf, -1, keepdims=True) + RMS_EPS) * g).astype(x.dtype)


def adaln(cvec, w, b):
    return jnp.einsum("bd,dk->bk", jax.nn.silu(cvec), w) + b


def modulate(x, shift, scale):
    return x * (1 + scale) + shift


def _rotate(xh, ang):
    x1, x2 = jnp.split(xh.astype(jnp.float32), 2, axis=-1)
    cos = jnp.cos(ang)[None, :, None, :]
    sin = jnp.sin(ang)[None, :, None, :]
    return jnp.concatenate([x1 * cos - x2 * sin, x1 * sin + x2 * cos], -1)


def axial_rope(x):
    L = x.shape[1]
    t = jnp.arange(L)
    n_freq = HEAD_DIM // 4
    inv = ROPE_THETA ** (-jnp.arange(n_freq, dtype=jnp.float32) / n_freq)
    ang_r = (t // GRID_W).astype(jnp.float32)[:, None] * inv
    ang_c = (t % GRID_W).astype(jnp.float32)[:, None] * inv
    xr, xc = jnp.split(x, 2, axis=-1)
    return jnp.concatenate([_rotate(xr, ang_r), _rotate(xc, ang_c)], -1).astype(x.dtype)


def blocked_attention(q, k, v):
    B, Lq, H, dh = q.shape
    G = k.shape[2]
    R = H // G
    scale = dh ** -0.5
    nb = Lq // Q_BLOCK
    qb = q.reshape(B, nb, Q_BLOCK, G, R, dh).transpose(1, 0, 2, 3, 4, 5)

    def one_block(qi):
        s = jnp.einsum("bqgrd,bkgd->bgrqk", qi, k, preferred_element_type=jnp.float32) * scale
        p = jax.nn.softmax(s, axis=-1).astype(v.dtype)
        return jnp.einsum("bgrqk,bkgd->bqgrd", p, v)

    o = lax.map(one_block, qb)
    return o.transpose(1, 0, 2, 3, 4, 5).reshape(B, Lq, H * dh)


def neighbourhood_attention(q, k, v, kc, vc, rpb):
    B, L, H, dh = q.shape
    rows = L // GRID_W
    wr = min(NA_WIN_R, rows)
    wc = NA_WIN_C
    scale = dh ** -0.5
    qg = q.reshape(B, rows, GRID_W, H, dh).transpose(1, 0, 2, 3, 4)
    kg = k.reshape(B, rows, GRID_W, H, dh)
    vg = v.reshape(B, rows, GRID_W, H, dh)
    cols = np.arange(GRID_W)
    col_start = np.clip(cols - wc // 2, 0, GRID_W - wc)
    col_idx = col_start[:, None] + np.arange(wc)[None, :]
    col_rel = col_idx - cols[:, None] + (NA_WIN_C - 1)
    row_starts = jnp.asarray(np.clip(np.arange(rows) - wr // 2, 0, rows - wr), jnp.int32)
    row_ids = jnp.arange(rows, dtype=jnp.int32)

    def one_row(args):
        qr, r, rs = args
        k_rows = lax.dynamic_slice_in_dim(kg, rs, wr, axis=1)
        v_rows = lax.dynamic_slice_in_dim(vg, rs, wr, axis=1)
        k_win = k_rows[:, :, col_idx]
        v_win = v_rows[:, :, col_idx]
        row_rel = rs + jnp.arange(wr) - r + (NA_WIN_R - 1)
        bias = rpb[:, row_rel[:, None, None], col_rel[None, :, :]]
        s_loc = jnp.einsum("bqhd,biqjhd->bhqij", qr, k_win, preferred_element_type=jnp.float32) * scale
        s_loc = s_loc + bias.transpose(0, 2, 1, 3)[None]
        s_ctx = jnp.einsum("bqhd,bphd->bhqp", qr, kc, preferred_element_type=jnp.float32) * scale
        s = jnp.concatenate([s_loc.reshape(B, H, GRID_W, wr * wc), s_ctx], axis=-1)
        p = jax.nn.softmax(s, axis=-1).astype(v.dtype)
        p_loc = p[..., : wr * wc].reshape(B, H, GRID_W, wr, wc)
        p_ctx = p[..., wr * wc:]
        return (jnp.einsum("bhqij,biqjhd->bqhd", p_loc, v_win)
                + jnp.einsum("bhqp,bphd->bqhd", p_ctx, vc))

    o = lax.map(one_row, (qg, row_ids, row_starts))
    return o.transpose(1, 0, 2, 3, 4).reshape(B, L, H * dh)


def short_conv(xb, xc, xh, w, b):
    u = xc * xh
    y = lax.conv_general_dilated(u, w[:, None, :], window_strides=(1,),
                                 padding=[(CONV_K // 2, CONV_K // 2)],
                                 dimension_numbers=("NWC", "WIO", "NWC"),
                                 feature_group_count=BRANCH_W) + b
    return xb * y


def multiscale_pool(u, w_grp, scale):
    B, L, _ = u.shape
    uf = u.astype(jnp.float32)
    cs = jnp.concatenate([jnp.zeros((B, 1, BRANCH_W), jnp.float32), jnp.cumsum(uf, axis=1)], axis=1)
    t = jnp.arange(L)
    outs = []
    for g, win in enumerate(POOL_WINDOWS):
        lo = jnp.clip(t - win // 2, 0, L)
        hi = jnp.clip(t + win // 2, 0, L)
        sl = slice(g * POOL_GW, (g + 1) * POOL_GW)
        csg = cs[..., sl]
        cnt = (hi - lo).astype(jnp.float32)[None, :, None]
        outs.append((csg[:, hi] - csg[:, lo]) / cnt - uf[..., sl])
    pooled = jnp.stack(outs, axis=2).astype(u.dtype)
    mixed = jnp.einsum("blgc,gcd->blgd", pooled, w_grp)
    return mixed.reshape(B, L, BRANCH_W) * scale


def merge_branches(branches, gate_logits, w_branch, w_out):
    B, L, _ = gate_logits.shape
    bstack = jnp.stack(branches, axis=2)
    proj = jnp.einsum("blnc,ncd->blnd", bstack, w_branch)
    gates = jax.nn.sigmoid(gate_logits.reshape(B, L, N_BRANCH, D_MODEL))
    merged = jnp.sum(gates * proj, axis=2)
    return jnp.einsum("bld,de->ble", merged, w_out)


def split_inputs(h, w_in):
    return jnp.split(jnp.einsum("bld,dk->blk", h, w_in), IN_OFFSETS, axis=-1)


def mixer_context(h, p):
    B, L, _ = h.shape
    (na_q, na_k, na_v, g_q, g_k, g_v, cv_b, cv_c, cv_h, pl_u, gate_logits) = split_inputs(h, p["w_in"])
    na_q = na_q.reshape(B, L, NA_HEADS, HEAD_DIM)
    na_k = na_k.reshape(B, L, NA_HEADS, HEAD_DIM)
    na_v = na_v.reshape(B, L, NA_HEADS, HEAD_DIM)
    o_na = blocked_attention(na_q, na_k, na_v)
    g_q = rms_norm(g_q.reshape(B, L, GQA_HEADS, HEAD_DIM), p["qn_g"])
    g_k = rms_norm(g_k.reshape(B, L, GQA_KV_HEADS, HEAD_DIM), p["kn_g"])
    g_v = g_v.reshape(B, L, GQA_KV_HEADS, HEAD_DIM)
    o_gqa = blocked_attention(g_q, g_k, g_v)
    o_conv = short_conv(cv_b, cv_c, cv_h, p["conv_w"], p["conv_b"])
    o_pool = multiscale_pool(pl_u, p["pool_w"], p["pool_scale"])
    y = merge_branches((o_na, o_gqa, o_conv, o_pool), gate_logits, p["w_branch"], p["w_out"])
    return y, (na_k, na_v, g_k, g_v)


def mixer_latent(h, p, na_kc, na_vc, gqa_kc, gqa_vc):
    B, L, _ = h.shape
    (na_q, na_k, na_v, g_q, g_k, g_v, cv_b, cv_c, cv_h, pl_u, gate_logits) = split_inputs(h, p["w_in"])
    o_na = neighbourhood_attention(na_q.reshape(B, L, NA_HEADS, HEAD_DIM),
                                   na_k.reshape(B, L, NA_HEADS, HEAD_DIM),
                                   na_v.reshape(B, L, NA_HEADS, HEAD_DIM),
                                   na_kc, na_vc, p["na_rpb"])
    g_q = axial_rope(rms_norm(g_q.reshape(B, L, GQA_HEADS, HEAD_DIM), p["qn_g"]))
    g_k = axial_rope(rms_norm(g_k.reshape(B, L, GQA_KV_HEADS, HEAD_DIM), p["kn_g"]))
    g_v = g_v.reshape(B, L, GQA_KV_HEADS, HEAD_DIM)
    o_gqa = blocked_attention(g_q, jnp.concatenate([g_k, gqa_kc], axis=1),
                              jnp.concatenate([g_v, gqa_vc], axis=1))
    o_conv = short_conv(cv_b, cv_c, cv_h, p["conv_w"], p["conv_b"])
    o_pool = multiscale_pool(pl_u, p["pool_w"], p["pool_scale"])
    y = merge_branches((o_na, o_gqa, o_conv, o_pool), gate_logits, p["w_branch"], p["w_out"])
    return y, None


def expert_choice_moe(x, w_router, w_gate, w_up, w_down):
    B, L, D = x.shape
    n = B * L
    cap = EC_CAPACITY * n // N_EXPERTS
    xt = x.reshape(n, D)
    aff = jax.nn.softmax(jnp.einsum("nd,de->ne", xt, w_router, preferred_element_type=jnp.float32), axis=-1)
    g, idx = lax.top_k(aff.T, cap)
    xs = xt[idx]
    hdn = jax.nn.silu(jnp.einsum("ecd,edf->ecf", xs, w_gate)) * jnp.einsum("ecd,edf->ecf", xs, w_up)
    ye = jnp.einsum("ecf,efd->ecd", hdn, w_down) * g[..., None].astype(x.dtype)
    y = jnp.zeros_like(xt).at[idx.reshape(-1)].add(ye.reshape(-1, D))
    return y.reshape(B, L, D)


def post_norm_layer(x, mod, mix_fn, p):
    shift1, scale1, gate1, shift2, scale2, gate2 = jnp.split(mod, N_MOD, axis=-1)
    y_mix, ctx = mix_fn(modulate(x, shift1, scale1))
    x = layer_norm(DEEPNORM_ALPHA * x + gate1 * y_mix, p["ln1_g"], p["ln1_b"])
    y_ff = expert_choice_moe(modulate(x, shift2, scale2), p["w_router"], p["w_gate"], p["w_up"], p["w_down"])
    x = layer_norm(DEEPNORM_ALPHA * x + gate2 * y_ff, p["ln2_g"], p["ln2_b"])
    return x, ctx


def setup_inputs(seed: int = 0) -> dict:
    key = jax.random.key(seed)
    ks = jax.random.split(key, 32)
    nrm = lambda k, s: jax.random.normal(k, s, jnp.float32)
    D = D_MODEL
    return {
        "x_prompt": nrm(ks[0], (BATCH, SEQ, D)),
        "x_sample": nrm(ks[1], (DEC_BATCH, DEC_SEQ, D)),
        "cache_na_k": nrm(ks[2], (DEC_BATCH, DEPTH, PAST_LEN, NA_HEADS, HEAD_DIM)),
        "cache_na_v": nrm(ks[3], (DEC_BATCH, DEPTH, PAST_LEN, NA_HEADS, HEAD_DIM)),
        "cache_gqa_k": nrm(ks[4], (DEC_BATCH, DEPTH, PAST_LEN, GQA_KV_HEADS, HEAD_DIM)),
        "cache_gqa_v": nrm(ks[5], (DEC_BATCH, DEPTH, PAST_LEN, GQA_KV_HEADS, HEAD_DIM)),
        "c": nrm(ks[6], (DEC_BATCH, D)),
        "c_ctx": nrm(ks[7], (D,)),
        "w_ada": nrm(ks[8], (DEPTH, D, N_MOD * D)) * (ADA_INIT * D ** -0.5),
        "b_ada": nrm(ks[9], (DEPTH, N_MOD * D)) * 0.01,
        "w_in": nrm(ks[10], (DEPTH, D, IN_W)) * D ** -0.5,
        "na_rpb": nrm(ks[11], (DEPTH, NA_HEADS, 2 * NA_WIN_R - 1, 2 * NA_WIN_C - 1)) * 0.1,
        "qn_g": 1.0 + 0.01 * nrm(ks[12], (DEPTH, HEAD_DIM)),
        "kn_g": 1.0 + 0.01 * nrm(ks[13], (DEPTH, HEAD_DIM)),
        "conv_w": nrm(ks[14], (DEPTH, CONV_K, BRANCH_W)) * CONV_K ** -0.5,
        "conv_b": nrm(ks[15], (DEPTH, BRANCH_W)) * 0.01,
        "pool_w": nrm(ks[16], (DEPTH, POOL_GROUPS, POOL_GW, POOL_GW)) * POOL_GW ** -0.5,
        "pool_scale": 1.0 + 0.01 * nrm(ks[17], (DEPTH, BRANCH_W)),
        "w_branch": nrm(ks[18], (DEPTH, N_BRANCH, BRANCH_W, D)) * BRANCH_W ** -0.5,
        "w_out": nrm(ks[19], (DEPTH, D, D)) * (DEEPNORM_BETA * D ** -0.5),
        "ln1_g": 1.0 + 0.01 * nrm(ks[20], (DEPTH, D)),
        "ln1_b": 0.01 * nrm(ks[21], (DEPTH, D)),
        "ln2_g": 1.0 + 0.01 * nrm(ks[22], (DEPTH, D)),
        "ln2_b": 0.01 * nrm(ks[23], (DEPTH, D)),
        "w_router": nrm(ks[24], (DEPTH, D, N_EXPERTS)) * D ** -0.5,
        "w_gate": nrm(ks[25], (DEPTH, N_EXPERTS, D, EXPERT_FF)) * D ** -0.5,
        "w_up": nrm(ks[26], (DEPTH, N_EXPERTS, D, EXPERT_FF)) * D ** -0.5,
        "w_down": nrm(ks[27], (DEPTH, N_EXPERTS, EXPERT_FF, D)) * (DEEPNORM_BETA * EXPERT_FF ** -0.5),
    }


def reference(x_prompt, x_sample, cache_na_k, cache_na_v, cache_gqa_k, cache_gqa_v, c, c_ctx,
              w_ada, b_ada, w_in, na_rpb, qn_g, kn_g, conv_w, conv_b, pool_w, pool_scale,
              w_branch, w_out, ln1_g, ln1_b, ln2_g, ln2_b, w_router, w_gate, w_up, w_down):
    y_p = x_prompt
    y_s = x_sample
    na_k_list, na_v_list, gqa_k_list, gqa_v_list = [], [], [], []
    for l in range(DEPTH):
        p = {"w_in": w_in[l], "na_rpb": na_rpb[l], "qn_g": qn_g[l], "kn_g": kn_g[l],
             "conv_w": conv_w[l], "conv_b": conv_b[l], "pool_w": pool_w[l], "pool_scale": pool_scale[l],
             "w_branch": w_branch[l], "w_out": w_out[l], "ln1_g": ln1_g[l], "ln1_b": ln1_b[l],
             "ln2_g": ln2_g[l], "ln2_b": ln2_b[l], "w_router": w_router[l], "w_gate": w_gate[l],
             "w_up": w_up[l], "w_down": w_down[l]}
        mod_ctx = adaln(c_ctx[None, :], w_ada[l], b_ada[l])[:, None, :]
        mod_lat = adaln(c, w_ada[l], b_ada[l])[:, None, :]
        y_p, (k_na, v_na, k_g, v_g) = post_norm_layer(y_p, mod_ctx, functools.partial(mixer_context, p=p), p)
        na_k_list.append(k_na)
        na_v_list.append(v_na)
        gqa_k_list.append(k_g)
        gqa_v_list.append(v_g)
        mix_lat = functools.partial(mixer_latent, p=p, na_kc=cache_na_k[:, l], na_vc=cache_na_v[:, l],
                                    gqa_kc=cache_gqa_k[:, l], gqa_vc=cache_gqa_v[:, l])
        y_s, _ = post_norm_layer(y_s, mod_lat, mix_lat, p)
    state_na_k = jnp.stack(na_k_list, axis=1)
    state_na_v = jnp.stack(na_v_list, axis=1)
    state_gqa_k = jnp.stack(gqa_k_list, axis=1)
    state_gqa_v = jnp.stack(gqa_v_list, axis=1)
    return (y_p, y_s, state_na_k, state_na_v, state_gqa_k, state_gqa_v)
```

```python
import functools

import numpy as np
import jax
import jax.numpy as jnp
from jax import lax
from jax.experimental import pallas as pl
from jax.experimental.pallas import tpu as pltpu

F32 = jnp.float32
BF16 = jnp.bfloat16
I32 = jnp.int32
HIGHEST = lax.Precision.HIGHEST

D_MODEL = 1024
NOMINAL_DEPTH = 4
GRID_W = 64
HEAD_DIM = 64
NA_HEADS = 8
NA_WIN_R = 8
NA_WIN_C = 16
GQA_HEADS = 8
GQA_KV_HEADS = 2
GQA_REP = GQA_HEADS // GQA_KV_HEADS
BRANCH_W = 512
N_BRANCH = 4
CONV_K = 3
POOL_WINDOWS = (2, 4, 8, 16)
POOL_GW = BRANCH_W // len(POOL_WINDOWS)
POOL_HALO = max(POOL_WINDOWS) // 2
N_EXPERTS = 16
EXPERT_FF = 1024
EC_CAPACITY = 2
ROPE_THETA = 10000.0
LN_EPS = 1e-6
RMS_EPS = 1e-6
N_MOD = 6
DEEPNORM_ALPHA = (2 * NOMINAL_DEPTH) ** 0.25
NA_W = NA_HEADS * HEAD_DIM
GQA_QW = GQA_HEADS * HEAD_DIM
GQA_KW = GQA_KV_HEADS * HEAD_DIM
ATT_W = 3 * NA_W + GQA_QW + 2 * GQA_KW
CP_W = 4 * BRANCH_W
GATE_W = N_BRANCH * D_MODEL
ATT_SCALE = HEAD_DIM ** -0.5
NEG_BIG = -1e30

LANES = 128
SUBLANES = 8
VMEM_LIMIT = 56 * 1024 * 1024

TOK_TILE_IN = 512
TOK_TILE = 256
CHUNK = LANES
FFN_TILE = 256
GQA_QBLOCK = 512


def _cparams(sem):
    return pltpu.CompilerParams(dimension_semantics=sem, vmem_limit_bytes=VMEM_LIMIT)


def _adaln_kernel(cv_ref, w_ref, b_ref, o_ref):
    cv = cv_ref[...]
    s = cv * jax.nn.sigmoid(cv)
    o_ref[0] = jnp.dot(s, w_ref[0], precision=HIGHEST, preferred_element_type=F32) + b_ref[0]


def _adaln(cvecs, w_ada, b_ada):
    depth = w_ada.shape[0]
    r = cvecs.shape[0]
    tn = 1536
    nw = N_MOD * D_MODEL
    return pl.pallas_call(
        _adaln_kernel,
        grid=(depth, nw // tn),
        in_specs=[
            pl.BlockSpec((r, D_MODEL), lambda l, j: (0, 0)),
            pl.BlockSpec((1, D_MODEL, tn), lambda l, j: (l, 0, j)),
            pl.BlockSpec((1, 1, tn), lambda l, j: (l, 0, j)),
        ],
        out_specs=pl.BlockSpec((1, r, tn), lambda l, j: (l, 0, j)),
        out_shape=jax.ShapeDtypeStruct((depth, r, nw), F32),
        compiler_params=_cparams(("arbitrary", "arbitrary")),
        name="adaln",
    )(cvecs, w_ada, b_ada.reshape(depth, 1, nw))


def _inproj_kernel(x_ref, mod_ref, w_ref, att_ref, cp_ref):
    shift = mod_ref[0, 0:1, :]
    scale = mod_ref[0, 1:2, :]
    xm = (x_ref[...] * (1.0 + scale) + shift).astype(BF16)
    z = jnp.dot(xm, w_ref[...], preferred_element_type=F32)
    att_ref[...] = z[:, :ATT_W]
    cp_ref[...] = z[:, ATT_W:]


def _inproj(x, mod, w_a, seq_len):
    n = x.shape[0]
    t = min(TOK_TILE_IN, seq_len)
    per = n // mod.shape[0] // t
    return pl.pallas_call(
        _inproj_kernel,
        grid=(n // t,),
        in_specs=[
            pl.BlockSpec((t, D_MODEL), lambda i: (i, 0)),
            pl.BlockSpec((1, N_MOD, D_MODEL), lambda i: (i // per, 0, 0)),
            pl.BlockSpec((D_MODEL, ATT_W + CP_W), lambda i: (0, 0)),
        ],
        out_specs=[
            pl.BlockSpec((t, ATT_W), lambda i: (i, 0)),
            pl.BlockSpec((t, CP_W), lambda i: (i, 0)),
        ],
        out_shape=[jax.ShapeDtypeStruct((n, ATT_W), F32), jax.ShapeDtypeStruct((n, CP_W), F32)],
        compiler_params=_cparams(("arbitrary",)),
        name="inproj",
    )(x, mod, w_a)


def _rms(x, g):
    return x * lax.rsqrt(jnp.mean(x * x, axis=-1, keepdims=True) + RMS_EPS) * g


def _rope(x, cos, sin):
    q = HEAD_DIM // 4
    swapped = jnp.concatenate([x[:, q:2 * q], x[:, 0:q], x[:, 3 * q:4 * q], x[:, 2 * q:3 * q]], axis=-1)
    return x * cos + swapped * sin


def _dot_nt(a, b):
    return lax.dot_general(a, b, (((1,), (1,)), ((), ())), preferred_element_type=F32)


def _softmax_pv(scores, values):
    m = scores[0].max(axis=-1, keepdims=True)
    for s in scores[1:]:
        m = jnp.maximum(m, s.max(axis=-1, keepdims=True))
    l = None
    o = None
    for s, v in zip(scores, values):
        p = jnp.exp(s - m)
        ls = p.sum(axis=-1, keepdims=True)
        os_ = jnp.dot(p.astype(BF16), v, preferred_element_type=F32)
        l = ls if l is None else l + ls
        o = os_ if o is None else o + os_
    return o / l


def _ctx_attn_kernel(z_ref, qn_ref, kn_ref, o_ref, gk_ref):
    outs = []
    for h in range(NA_HEADS):
        q = (z_ref[:, h * HEAD_DIM:(h + 1) * HEAD_DIM] * ATT_SCALE).astype(BF16)
        k = z_ref[:, NA_W + h * HEAD_DIM:NA_W + (h + 1) * HEAD_DIM].astype(BF16)
        v = z_ref[:, 2 * NA_W + h * HEAD_DIM:2 * NA_W + (h + 1) * HEAD_DIM].astype(BF16)
        outs.append(_softmax_pv([_dot_nt(q, k)], [v]))
    qoff = 3 * NA_W
    koff = qoff + GQA_QW
    voff = koff + GQA_KW
    ks, vs = [], []
    for g in range(GQA_KV_HEADS):
        kg = _rms(z_ref[:, koff + g * HEAD_DIM:koff + (g + 1) * HEAD_DIM], kn_ref[...])
        gk_ref[:, g * HEAD_DIM:(g + 1) * HEAD_DIM] = kg
        ks.append(kg.astype(BF16))
        vs.append(z_ref[:, voff + g * HEAD_DIM:voff + (g + 1) * HEAD_DIM].astype(BF16))
    for h in range(GQA_HEADS):
        g = h // GQA_REP
        qh = _rms(z_ref[:, qoff + h * HEAD_DIM:qoff + (h + 1) * HEAD_DIM], qn_ref[...])
        q = (qh * ATT_SCALE).astype(BF16)
        outs.append(_softmax_pv([_dot_nt(q, ks[g])], [vs[g]]))
    o_ref[...] = jnp.concatenate(outs, axis=-1).astype(BF16)


def _ctx_attn(z_att, qn_g, kn_g, seq_len):
    n = z_att.shape[0]
    return pl.pallas_call(
        _ctx_attn_kernel,
        grid=(n // seq_len,),
        in_specs=[
            pl.BlockSpec((seq_len, ATT_W), lambda b: (b, 0)),
            pl.BlockSpec((1, HEAD_DIM), lambda b: (0, 0)),
            pl.BlockSpec((1, HEAD_DIM), lambda b: (0, 0)),
        ],
        out_specs=[
            pl.BlockSpec((seq_len, NA_W + GQA_QW), lambda b: (b, 0)),
            pl.BlockSpec((seq_len, GQA_KW), lambda b: (b, 0)),
        ],
        out_shape=[jax.ShapeDtypeStruct((n, NA_W + GQA_QW), BF16),
                   jax.ShapeDtypeStruct((n, GQA_KW), F32)],
        compiler_params=_cparams(("arbitrary",)),
        name="ctx_attn",
    )(z_att, qn_g.reshape(1, HEAD_DIM), kn_g.reshape(1, HEAD_DIM))


def _na_row_start(r, rows):
    return jnp.clip(r - NA_WIN_R // 2, 0, rows - NA_WIN_R)


def _lat_na_kernel(q_ref, k_ref, v_ref, kc_ref, vc_ref, tb_ref, o_ref, *, rows):
    r = pl.program_id(1)
    wr = NA_WIN_R
    rs = _na_row_start(r, rows)
    start = pl.multiple_of(rs * GRID_W, GRID_W)
    outs = []
    for h in range(NA_HEADS):
        hs = slice(h * HEAD_DIM, (h + 1) * HEAD_DIM)
        q = (q_ref[:, hs] * ATT_SCALE).astype(BF16)
        k = k_ref[pl.ds(start, wr * GRID_W), hs].astype(BF16)
        v = v_ref[pl.ds(start, wr * GRID_W), hs].astype(BF16)
        kc = kc_ref[0, :, hs].astype(BF16)
        vc = vc_ref[0, :, hs].astype(BF16)
        s_loc = _dot_nt(q, k) + tb_ref[0, h]
        s_ctx = _dot_nt(q, kc)
        outs.append(_softmax_pv([s_loc, s_ctx], [v, vc]))
    o_ref[...] = jnp.concatenate(outs, axis=-1).astype(BF16)


def _na_bias_table(rpb):
    wr = NA_WIN_R
    cols = np.arange(GRID_W)
    col_start = np.clip(cols - NA_WIN_C // 2, 0, GRID_W - NA_WIN_C)
    ck = np.arange(GRID_W)[None, :]
    valid = (ck >= col_start[:, None]) & (ck < col_start[:, None] + NA_WIN_C)
    rel = np.clip(ck - cols[:, None] + (NA_WIN_C - 1), 0, 2 * NA_WIN_C - 2)
    d = np.arange(wr)[:, None] + np.arange(wr)[None, :]
    t = rpb[:, d][:, :, :, rel]
    t = jnp.where(valid[None, None, None], t, NEG_BIG)
    t = t.transpose(1, 0, 3, 2, 4)
    return t.reshape(wr, NA_HEADS, GRID_W, wr * GRID_W)


def _lat_na(z_att, kc, vc, tb, seq_len):
    n = z_att.shape[0]
    nb = n // seq_len
    rows = seq_len // GRID_W
    assert rows >= NA_WIN_R
    wr = NA_WIN_R
    p = kc.shape[1]
    blk = rows
    return pl.pallas_call(
        functools.partial(_lat_na_kernel, rows=rows),
        grid=(nb, rows),
        in_specs=[
            pl.BlockSpec((GRID_W, NA_W), lambda b, r: (b * blk + r, 0)),
            pl.BlockSpec((seq_len, NA_W), lambda b, r: (b, 1)),
            pl.BlockSpec((seq_len, NA_W), lambda b, r: (b, 2)),
            pl.BlockSpec((1, p, NA_W), lambda b, r: (b, 0, 0)),
            pl.BlockSpec((1, p, NA_W), lambda b, r: (b, 0, 0)),
            pl.BlockSpec((1, NA_HEADS, GRID_W, wr * GRID_W),
                         lambda b, r: (_na_row_start(r, rows) - r + (wr - 1), 0, 0, 0)),
        ],
        out_specs=pl.BlockSpec((GRID_W, NA_W), lambda b, r: (b * blk + r, 0)),
        out_shape=jax.ShapeDtypeStruct((n, NA_W), BF16),
        compiler_params=_cparams(("arbitrary", "arbitrary")),
        name="lat_na",
    )(z_att, z_att, z_att, kc, vc, tb)


def _lat_gqa_kernel(q_ref, k_ref, v_ref, kc_ref, vc_ref, cq_ref, sq_ref, ck_ref, sk_ref, qn_ref, kn_ref, o_ref):
    ks, vs, kcs, vcs = [], [], [], []
    for g in range(GQA_KV_HEADS):
        gs = slice(g * HEAD_DIM, (g + 1) * HEAD_DIM)
        kg = _rope(_rms(k_ref[:, gs], kn_ref[...]), ck_ref[...], sk_ref[...])
        ks.append(kg.astype(BF16))
        vs.append(v_ref[:, gs].astype(BF16))
        kcs.append(kc_ref[0, :, gs].astype(BF16))
        vcs.append(vc_ref[0, :, gs].astype(BF16))
    outs = []
    for h in range(GQA_HEADS):
        g = h // GQA_REP
        qh = _rope(_rms(q_ref[:, h * HEAD_DIM:(h + 1) * HEAD_DIM], qn_ref[...]), cq_ref[...], sq_ref[...])
        q = (qh * ATT_SCALE).astype(BF16)
        outs.append(_softmax_pv([_dot_nt(q, ks[g]), _dot_nt(q, kcs[g])], [vs[g], vcs[g]]))
    o_ref[...] = jnp.concatenate(outs, axis=-1).astype(BF16)


def _rope_tables(seq_len):
    t = jnp.arange(seq_len)
    n_freq = HEAD_DIM // 4
    inv = ROPE_THETA ** (-jnp.arange(n_freq, dtype=F32) / n_freq)
    ang_r = (t // GRID_W).astype(F32)[:, None] * inv
    ang_c = (t % GRID_W).astype(F32)[:, None] * inv
    cr, sr, cc, sc = jnp.cos(ang_r), jnp.sin(ang_r), jnp.cos(ang_c), jnp.sin(ang_c)
    cos = jnp.concatenate([cr, cr, cc, cc], axis=-1)
    sin = jnp.concatenate([-sr, sr, -sc, sc], axis=-1)
    return cos, sin


def _lat_gqa(z_att, kc, vc, cos, sin, qn_g, kn_g, seq_len):
    n = z_att.shape[0]
    nb = n // seq_len
    tq = min(GQA_QBLOCK, seq_len)
    nqb = seq_len // tq
    p = kc.shape[1]
    qcol = (3 * NA_W) // GQA_QW
    kcol = (3 * NA_W + GQA_QW) // GQA_KW
    return pl.pallas_call(
        _lat_gqa_kernel,
        grid=(nb, nqb),
        in_specs=[
            pl.BlockSpec((tq, GQA_QW), lambda b, i: (b * nqb + i, qcol)),
            pl.BlockSpec((seq_len, GQA_KW), lambda b, i: (b, kcol)),
            pl.BlockSpec((seq_len, GQA_KW), lambda b, i: (b, kcol + 1)),
            pl.BlockSpec((1, p, GQA_KW), lambda b, i: (b, 0, 0)),
            pl.BlockSpec((1, p, GQA_KW), lambda b, i: (b, 0, 0)),
            pl.BlockSpec((tq, HEAD_DIM), lambda b, i: (i, 0)),
            pl.BlockSpec((tq, HEAD_DIM), lambda b, i: (i, 0)),
            pl.BlockSpec((seq_len, HEAD_DIM), lambda b, i: (0, 0)),
            pl.BlockSpec((seq_len, HEAD_DIM), lambda b, i: (0, 0)),
            pl.BlockSpec((1, HEAD_DIM), lambda b, i: (0, 0)),
            pl.BlockSpec((1, HEAD_DIM), lambda b, i: (0, 0)),
        ],
        out_specs=pl.BlockSpec((tq, GQA_QW), lambda b, i: (b * nqb + i, 0)),
        out_shape=jax.ShapeDtypeStruct((n, GQA_QW), BF16),
        compiler_params=_cparams(("arbitrary", "arbitrary")),
        name="lat_gqa",
    )(z_att, z_att, z_att, kc, vc, cos, sin, cos, sin, qn_g.reshape(1, HEAD_DIM), kn_g.reshape(1, HEAD_DIM))


def _layer_norm(x, g, b):
    mu = jnp.mean(x, axis=-1, keepdims=True)
    xc = x - mu
    var = jnp.mean(xc * xc, axis=-1, keepdims=True)
    return xc * lax.rsqrt(var + LN_EPS) * g + b


def _merge_kernel(x_ref, mod_ref, cp_ref, cpp_ref, cpn_ref, *rest, seq_len, n_att):
    att_refs = rest[:n_att]
    (wg_ref, wb_ref, wo_ref, pw_ref, cw_ref, cb_ref, ps_ref, lg_ref, lb_ref, wr_ref,
     x1_ref, xm2_ref, aff_ref, ucv_ref, upl_ref) = rest[n_att:]
    t = x_ref.shape[0]
    halo = POOL_HALO
    i = pl.program_id(0)
    pos0 = (i * t) % seq_len
    has_prev = pos0 > 0
    has_next = pos0 + t < seq_len

    x = x_ref[...]
    shift1, scale1, gate1 = mod_ref[0, 0:1, :], mod_ref[0, 1:2, :], mod_ref[0, 2:3, :]
    shift2, scale2 = mod_ref[0, 3:4, :], mod_ref[0, 4:5, :]
    xm = (x * (1.0 + scale1) + shift1).astype(BF16)

    bw = BRANCH_W
    cv_b = cp_ref[:, 0:bw]
    ucv_ref[halo:halo + t, :] = cp_ref[:, bw:2 * bw] * cp_ref[:, 2 * bw:3 * bw]
    upl_ref[halo:halo + t, :] = cp_ref[:, 3 * bw:4 * bw]
    ucv_ref[0:halo, :] = jnp.where(has_prev, cpp_ref[:, bw:2 * bw] * cpp_ref[:, 2 * bw:3 * bw], 0.0)
    upl_ref[0:halo, :] = jnp.where(has_prev, cpp_ref[:, 3 * bw:4 * bw], 0.0)
    ucv_ref[halo + t:, :] = jnp.where(has_next, cpn_ref[:, bw:2 * bw] * cpn_ref[:, 2 * bw:3 * bw], 0.0)
    upl_ref[halo + t:, :] = jnp.where(has_next, cpn_ref[:, 3 * bw:4 * bw], 0.0)

    conv = (cw_ref[0:1, :] * ucv_ref[halo - 1:halo - 1 + t, :]
            + cw_ref[1:2, :] * ucv_ref[halo:halo + t, :]
            + cw_ref[2:3, :] * ucv_ref[halo + 1:halo + 1 + t, :]) + cb_ref[...]
    o_conv = (cv_b * conv).astype(BF16)

    pos = (pos0 + lax.broadcasted_iota(I32, (t, 1), 0))
    mixed = []
    for g, win in enumerate(POOL_WINDOWS):
        gs = slice(g * POOL_GW, (g + 1) * POOL_GW)
        acc = None
        for dlt in range(-(win // 2), win // 2):
            term = upl_ref[halo + dlt:halo + dlt + t, gs]
            acc = term if acc is None else acc + term
        lo = jnp.maximum(pos - win // 2, 0)
        hi = jnp.minimum(pos + win // 2, seq_len)
        cnt = (hi - lo).astype(F32)
        pooled = acc / cnt - upl_ref[halo:halo + t, gs]
        mixed.append(jnp.dot(pooled.astype(BF16), pw_ref[g], preferred_element_type=F32))
    o_pool = (jnp.concatenate(mixed, axis=-1) * ps_ref[...]).astype(BF16)

    if n_att == 1:
        branches = [att_refs[0][:, 0:bw], att_refs[0][:, bw:2 * bw], o_conv, o_pool]
    else:
        branches = [att_refs[0][...], att_refs[1][...], o_conv, o_pool]

    merged = None
    for nb in range(N_BRANCH):
        zg = jnp.dot(xm, wg_ref[:, nb * D_MODEL:(nb + 1) * D_MODEL], preferred_element_type=F32)
        proj = jnp.dot(branches[nb], wb_ref[nb], preferred_element_type=F32)
        term = jax.nn.sigmoid(zg) * proj
        merged = term if merged is None else merged + term
    y = jnp.dot(merged.astype(BF16), wo_ref[...], preferred_element_type=F32)
    x1 = _layer_norm(DEEPNORM_ALPHA * x + gate1 * y, lg_ref[...], lb_ref[...])
    x1_ref[...] = x1
    xm2 = x1 * (1.0 + scale2) + shift2
    xm2_ref[...] = xm2
    logits = jnp.dot(xm2, wr_ref[...], precision=HIGHEST, preferred_element_type=F32)
    m = logits.max(axis=-1, keepdims=True)
    e = jnp.exp(logits - m)
    aff_ref[...] = e / e.sum(axis=-1, keepdims=True)


def _merge(x, mod, z_cp, atts, w_gates, w_branch, w_out, pool_w, conv_w, conv_b, pool_scale, ln_g, ln_b,
           w_router, seq_len):
    n = x.shape[0]
    t = min(TOK_TILE, seq_len)
    per = n // mod.shape[0] // t
    hb = t // POOL_HALO
    nhb = n // POOL_HALO
    row = lambda a: a.reshape(1, -1)
    const2 = lambda i: (0, 0)
    const3 = lambda i: (0, 0, 0)
    att_specs = [pl.BlockSpec((t, a.shape[1]), lambda i: (i, 0)) for a in atts]
    return pl.pallas_call(
        functools.partial(_merge_kernel, seq_len=seq_len, n_att=len(atts)),
        grid=(n // t,),
        in_specs=[
            pl.BlockSpec((t, D_MODEL), lambda i: (i, 0)),
            pl.BlockSpec((1, N_MOD, D_MODEL), lambda i: (i // per, 0, 0)),
            pl.BlockSpec((t, CP_W), lambda i: (i, 0)),
            pl.BlockSpec((POOL_HALO, CP_W), lambda i: (jnp.maximum(i * hb - 1, 0), 0)),
            pl.BlockSpec((POOL_HALO, CP_W), lambda i: (jnp.minimum((i + 1) * hb, nhb - 1), 0)),
            *att_specs,
            pl.BlockSpec((D_MODEL, GATE_W), const2),
            pl.BlockSpec((N_BRANCH, BRANCH_W, D_MODEL), const3),
            pl.BlockSpec((D_MODEL, D_MODEL), const2),
            pl.BlockSpec((len(POOL_WINDOWS), POOL_GW, POOL_GW), const3),
            pl.BlockSpec((CONV_K, BRANCH_W), const2),
            pl.BlockSpec((1, BRANCH_W), const2),
            pl.BlockSpec((1, BRANCH_W), const2),
            pl.BlockSpec((1, D_MODEL), const2),
            pl.BlockSpec((1, D_MODEL), const2),
            pl.BlockSpec((D_MODEL, N_EXPERTS), const2),
        ],
        out_specs=[
            pl.BlockSpec((t, D_MODEL), lambda i: (i, 0)),
            pl.BlockSpec((t, D_MODEL), lambda i: (i, 0)),
            pl.BlockSpec((t, N_EXPERTS), lambda i: (i, 0)),
        ],
        out_shape=[jax.ShapeDtypeStruct((n, D_MODEL), F32), jax.ShapeDtypeStruct((n, D_MODEL), F32),
                   jax.ShapeDtypeStruct((n, N_EXPERTS), F32)],
        scratch_shapes=[pltpu.VMEM((t + 2 * POOL_HALO, BRANCH_W), F32),
                        pltpu.VMEM((t + 2 * POOL_HALO, BRANCH_W), F32)],
        compiler_params=_cparams(("arbitrary",)),
        name="merge",
    )(x, mod, z_cp, z_cp, z_cp, *atts, w_gates, w_branch, w_out, pool_w, conv_w, row(conv_b), row(pool_scale),
      row(ln_g), row(ln_b), w_router)


def _topk_kernel(aff_ref, selw_ref, idx_ref, lct_ref, tot_ref, excl_ref, thr_ref, *, cap):
    ne, nc, _ = aff_ref.shape
    capf = float(cap)
    bits_all = pltpu.bitcast(aff_ref[...], I32)

    def search(i, cur):
        cand = cur | (jnp.int32(1) << (30 - i))
        cnt = jnp.sum((bits_all >= cand).astype(F32), axis=(1, 2), keepdims=True)
        return jnp.where(cnt >= capf, cand, cur)

    thr = lax.fori_loop(0, 31, search, jnp.zeros((ne, 1, 1), I32))
    thr_ref[...] = jnp.broadcast_to(thr, thr_ref.shape)

    jj = lax.broadcasted_iota(I32, (CHUNK, CHUNK), 0)
    kk = lax.broadcasted_iota(I32, (CHUNK, CHUNK), 1)
    ut_incl = (jj <= kk).astype(BF16)
    lt_incl = (kk <= jj).astype(BF16)
    cc = lax.broadcasted_iota(I32, (nc, nc), 0)
    dd = lax.broadcasted_iota(I32, (nc, nc), 1)
    cl_excl = (dd < cc).astype(BF16)
    cl_incl = (dd <= cc).astype(BF16)
    lane = lax.broadcasted_iota(I32, (1, CHUNK), 1)
    sub_c = lax.broadcasted_iota(I32, (nc, CHUNK), 0).astype(F32)

    def per_expert(e, carry):
        a = aff_ref[e]
        b = pltpu.bitcast(a, I32)
        t = thr_ref[e][0:1, :]
        gt = b > t
        eq = b == t
        need = capf - jnp.sum(gt.astype(F32), keepdims=True)
        eqf = eq.astype(F32)
        incl_eq = jnp.dot(eqf.astype(BF16), ut_incl, preferred_element_type=F32)
        tot_eq = jnp.broadcast_to(incl_eq[:, CHUNK - 1:CHUNK], (nc, CHUNK))
        cum_eq = jnp.dot(cl_excl, tot_eq.astype(BF16), preferred_element_type=F32)
        rank = cum_eq + incl_eq - eqf
        sel = jnp.logical_or(gt, jnp.logical_and(eq, rank < need))
        selw_ref[e] = jnp.where(sel, a, -1.0)

        selb = sel.astype(F32).astype(BF16)
        incl = jnp.dot(selb, ut_incl, preferred_element_type=F32)
        tot = jnp.broadcast_to(incl[:, CHUNK - 1:CHUNK], (nc, CHUNK))
        totb = tot.astype(BF16)
        cum_incl = jnp.dot(cl_incl, totb, preferred_element_type=F32)
        cum_excl = cum_incl - tot
        tot_ref[e] = tot.astype(I32)
        excl_ref[e] = cum_excl.astype(I32)
        incl_t = _dot_nt(lt_incl, selb)
        incl_tb = incl_t.astype(BF16)

        def per_block(sb, c2):
            s_row = (sb * CHUNK + lane).astype(F32)
            m = cum_incl <= s_row
            c_of_s = jnp.sum(m.astype(F32), axis=0, keepdims=True)
            excl_s = jnp.max(jnp.where(m, cum_incl, 0.0), axis=0, keepdims=True)
            onehot_t = (sub_c == c_of_s).astype(BF16)
            rows_t = jnp.dot(incl_tb, onehot_t, preferred_element_type=F32)
            s_local = s_row - excl_s
            t_local = jnp.sum((rows_t <= s_local).astype(F32), axis=0, keepdims=True)
            idx_ref[e, pl.ds(sb, 1), :] = (c_of_s * float(CHUNK) + t_local).astype(I32)
            return c2

        lax.fori_loop(0, cap // CHUNK, per_block, 0)

        def per_k(k, c2):
            row = jnp.sum((incl_t <= jnp.asarray(k, F32)).astype(F32), axis=0, keepdims=True)
            lct_ref[e, pl.ds(k, 1), :] = row.astype(I32)
            return c2

        lax.fori_loop(0, CHUNK, per_k, 0)
        return carry

    lax.fori_loop(0, ne, per_expert, 0)


def _topk(aff3, cap):
    ne, nc, _ = aff3.shape
    full3 = lambda shp: pl.BlockSpec(shp, lambda i: (0, 0, 0))
    return pl.pallas_call(
        functools.partial(_topk_kernel, cap=cap),
        grid=(1,),
        in_specs=[full3((ne, nc, CHUNK))],
        out_specs=[full3((ne, nc, CHUNK)), full3((ne, cap // CHUNK, CHUNK)), full3((ne, CHUNK, nc)),
                   full3((ne, nc, CHUNK)), full3((ne, nc, CHUNK))],
        out_shape=[jax.ShapeDtypeStruct((ne, nc, CHUNK), F32),
                   jax.ShapeDtypeStruct((ne, cap // CHUNK, CHUNK), I32),
                   jax.ShapeDtypeStruct((ne, CHUNK, nc), I32),
                   jax.ShapeDtypeStruct((ne, nc, CHUNK), I32),
                   jax.ShapeDtypeStruct((ne, nc, CHUNK), I32)],
        scratch_shapes=[pltpu.VMEM((ne, SUBLANES, CHUNK), I32)],
        compiler_params=_cparams(("arbitrary",)),
        name="topk",
    )(aff3)


def _row_gather_copy(src_hbm, dst, sem, row, j):
    return pltpu.make_async_copy(src_hbm.at[pl.ds(row, 1), :], dst.at[pl.ds(j, 1), :], sem)


def _ffn_kernel(idx_ref, idxn_ref, x_hbm, wg_ref, wu_ref, wd_ref, o_ref, xs_ref, sem_ref, *, steps):
    tm = xs_ref.shape[1]
    s = pl.program_id(0) * pl.num_programs(1) + pl.program_id(1)
    slot = s % 2

    def for_rows(idx_smem, buf_slot, fn):
        def body(j, c):
            fn(_row_gather_copy(x_hbm, xs_ref.at[buf_slot], sem_ref.at[buf_slot], idx_smem[0, 0, j], j))
            return c
        lax.fori_loop(0, tm, body, 0)

    @pl.when(s == 0)
    def _():
        for_rows(idx_ref, 0, lambda cp: cp.start())

    @pl.when(s + 1 < steps)
    def _():
        for_rows(idxn_ref, 1 - slot, lambda cp: cp.start())

    for_rows(idx_ref, slot, lambda cp: cp.wait())
    xs = xs_ref[slot].astype(BF16)
    hg = jnp.dot(xs, wg_ref[0], preferred_element_type=F32)
    hu = jnp.dot(xs, wu_ref[0], preferred_element_type=F32)
    hdn = (hg * jax.nn.sigmoid(hg) * hu).astype(BF16)
    o_ref[...] = jnp.dot(hdn, wd_ref[0], preferred_element_type=F32)


def _ffn(idx_tiles, xm2, w_gate, w_up, w_down, cap):
    tm = idx_tiles.shape[2]
    nt = cap // tm
    steps = N_EXPERTS * nt
    wspec = lambda: pl.BlockSpec((1, D_MODEL, EXPERT_FF), lambda e, i: (e, 0, 0))
    return pl.pallas_call(
        functools.partial(_ffn_kernel, steps=steps),
        grid=(N_EXPERTS, nt),
        in_specs=[
            pl.BlockSpec((1, 1, tm), lambda e, i: (e * nt + i, 0, 0), memory_space=pltpu.SMEM),
            pl.BlockSpec((1, 1, tm), lambda e, i: (jnp.minimum(e * nt + i + 1, steps - 1), 0, 0),
                         memory_space=pltpu.SMEM),
            pl.BlockSpec(memory_space=pl.ANY),
            wspec(), wspec(),
            pl.BlockSpec((1, EXPERT_FF, D_MODEL), lambda e, i: (e, 0, 0)),
        ],
        out_specs=pl.BlockSpec((tm, D_MODEL), lambda e, i: (e * nt + i, 0)),
        out_shape=jax.ShapeDtypeStruct((N_EXPERTS * cap, D_MODEL), F32),
        scratch_shapes=[pltpu.VMEM((2, tm, D_MODEL), F32), pltpu.SemaphoreType.DMA((2,))],
        compiler_params=_cparams(("arbitrary", "arbitrary")),
        name="ffn",
    )(idx_tiles, idx_tiles, xm2, w_gate, w_up, w_down)


def _combine_kernel(tot_ref, excl_ref, lc_ref, ye_hbm, x1_ref, mod_ref, selw_ref, lg_ref, lb_ref, o_ref,
                    contrib_ref, sem_ref, *, cap, chunks):
    i = pl.program_id(0)

    def for_each_row(fn):
        for e in range(N_EXPERTS):
            for c in range(chunks):
                base = (i * N_EXPERTS + e) * chunks + c
                cnt = tot_ref[base]
                first = excl_ref[base] + e * cap

                def body(j, carry, e=e, c=c, first=first):
                    tl = lc_ref[0, 0, (e * chunks + c) * CHUNK + j]
                    fn(pltpu.make_async_copy(ye_hbm.at[pl.ds(first + j, 1), :],
                                             contrib_ref.at[e, pl.ds(c * CHUNK + tl, 1), :], sem_ref.at[0]))
                    return carry

                lax.fori_loop(0, cnt, body, 0)

    for_each_row(lambda cp: cp.start())
    for_each_row(lambda cp: cp.wait())

    gate2 = mod_ref[0, 5:6, :]
    y = None
    for e in range(N_EXPERTS):
        w = selw_ref[:, e:e + 1]
        term = jnp.where(w >= 0.0, w * contrib_ref[e], 0.0)
        y = term if y is None else y + term
    o_ref[...] = _layer_norm(DEEPNORM_ALPHA * x1_ref[...] + gate2 * y, lg_ref[...], lb_ref[...])


def _combine(tot_flat, excl_flat, lc_tiles, ye, x1, mod, selw_tok, ln_g, ln_b, seq_len, cap):
    n = x1.shape[0]
    t = min(TOK_TILE, seq_len)
    per = n // mod.shape[0] // t
    chunks = t // CHUNK
    row = lambda a: a.reshape(1, -1)
    grid_spec = pltpu.PrefetchScalarGridSpec(
        num_scalar_prefetch=2,
        grid=(n // t,),
        in_specs=[
            pl.BlockSpec((1, 1, N_EXPERTS * chunks * CHUNK), lambda i, *_: (i, 0, 0), memory_space=pltpu.SMEM),
            pl.BlockSpec(memory_space=pl.ANY),
            pl.BlockSpec((t, D_MODEL), lambda i, *_: (i, 0)),
            pl.BlockSpec((1, N_MOD, D_MODEL), lambda i, *_: (i // per, 0, 0)),
            pl.BlockSpec((t, N_EXPERTS), lambda i, *_: (i, 0)),
            pl.BlockSpec((1, D_MODEL), lambda i, *_: (0, 0)),
            pl.BlockSpec((1, D_MODEL), lambda i, *_: (0, 0)),
        ],
        out_specs=pl.BlockSpec((t, D_MODEL), lambda i, *_: (i, 0)),
        scratch_shapes=[pltpu.VMEM((N_EXPERTS, t, D_MODEL), F32), pltpu.SemaphoreType.DMA((1,))],
    )
    return pl.pallas_call(
        functools.partial(_combine_kernel, cap=cap, chunks=chunks),
        grid_spec=grid_spec,
        out_shape=jax.ShapeDtypeStruct((n, D_MODEL), F32),
        compiler_params=_cparams(("arbitrary",)),
        name="combine",
    )(tot_flat, excl_flat, lc_tiles, ye, x1, mod, selw_tok, row(ln_g), row(ln_b))


def _moe(x1, xm2, aff, mod, w_gate, w_up, w_down, ln_g, ln_b, seq_len):
    n = x1.shape[0]
    cap = EC_CAPACITY * n // N_EXPERTS
    nc = n // CHUNK
    t = min(TOK_TILE, seq_len)
    chunks = t // CHUNK
    ntile = n // t
    aff3 = aff.T.reshape(N_EXPERTS, nc, CHUNK)
    selw3, idx3, lct, tot3, excl3 = _topk(aff3, cap)
    selw_tok = selw3.reshape(N_EXPERTS, n).T
    tm = min(FFN_TILE, cap)
    idx_tiles = idx3.reshape(N_EXPERTS * cap // tm, 1, tm)
    lc = lct.transpose(2, 0, 1).reshape(ntile, chunks, N_EXPERTS, CHUNK).transpose(0, 2, 1, 3)
    lc_tiles = lc.reshape(ntile, 1, N_EXPERTS * chunks * CHUNK)
    per_tile = lambda a: a[:, :, 0].reshape(N_EXPERTS, ntile, chunks).transpose(1, 0, 2).reshape(-1)
    ye = _ffn(idx_tiles, xm2, w_gate, w_up, w_down, cap)
    return _combine(per_tile(tot3), per_tile(excl3), lc_tiles, ye, x1, mod, selw_tok, ln_g, ln_b, seq_len, cap)


def kernel(x_prompt, x_sample, cache_na_k, cache_na_v, cache_gqa_k, cache_gqa_v, c, c_ctx, w_ada, b_ada, w_in,
           na_rpb, qn_g, kn_g, conv_w, conv_b, pool_w, pool_scale, w_branch, w_out, ln1_g, ln1_b, ln2_g, ln2_b,
           w_router, w_gate, w_up, w_down):
    depth = w_in.shape[0]
    bp, lp, _ = x_prompt.shape
    bs, ls, _ = x_sample.shape
    past = cache_na_k.shape[2]

    mods = _adaln(jnp.concatenate([c_ctx[None, :], c], axis=0), w_ada, b_ada)
    mods = mods.reshape(depth, 1 + bs, N_MOD, D_MODEL)
    cos, sin = _rope_tables(ls)

    yp = x_prompt.reshape(bp * lp, D_MODEL)
    ys = x_sample.reshape(bs * ls, D_MODEL)
    na_k_l, na_v_l, g_k_l, g_v_l = [], [], [], []
    for l in range(depth):
        mod_c = mods[l, 0:1]
        mod_s = mods[l, 1:]
        w_a = w_in[l, :, :ATT_W + CP_W].astype(BF16)
        w_gates = w_in[l, :, ATT_W + CP_W:].astype(BF16)
        w_br = w_branch[l].astype(BF16)
        w_o = w_out[l].astype(BF16)
        p_w = pool_w[l].astype(BF16)
        wg, wu, wd = w_gate[l].astype(BF16), w_up[l].astype(BF16), w_down[l].astype(BF16)
        merge_w = (w_gates, w_br, w_o, p_w, conv_w[l], conv_b[l], pool_scale[l], ln1_g[l], ln1_b[l], w_router[l])

        z_att, z_cp = _inproj(yp, mod_c, w_a, bp * lp)
        o_att, gk = _ctx_attn(z_att, qn_g[l], kn_g[l], lp)
        na_k_l.append(z_att[:, NA_W:2 * NA_W].reshape(bp, lp, NA_HEADS, HEAD_DIM))
        na_v_l.append(z_att[:, 2 * NA_W:3 * NA_W].reshape(bp, lp, NA_HEADS, HEAD_DIM))
        g_k_l.append(gk.reshape(bp, lp, GQA_KV_HEADS, HEAD_DIM))
        g_v_l.append(z_att[:, ATT_W - GQA_KW:].reshape(bp, lp, GQA_KV_HEADS, HEAD_DIM))
        x1, xm2, aff = _merge(yp, mod_c, z_cp, [o_att], *merge_w, lp)
        yp = _moe(x1, xm2, aff, mod_c, wg, wu, wd, ln2_g[l], ln2_b[l], lp)

        z_att, z_cp = _inproj(ys, mod_s, w_a, ls)
        tb = _na_bias_table(na_rpb[l])
        o_na = _lat_na(z_att, cache_na_k[:, l].reshape(bs, past, NA_W), cache_na_v[:, l].reshape(bs, past, NA_W),
                       tb, ls)
        o_gqa = _lat_gqa(z_att, cache_gqa_k[:, l].reshape(bs, past, GQA_KW),
                         cache_gqa_v[:, l].reshape(bs, past, GQA_KW), cos, sin, qn_g[l], kn_g[l], ls)
        x1, xm2, aff = _merge(ys, mod_s, z_cp, [o_na, o_gqa], *merge_w, ls)
        ys = _moe(x1, xm2, aff, mod_s, wg, wu, wd, ln2_g[l], ln2_b[l], ls)

    return (yp.reshape(bp, lp, D_MODEL), ys.reshape(bs, ls, D_MODEL),
            jnp.stack(na_k_l, axis=1), jnp.stack(na_v_l, axis=1),
            jnp.stack(g_k_l, axis=1), jnp.stack(g_v_l, axis=1))
```

```python
import functools

import numpy as np
import jax
import jax.numpy as jnp
from jax import lax
from jax.experimental import pallas as pl
from jax.experimental.pallas import tpu as pltpu

F32 = jnp.float32
BF16 = jnp.bfloat16
I32 = jnp.int32
HIGHEST = lax.Precision.HIGHEST

D_MODEL = 1024
NOMINAL_DEPTH = 4
GRID_W = 64
HEAD_DIM = 64
NA_HEADS = 8
NA_WIN_R = 8
NA_WIN_C = 16
GQA_HEADS = 8
GQA_KV_HEADS = 2
GQA_REP = GQA_HEADS // GQA_KV_HEADS
BRANCH_W = 512
N_BRANCH = 4
CONV_K = 3
POOL_WINDOWS = (2, 4, 8, 16)
POOL_GW = BRANCH_W // len(POOL_WINDOWS)
POOL_HALO = max(POOL_WINDOWS) // 2
N_EXPERTS = 16
EXPERT_FF = 1024
EC_CAPACITY = 2
ROPE_THETA = 10000.0
LN_EPS = 1e-6
RMS_EPS = 1e-6
N_MOD = 6
DEEPNORM_ALPHA = (2 * NOMINAL_DEPTH) ** 0.25
NA_W = NA_HEADS * HEAD_DIM
GQA_QW = GQA_HEADS * HEAD_DIM
GQA_KW = GQA_KV_HEADS * HEAD_DIM
ATT_W = 3 * NA_W + GQA_QW + 2 * GQA_KW
CP_W = 4 * BRANCH_W
GATE_W = N_BRANCH * D_MODEL
ATT_SCALE = HEAD_DIM ** -0.5
NEG_BIG = -1e30
PAIR_W = 2 * HEAD_DIM

LANES = 128
SUBLANES = 8
VMEM_LIMIT = 56 * 1024 * 1024

TOK_TILE_IN = 512
MERGE_TILE = 512
TOK_TILE = 256
CHUNK = LANES
FFN_TILE = 256
GQA_QBLOCK = 512


def _cparams(sem):
    return pltpu.CompilerParams(dimension_semantics=sem, vmem_limit_bytes=VMEM_LIMIT)


def _adaln_kernel(cv_ref, w_ref, b_ref, o_ref):
    cv = cv_ref[...]
    s = cv * jax.nn.sigmoid(cv)
    o_ref[0] = jnp.dot(s, w_ref[0], precision=HIGHEST, preferred_element_type=F32) + b_ref[0]


def _adaln(cvecs, w_ada, b_ada):
    depth = w_ada.shape[0]
    r = cvecs.shape[0]
    tn = 1536
    nw = N_MOD * D_MODEL
    return pl.pallas_call(
        _adaln_kernel,
        grid=(depth, nw // tn),
        in_specs=[
            pl.BlockSpec((r, D_MODEL), lambda l, j: (0, 0)),
            pl.BlockSpec((1, D_MODEL, tn), lambda l, j: (l, 0, j)),
            pl.BlockSpec((1, 1, tn), lambda l, j: (l, 0, j)),
        ],
        out_specs=pl.BlockSpec((1, r, tn), lambda l, j: (l, 0, j)),
        out_shape=jax.ShapeDtypeStruct((depth, r, nw), F32),
        compiler_params=_cparams(("arbitrary", "arbitrary")),
        name="adaln",
    )(cvecs, w_ada, b_ada.reshape(depth, 1, nw))


def _lane_lo(shape):
    return (lax.broadcasted_iota(I32, shape, len(shape) - 1) % PAIR_W) < HEAD_DIM


def _pair_rms(x2, g2):
    lo = _lane_lo(x2.shape)
    sq = x2 * x2
    ss_lo = jnp.sum(jnp.where(lo, sq, 0.0), axis=-1, keepdims=True)
    ss_hi = jnp.sum(jnp.where(lo, 0.0, sq), axis=-1, keepdims=True)
    inv = lax.rsqrt(jnp.where(lo, ss_lo, ss_hi) * (1.0 / HEAD_DIM) + RMS_EPS)
    return x2 * inv * g2


def _inproj_kernel(x_ref, mod_ref, w_ref, *rest, with_state):
    if with_state:
        kn_ref, att_ref, cp_ref, nak_ref, nav_ref, gk_ref, gv_ref = rest
    else:
        att_ref, cp_ref = rest
    shift = mod_ref[0, 0:1, :]
    scale = mod_ref[0, 1:2, :]
    xm = (x_ref[...] * (1.0 + scale) + shift).astype(BF16)
    z = jnp.dot(xm, w_ref[...], preferred_element_type=F32)
    att_ref[...] = z[:, :ATT_W].astype(BF16)
    cp_ref[...] = z[:, ATT_W:]
    if with_state:
        koff = 3 * NA_W + GQA_QW
        nak_ref[...] = z[:, NA_W:2 * NA_W]
        nav_ref[...] = z[:, 2 * NA_W:3 * NA_W]
        gk_ref[...] = _pair_rms(z[:, koff:koff + GQA_KW], kn_ref[...])
        gv_ref[...] = z[:, koff + GQA_KW:koff + 2 * GQA_KW]


def _inproj(x, mod, w_a, seq_len, kn2=None):
    n = x.shape[0]
    t = min(TOK_TILE_IN, seq_len)
    per = n // mod.shape[0] // t
    with_state = kn2 is not None
    tile = lambda w: pl.BlockSpec((t, w), lambda i: (i, 0))
    in_specs = [
        pl.BlockSpec((t, D_MODEL), lambda i: (i, 0)),
        pl.BlockSpec((1, N_MOD, D_MODEL), lambda i: (i // per, 0, 0)),
        pl.BlockSpec((D_MODEL, ATT_W + CP_W), lambda i: (0, 0)),
    ]
    out_specs = [tile(ATT_W), tile(CP_W)]
    out_shape = [jax.ShapeDtypeStruct((n, ATT_W), BF16), jax.ShapeDtypeStruct((n, CP_W), F32)]
    args = [x, mod, w_a]
    if with_state:
        in_specs.append(pl.BlockSpec((1, GQA_KW), lambda i: (0, 0)))
        args.append(kn2)
        for w in (NA_W, NA_W, GQA_KW, GQA_KW):
            out_specs.append(tile(w))
            out_shape.append(jax.ShapeDtypeStruct((n, w), F32))
    return pl.pallas_call(
        functools.partial(_inproj_kernel, with_state=with_state),
        grid=(n // t,),
        in_specs=in_specs,
        out_specs=out_specs,
        out_shape=out_shape,
        compiler_params=_cparams(("arbitrary",)),
        name="inproj",
    )(*args)


def _pair_rope(x2, cos2, sin2):
    q = HEAD_DIM // 4
    first = (lax.broadcasted_iota(I32, x2.shape, 1) % (2 * q)) < q
    swapped = jnp.where(first, pltpu.roll(x2, PAIR_W - q, 1), pltpu.roll(x2, q, 1))
    return x2 * cos2 + swapped * sin2


def _dup_head(x2, g):
    lo = _lane_lo(x2.shape)
    other = pltpu.roll(x2, HEAD_DIM, 1)
    return jnp.where(lo, x2, other) if g == 0 else jnp.where(lo, other, x2)


def _dot_nt(a, b):
    return lax.dot_general(a, b, (((1,), (1,)), ((), ())), preferred_element_type=F32)


def _softmax_pv(scores, values):
    m = scores[0].max(axis=-1, keepdims=True)
    for s in scores[1:]:
        m = jnp.maximum(m, s.max(axis=-1, keepdims=True))
    l = None
    o = None
    for s, v in zip(scores, values):
        p = jnp.exp(s - m)
        ls = p.sum(axis=-1, keepdims=True)
        os_ = jnp.dot(p.astype(BF16), v, preferred_element_type=F32)
        l = ls if l is None else l + ls
        o = os_ if o is None else o + os_
    return o / l


def _pair_attn(q2, keys, values, biases=None):
    lo = _lane_lo(q2.shape)
    halves = []
    for half in range(2):
        qh = jnp.where(lo if half == 0 else jnp.logical_not(lo), q2, 0.0).astype(BF16)
        scores = []
        for bi, k2 in enumerate(keys):
            s = _dot_nt(qh, k2)
            if biases is not None and biases[bi] is not None:
                s = s + biases[bi][half]
            scores.append(s)
        halves.append(_softmax_pv(scores, values))
    return jnp.where(lo, halves[0], halves[1])


def _ctx_attn_kernel(z_ref, qn_ref, kn_ref, o_ref):
    for p in range(NA_HEADS // 2):
        ps = slice(p * PAIR_W, (p + 1) * PAIR_W)
        q2 = z_ref[:, ps].astype(F32) * ATT_SCALE
        k2 = z_ref[:, NA_W + p * PAIR_W:NA_W + (p + 1) * PAIR_W]
        v2 = z_ref[:, 2 * NA_W + p * PAIR_W:2 * NA_W + (p + 1) * PAIR_W]
        o_ref[:, ps] = _pair_attn(q2, [k2], [v2]).astype(BF16)
    qoff = 3 * NA_W
    koff = qoff + GQA_QW
    voff = koff + GQA_KW
    kn2 = _pair_rms(z_ref[:, koff:koff + GQA_KW].astype(F32), kn_ref[...])
    v2 = z_ref[:, voff:voff + GQA_KW].astype(F32)
    kd = [_dup_head(kn2, g).astype(BF16) for g in range(GQA_KV_HEADS)]
    vd = [_dup_head(v2, g).astype(BF16) for g in range(GQA_KV_HEADS)]
    for p in range(GQA_HEADS // 2):
        g = (2 * p) // GQA_REP
        q2 = _pair_rms(z_ref[:, qoff + p * PAIR_W:qoff + (p + 1) * PAIR_W].astype(F32), qn_ref[...]) * ATT_SCALE
        o_ref[:, NA_W + p * PAIR_W:NA_W + (p + 1) * PAIR_W] = _pair_attn(q2, [kd[g]], [vd[g]]).astype(BF16)


def _ctx_attn(z_att, qn2, kn2, seq_len):
    n = z_att.shape[0]
    return pl.pallas_call(
        _ctx_attn_kernel,
        grid=(n // seq_len,),
        in_specs=[
            pl.BlockSpec((seq_len, ATT_W), lambda b: (b, 0)),
            pl.BlockSpec((1, PAIR_W), lambda b: (0, 0)),
            pl.BlockSpec((1, PAIR_W), lambda b: (0, 0)),
        ],
        out_specs=pl.BlockSpec((seq_len, NA_W + GQA_QW), lambda b: (b, 0)),
        out_shape=jax.ShapeDtypeStruct((n, NA_W + GQA_QW), BF16),
        compiler_params=_cparams(("arbitrary",)),
        name="ctx_attn",
    )(z_att, qn2, kn2)


def _na_row_start(r, rows):
    return jnp.clip(r - NA_WIN_R // 2, 0, rows - NA_WIN_R)


def _lat_na_kernel(q_ref, k_ref, v_ref, kc_ref, vc_ref, tb_ref, o_ref, *, rows):
    r = pl.program_id(1)
    wr = NA_WIN_R
    rs = _na_row_start(r, rows)
    start = pl.multiple_of(rs * GRID_W, GRID_W)
    for p in range(NA_HEADS // 2):
        ps = slice(p * PAIR_W, (p + 1) * PAIR_W)
        q2 = q_ref[:, ps].astype(F32) * ATT_SCALE
        k2 = k_ref[pl.ds(start, wr * GRID_W), ps]
        v2 = v_ref[pl.ds(start, wr * GRID_W), ps]
        kc2 = kc_ref[0, :, ps]
        vc2 = vc_ref[0, :, ps]
        bias = (tb_ref[0, 2 * p], tb_ref[0, 2 * p + 1])
        o_ref[:, ps] = _pair_attn(q2, [k2, kc2], [v2, vc2], [bias, None]).astype(BF16)


def _na_bias_table(rpb):
    wr = NA_WIN_R
    cols = np.arange(GRID_W)
    col_start = np.clip(cols - NA_WIN_C // 2, 0, GRID_W - NA_WIN_C)
    ck = np.arange(GRID_W)[None, :]
    valid = (ck >= col_start[:, None]) & (ck < col_start[:, None] + NA_WIN_C)
    rel = np.clip(ck - cols[:, None] + (NA_WIN_C - 1), 0, 2 * NA_WIN_C - 2)
    d = np.arange(wr)[:, None] + np.arange(wr)[None, :]
    t = rpb[:, d][:, :, :, rel]
    t = jnp.where(valid[None, None, None], t, NEG_BIG)
    t = t.transpose(1, 0, 3, 2, 4)
    return t.reshape(wr, NA_HEADS, GRID_W, wr * GRID_W)


def _lat_na(z_att, kc, vc, tb, seq_len):
    n = z_att.shape[0]
    nb = n // seq_len
    rows = seq_len // GRID_W
    assert rows >= NA_WIN_R
    wr = NA_WIN_R
    p = kc.shape[1]
    blk = rows
    return pl.pallas_call(
        functools.partial(_lat_na_kernel, rows=rows),
        grid=(nb, rows),
        in_specs=[
            pl.BlockSpec((GRID_W, NA_W), lambda b, r: (b * blk + r, 0)),
            pl.BlockSpec((seq_len, NA_W), lambda b, r: (b, 1)),
            pl.BlockSpec((seq_len, NA_W), lambda b, r: (b, 2)),
            pl.BlockSpec((1, p, NA_W), lambda b, r: (b, 0, 0)),
            pl.BlockSpec((1, p, NA_W), lambda b, r: (b, 0, 0)),
            pl.BlockSpec((1, NA_HEADS, GRID_W, wr * GRID_W),
                         lambda b, r: (_na_row_start(r, rows) - r + (wr - 1), 0, 0, 0)),
        ],
        out_specs=pl.BlockSpec((GRID_W, NA_W), lambda b, r: (b * blk + r, 0)),
        out_shape=jax.ShapeDtypeStruct((n, NA_W), BF16),
        compiler_params=_cparams(("arbitrary", "arbitrary")),
        name="lat_na",
    )(z_att, z_att, z_att, kc, vc, tb)


def _lat_gqa_kernel(q_ref, k_ref, v_ref, kc_ref, vc_ref, cq_ref, sq_ref, ck_ref, sk_ref, qn_ref, kn_ref, o_ref):
    kn2 = _pair_rope(_pair_rms(k_ref[...].astype(F32), kn_ref[...]), ck_ref[...], sk_ref[...])
    v2 = v_ref[...].astype(F32)
    kc2 = kc_ref[0].astype(F32)
    vc2 = vc_ref[0].astype(F32)
    kd, vd, kcd, vcd = [], [], [], []
    for g in range(GQA_KV_HEADS):
        kd.append(_dup_head(kn2, g).astype(BF16))
        vd.append(_dup_head(v2, g).astype(BF16))
        kcd.append(_dup_head(kc2, g).astype(BF16))
        vcd.append(_dup_head(vc2, g).astype(BF16))
    for p in range(GQA_HEADS // 2):
        g = (2 * p) // GQA_REP
        ps = slice(p * PAIR_W, (p + 1) * PAIR_W)
        q2 = _pair_rope(_pair_rms(q_ref[:, ps].astype(F32), qn_ref[...]), cq_ref[...], sq_ref[...]) * ATT_SCALE
        o_ref[:, ps] = _pair_attn(q2, [kd[g], kcd[g]], [vd[g], vcd[g]]).astype(BF16)


def _rope_tables(seq_len):
    t = jnp.arange(seq_len)
    n_freq = HEAD_DIM // 4
    inv = ROPE_THETA ** (-jnp.arange(n_freq, dtype=F32) / n_freq)
    ang_r = (t // GRID_W).astype(F32)[:, None] * inv
    ang_c = (t % GRID_W).astype(F32)[:, None] * inv
    cr, sr, cc, sc = jnp.cos(ang_r), jnp.sin(ang_r), jnp.cos(ang_c), jnp.sin(ang_c)
    cos = jnp.concatenate([cr, cr, cc, cc] * 2, axis=-1)
    sin = jnp.concatenate([-sr, sr, -sc, sc] * 2, axis=-1)
    return cos, sin


def _lat_gqa(z_att, kc, vc, cos, sin, qn2, kn2, seq_len):
    n = z_att.shape[0]
    nb = n // seq_len
    tq = min(GQA_QBLOCK, seq_len)
    nqb = seq_len // tq
    p = kc.shape[1]
    qcol = (3 * NA_W) // GQA_QW
    kcol = (3 * NA_W + GQA_QW) // GQA_KW
    return pl.pallas_call(
        _lat_gqa_kernel,
        grid=(nb, nqb),
        in_specs=[
            pl.BlockSpec((tq, GQA_QW), lambda b, i: (b * nqb + i, qcol)),
            pl.BlockSpec((seq_len, GQA_KW), lambda b, i: (b, kcol)),
            pl.BlockSpec((seq_len, GQA_KW), lambda b, i: (b, kcol + 1)),
            pl.BlockSpec((1, p, GQA_KW), lambda b, i: (b, 0, 0)),
            pl.BlockSpec((1, p, GQA_KW), lambda b, i: (b, 0, 0)),
            pl.BlockSpec((tq, PAIR_W), lambda b, i: (i, 0)),
            pl.BlockSpec((tq, PAIR_W), lambda b, i: (i, 0)),
            pl.BlockSpec((seq_len, PAIR_W), lambda b, i: (0, 0)),
            pl.BlockSpec((seq_len, PAIR_W), lambda b, i: (0, 0)),
            pl.BlockSpec((1, PAIR_W), lambda b, i: (0, 0)),
            pl.BlockSpec((1, PAIR_W), lambda b, i: (0, 0)),
        ],
        out_specs=pl.BlockSpec((tq, GQA_QW), lambda b, i: (b * nqb + i, 0)),
        out_shape=jax.ShapeDtypeStruct((n, GQA_QW), BF16),
        compiler_params=_cparams(("arbitrary", "arbitrary")),
        name="lat_gqa",
    )(z_att, z_att, z_att, kc, vc, cos, sin, cos, sin, qn2, kn2)


def _layer_norm(x, g, b):
    mu = jnp.mean(x, axis=-1, keepdims=True)
    xc = x - mu
    var = jnp.mean(xc * xc, axis=-1, keepdims=True)
    return xc * lax.rsqrt(var + LN_EPS) * g + b


def _merge_kernel(x_ref, mod_ref, cp_ref, cpp_ref, cpn_ref, *rest, seq_len, n_att):
    att_refs = rest[:n_att]
    (wg_ref, wb_ref, wo_ref, pw_ref, cw_ref, cb_ref, ps_ref, lg_ref, lb_ref, wr_ref,
     x1_ref, xm2_ref, aff_ref, ucv_ref, upl_ref) = rest[n_att:]
    t = x_ref.shape[0]
    halo = POOL_HALO
    i = pl.program_id(0)
    pos0 = (i * t) % seq_len
    has_prev = pos0 > 0
    has_next = pos0 + t < seq_len

    x = x_ref[...]
    shift1, scale1, gate1 = mod_ref[0, 0:1, :], mod_ref[0, 1:2, :], mod_ref[0, 2:3, :]
    shift2, scale2 = mod_ref[0, 3:4, :], mod_ref[0, 4:5, :]
    xm = (x * (1.0 + scale1) + shift1).astype(BF16)

    bw = BRANCH_W
    cv_b = cp_ref[:, 0:bw]
    ucv_ref[halo:halo + t, :] = cp_ref[:, bw:2 * bw] * cp_ref[:, 2 * bw:3 * bw]
    upl_ref[halo:halo + t, :] = cp_ref[:, 3 * bw:4 * bw]
    ucv_ref[0:halo, :] = jnp.where(has_prev, cpp_ref[:, bw:2 * bw] * cpp_ref[:, 2 * bw:3 * bw], 0.0)
    upl_ref[0:halo, :] = jnp.where(has_prev, cpp_ref[:, 3 * bw:4 * bw], 0.0)
    ucv_ref[halo + t:, :] = jnp.where(has_next, cpn_ref[:, bw:2 * bw] * cpn_ref[:, 2 * bw:3 * bw], 0.0)
    upl_ref[halo + t:, :] = jnp.where(has_next, cpn_ref[:, 3 * bw:4 * bw], 0.0)

    conv = (cw_ref[0:1, :] * ucv_ref[halo - 1:halo - 1 + t, :]
            + cw_ref[1:2, :] * ucv_ref[halo:halo + t, :]
            + cw_ref[2:3, :] * ucv_ref[halo + 1:halo + 1 + t, :]) + cb_ref[...]
    o_conv = (cv_b * conv).astype(BF16)

    pos = (pos0 + lax.broadcasted_iota(I32, (t, 1), 0))
    mixed = []
    for g, win in enumerate(POOL_WINDOWS):
        gs = slice(g * POOL_GW, (g + 1) * POOL_GW)
        acc = None
        for dlt in range(-(win // 2), win // 2):
            term = upl_ref[halo + dlt:halo + dlt + t, gs]
            acc = term if acc is None else acc + term
        lo = jnp.maximum(pos - win // 2, 0)
        hi = jnp.minimum(pos + win // 2, seq_len)
        cnt = (hi - lo).astype(F32)
        pooled = acc / cnt - upl_ref[halo:halo + t, gs]
        mixed.append(jnp.dot(pooled.astype(BF16), pw_ref[g], preferred_element_type=F32))
    o_pool = (jnp.concatenate(mixed, axis=-1) * ps_ref[...]).astype(BF16)

    if n_att == 1:
        branches = [att_refs[0][:, 0:bw], att_refs[0][:, bw:2 * bw], o_conv, o_pool]
    else:
        branches = [att_refs[0][...], att_refs[1][...], o_conv, o_pool]

    merged = None
    for nb in range(N_BRANCH):
        zg = jnp.dot(xm, wg_ref[:, nb * D_MODEL:(nb + 1) * D_MODEL], preferred_element_type=F32)
        proj = jnp.dot(branches[nb], wb_ref[nb], preferred_element_type=F32)
        term = jax.nn.sigmoid(zg) * proj
        merged = term if merged is None else merged + term
    y = jnp.dot(merged.astype(BF16), wo_ref[...], preferred_element_type=F32)
    x1 = _layer_norm(DEEPNORM_ALPHA * x + gate1 * y, lg_ref[...], lb_ref[...])
    x1_ref[...] = x1
    xm2 = x1 * (1.0 + scale2) + shift2
    xm2_ref[...] = xm2
    logits = jnp.dot(xm2, wr_ref[...], precision=HIGHEST, preferred_element_type=F32)
    m = logits.max(axis=-1, keepdims=True)
    e = jnp.exp(logits - m)
    aff_ref[...] = e / e.sum(axis=-1, keepdims=True)


def _merge(x, mod, z_cp, atts, w_gates, w_branch, w_out, pool_w, conv_w, conv_b, pool_scale, ln_g, ln_b,
           w_router, seq_len):
    n = x.shape[0]
    t = min(MERGE_TILE, seq_len)
    per = n // mod.shape[0] // t
    hb = t // POOL_HALO
    nhb = n // POOL_HALO
    row = lambda a: a.reshape(1, -1)
    const2 = lambda i: (0, 0)
    const3 = lambda i: (0, 0, 0)
    once = lambda shp, imap: pl.BlockSpec(shp, imap, pipeline_mode=pl.Buffered(1))
    att_specs = [pl.BlockSpec((t, a.shape[1]), lambda i: (i, 0)) for a in atts]
    return pl.pallas_call(
        functools.partial(_merge_kernel, seq_len=seq_len, n_att=len(atts)),
        grid=(n // t,),
        in_specs=[
            pl.BlockSpec((t, D_MODEL), lambda i: (i, 0)),
            pl.BlockSpec((1, N_MOD, D_MODEL), lambda i: (i // per, 0, 0)),
            pl.BlockSpec((t, CP_W), lambda i: (i, 0)),
            pl.BlockSpec((POOL_HALO, CP_W), lambda i: (jnp.maximum(i * hb - 1, 0), 0)),
            pl.BlockSpec((POOL_HALO, CP_W), lambda i: (jnp.minimum((i + 1) * hb, nhb - 1), 0)),
            *att_specs,
            once((D_MODEL, GATE_W), const2),
            once((N_BRANCH, BRANCH_W, D_MODEL), const3),
            once((D_MODEL, D_MODEL), const2),
            pl.BlockSpec((len(POOL_WINDOWS), POOL_GW, POOL_GW), const3),
            pl.BlockSpec((CONV_K, BRANCH_W), const2),
            pl.BlockSpec((1, BRANCH_W), const2),
            pl.BlockSpec((1, BRANCH_W), const2),
            pl.BlockSpec((1, D_MODEL), const2),
            pl.BlockSpec((1, D_MODEL), const2),
            pl.BlockSpec((D_MODEL, N_EXPERTS), const2),
        ],
        out_specs=[
            pl.BlockSpec((t, D_MODEL), lambda i: (i, 0)),
            pl.BlockSpec((t, D_MODEL), lambda i: (i, 0)),
            pl.BlockSpec((t, N_EXPERTS), lambda i: (i, 0)),
        ],
        out_shape=[jax.ShapeDtypeStruct((n, D_MODEL), F32), jax.ShapeDtypeStruct((n, D_MODEL), F32),
                   jax.ShapeDtypeStruct((n, N_EXPERTS), F32)],
        scratch_shapes=[pltpu.VMEM((t + 2 * POOL_HALO, BRANCH_W), F32),
                        pltpu.VMEM((t + 2 * POOL_HALO, BRANCH_W), F32)],
        compiler_params=_cparams(("arbitrary",)),
        name="merge",
    )(x, mod, z_cp, z_cp, z_cp, *atts, w_gates, w_branch, w_out, pool_w, conv_w, row(conv_b), row(pool_scale),
      row(ln_g), row(ln_b), w_router)


def _topk_kernel(aff_ref, idx_ref, dst_ref, gp_ref, cnt_ref, thr_ref, rank_ref, *, cap, plane_stride):
    ne, nc, _ = aff_ref.shape
    capf = float(cap)
    bits_all = pltpu.bitcast(aff_ref[...], I32)

    def search(i, cur):
        cand = cur | (jnp.int32(1) << (30 - i))
        cnt = jnp.sum((bits_all >= cand).astype(F32), axis=(1, 2), keepdims=True)
        return jnp.where(cnt >= capf, cand, cur)

    thr = lax.fori_loop(0, 31, search, jnp.zeros((ne, 1, 1), I32))
    thr_ref[...] = jnp.broadcast_to(thr, thr_ref.shape)

    jj = lax.broadcasted_iota(I32, (CHUNK, CHUNK), 0)
    kk = lax.broadcasted_iota(I32, (CHUNK, CHUNK), 1)
    ut_incl = (jj <= kk).astype(BF16)
    lt_incl = (kk <= jj).astype(BF16)
    cc = lax.broadcasted_iota(I32, (nc, nc), 0)
    dd = lax.broadcasted_iota(I32, (nc, nc), 1)
    cl_excl = (dd < cc).astype(BF16)
    cl_incl = (dd <= cc).astype(BF16)
    eye = (jj == kk).astype(BF16)
    lane = lax.broadcasted_iota(I32, (1, CHUNK), 1)
    sub_c = lax.broadcasted_iota(I32, (nc, CHUNK), 0).astype(F32)
    sub_j = jj.astype(F32)
    rank_ref[...] = jnp.zeros(rank_ref.shape, F32)
    gp_ref[...] = jnp.full(gp_ref.shape, -1.0, F32)

    def per_expert(e, carry):
        a = aff_ref[e]
        b = pltpu.bitcast(a, I32)
        t = thr_ref[e][0:1, :]
        gt = b > t
        eq = b == t
        need = capf - jnp.sum(gt.astype(F32), keepdims=True)
        eqf = eq.astype(F32)
        incl_eq = jnp.dot(eqf.astype(BF16), ut_incl, preferred_element_type=F32)
        tot_eq = jnp.broadcast_to(incl_eq[:, CHUNK - 1:CHUNK], (nc, CHUNK))
        cum_eq = jnp.dot(cl_excl, tot_eq.astype(BF16), preferred_element_type=F32)
        rank = cum_eq + incl_eq - eqf
        sel = jnp.logical_or(gt, jnp.logical_and(eq, rank < need))
        self_ = sel.astype(F32)

        plane = rank_ref[...]
        rank_ref[...] = plane + self_
        for k in range(ne):
            gp_ref[k] = jnp.where(jnp.logical_and(sel, plane == float(k)), a, gp_ref[k])
        plane_tb = _dot_nt(eye, plane.astype(BF16)).astype(BF16)

        selb = self_.astype(BF16)
        incl = jnp.dot(selb, ut_incl, preferred_element_type=F32)
        tot = jnp.broadcast_to(incl[:, CHUNK - 1:CHUNK], (nc, CHUNK))
        cum_incl = jnp.dot(cl_incl, tot.astype(BF16), preferred_element_type=F32)
        incl_tb = _dot_nt(lt_incl, selb).astype(BF16)

        def per_block(sb, c2):
            s_row = (sb * CHUNK + lane).astype(F32)
            m = cum_incl <= s_row
            c_of_s = jnp.sum(m.astype(F32), axis=0, keepdims=True)
            excl_s = jnp.max(jnp.where(m, cum_incl, 0.0), axis=0, keepdims=True)
            onehot_t = (sub_c == c_of_s).astype(BF16)
            rows_t = jnp.dot(incl_tb, onehot_t, preferred_element_type=F32)
            s_local = s_row - excl_s
            t_local = jnp.sum((rows_t <= s_local).astype(F32), axis=0, keepdims=True)
            tok = c_of_s * float(CHUNK) + t_local
            plane_rows = jnp.dot(plane_tb, onehot_t, preferred_element_type=F32)
            plane_s = jnp.sum(jnp.where(sub_j == t_local, plane_rows, 0.0), axis=0, keepdims=True)
            idx_ref[e, pl.ds(sb, 1), :] = tok.astype(I32)
            dst_ref[e, pl.ds(sb, 1), :] = plane_s.astype(I32) * plane_stride + tok.astype(I32)
            return c2

        lax.fori_loop(0, cap // CHUNK, per_block, 0)
        return carry

    lax.fori_loop(0, ne, per_expert, 0)
    cnt_ref[...] = rank_ref[...].astype(I32)


def _topk(aff3, cap, plane_stride):
    ne, nc, _ = aff3.shape
    full3 = lambda shp: pl.BlockSpec(shp, lambda i: (0, 0, 0))
    return pl.pallas_call(
        functools.partial(_topk_kernel, cap=cap, plane_stride=plane_stride),
        grid=(1,),
        in_specs=[full3((ne, nc, CHUNK))],
        out_specs=[full3((ne, cap // CHUNK, CHUNK)), full3((ne, cap // CHUNK, CHUNK)), full3((ne, nc, CHUNK)),
                   pl.BlockSpec((nc, CHUNK), lambda i: (0, 0))],
        out_shape=[jax.ShapeDtypeStruct((ne, cap // CHUNK, CHUNK), I32),
                   jax.ShapeDtypeStruct((ne, cap // CHUNK, CHUNK), I32),
                   jax.ShapeDtypeStruct((ne, nc, CHUNK), F32),
                   jax.ShapeDtypeStruct((nc, CHUNK), I32)],
        scratch_shapes=[pltpu.VMEM((ne, SUBLANES, CHUNK), I32), pltpu.VMEM((nc, CHUNK), F32)],
        compiler_params=_cparams(("arbitrary",)),
        name="topk",
    )(aff3)


def _ffn_kernel(idx_ref, idxn_ref, dst_ref, x_hbm, wg_ref, wu_ref, wd_ref, y_in_hbm, y_hbm, xs_ref, ys_ref, gsem_ref,
                ssem_ref, *, steps):
    tm = xs_ref.shape[1]
    s = pl.program_id(0) * pl.num_programs(1) + pl.program_id(1)
    slot = s % 2

    def gather_rows(idx_smem, buf_slot):
        for j in range(tm):
            pltpu.make_async_copy(x_hbm.at[pl.ds(idx_smem[0, 0, j], 1), :], xs_ref.at[buf_slot, pl.ds(j, 1), :],
                                  gsem_ref.at[buf_slot]).start()

    def gathered(buf_slot):
        return pltpu.make_async_copy(x_hbm.at[pl.ds(0, tm), :], xs_ref.at[buf_slot], gsem_ref.at[buf_slot])

    def scattered(buf_slot):
        return pltpu.make_async_copy(ys_ref.at[buf_slot], y_hbm.at[pl.ds(0, tm), :], ssem_ref.at[buf_slot])

    @pl.when(s == 0)
    def _():
        gather_rows(idx_ref, 0)

    @pl.when(s + 1 < steps)
    def _():
        gather_rows(idxn_ref, 1 - slot)

    gathered(slot).wait()
    xs = xs_ref[slot].astype(BF16)
    hg = jnp.dot(xs, wg_ref[0], preferred_element_type=F32)
    hu = jnp.dot(xs, wu_ref[0], preferred_element_type=F32)
    hdn = (hg * jax.nn.sigmoid(hg) * hu).astype(BF16)
    ye = jnp.dot(hdn, wd_ref[0], preferred_element_type=F32)

    @pl.when(s >= 2)
    def _():
        scattered(slot).wait()

    ys_ref[slot] = ye
    for j in range(tm):
        pltpu.make_async_copy(ys_ref.at[slot, pl.ds(j, 1), :], y_hbm.at[pl.ds(dst_ref[0, 0, j], 1), :],
                              ssem_ref.at[slot]).start()

    @pl.when(s == steps - 1)
    def _():
        scattered(slot).wait()
        if steps > 1:
            scattered(1 - slot).wait()


def _ffn(idx_tiles, dst_tiles, xm2, w_gate, w_up, w_down, cap, ybuf):
    tm = idx_tiles.shape[2]
    nt = cap // tm
    steps = N_EXPERTS * nt
    wspec = lambda: pl.BlockSpec((1, D_MODEL, EXPERT_FF), lambda e, i: (e, 0, 0))
    return pl.pallas_call(
        functools.partial(_ffn_kernel, steps=steps),
        grid=(N_EXPERTS, nt),
        in_specs=[
            pl.BlockSpec((1, 1, tm), lambda e, i: (e * nt + i, 0, 0), memory_space=pltpu.SMEM),
            pl.BlockSpec((1, 1, tm), lambda e, i: (jnp.minimum(e * nt + i + 1, steps - 1), 0, 0),
                         memory_space=pltpu.SMEM),
            pl.BlockSpec((1, 1, tm), lambda e, i: (e * nt + i, 0, 0), memory_space=pltpu.SMEM),
            pl.BlockSpec(memory_space=pl.ANY),
            wspec(), wspec(),
            pl.BlockSpec((1, EXPERT_FF, D_MODEL), lambda e, i: (e, 0, 0)),
            pl.BlockSpec(memory_space=pl.ANY),
        ],
        out_specs=pl.BlockSpec(memory_space=pl.ANY),
        out_shape=jax.ShapeDtypeStruct(ybuf.shape, F32),
        input_output_aliases={7: 0},
        scratch_shapes=[pltpu.VMEM((2, tm, D_MODEL), F32), pltpu.VMEM((2, tm, D_MODEL), F32),
                        pltpu.SemaphoreType.DMA((2,)), pltpu.SemaphoreType.DMA((2,))],
        compiler_params=_cparams(("arbitrary", "arbitrary")),
        name="ffn",
    )(idx_tiles, idx_tiles, dst_tiles, xm2, w_gate, w_up, w_down, ybuf)


def _combine_kernel(kmax_ref, y_hbm, x1_ref, mod_ref, gp_ref, lg_ref, lb_ref, o_ref, planes_ref, acc_ref, sem_ref,
                    *, n_planes):
    t = x1_ref.shape[0]
    i = pl.program_id(0)
    ntile = pl.num_programs(0)
    slot = i % 2

    def for_planes(tile, buf_slot, fn):
        for k in range(n_planes):
            @pl.when(k < kmax_ref[tile])
            def _(k=k):
                fn(pltpu.make_async_copy(y_hbm.at[k, pl.ds(tile * t, t), :], planes_ref.at[buf_slot, k],
                                         sem_ref.at[buf_slot]))

    @pl.when(i == 0)
    def _():
        for_planes(0, 0, lambda cp: cp.start())

    @pl.when(i + 1 < ntile)
    def _():
        for_planes(i + 1, 1 - slot, lambda cp: cp.start())

    for_planes(i, slot, lambda cp: cp.wait())

    acc_ref[...] = jnp.zeros(acc_ref.shape, F32)
    for k in range(n_planes):
        @pl.when(k < kmax_ref[i])
        def _(k=k):
            w = gp_ref[:, k:k + 1]
            acc_ref[...] += jnp.where(w >= 0.0, w * planes_ref[slot, k], 0.0)

    gate2 = mod_ref[0, 5:6, :]
    o_ref[...] = _layer_norm(DEEPNORM_ALPHA * x1_ref[...] + gate2 * acc_ref[...], lg_ref[...], lb_ref[...])


def _combine(kmax, y_planes, x1, mod, gp_tok, ln_g, ln_b, seq_len):
    n = x1.shape[0]
    n_planes = y_planes.shape[0]
    t = min(TOK_TILE, seq_len)
    per = n // mod.shape[0] // t
    row = lambda a: a.reshape(1, -1)
    grid_spec = pltpu.PrefetchScalarGridSpec(
        num_scalar_prefetch=1,
        grid=(n // t,),
        in_specs=[
            pl.BlockSpec(memory_space=pl.ANY),
            pl.BlockSpec((t, D_MODEL), lambda i, *_: (i, 0)),
            pl.BlockSpec((1, N_MOD, D_MODEL), lambda i, *_: (i // per, 0, 0)),
            pl.BlockSpec((t, n_planes), lambda i, *_: (i, 0)),
            pl.BlockSpec((1, D_MODEL), lambda i, *_: (0, 0)),
            pl.BlockSpec((1, D_MODEL), lambda i, *_: (0, 0)),
        ],
        out_specs=pl.BlockSpec((t, D_MODEL), lambda i, *_: (i, 0)),
        scratch_shapes=[pltpu.VMEM((2, n_planes, t, D_MODEL), F32), pltpu.VMEM((t, D_MODEL), F32),
                        pltpu.SemaphoreType.DMA((2,))],
    )
    return pl.pallas_call(
        functools.partial(_combine_kernel, n_planes=n_planes),
        grid_spec=grid_spec,
        out_shape=jax.ShapeDtypeStruct((n, D_MODEL), F32),
        compiler_params=_cparams(("arbitrary",)),
        name="combine",
    )(kmax, y_planes, x1, mod, gp_tok, row(ln_g), row(ln_b))


def _moe(x1, xm2, aff, mod, w_gate, w_up, w_down, ln_g, ln_b, seq_len, ybuf):
    n = x1.shape[0]
    cap = EC_CAPACITY * n // N_EXPERTS
    nc = n // CHUNK
    t = min(TOK_TILE, seq_len)
    stride = ybuf.shape[0] // N_EXPERTS
    aff3 = aff.T.reshape(N_EXPERTS, nc, CHUNK)
    idx3, dst3, gp3, cnt = _topk(aff3, cap, stride)
    gp_tok = gp3.reshape(N_EXPERTS, n).T
    kmax = jnp.max(cnt.reshape(n // t, t), axis=1)
    tm = min(FFN_TILE, cap)
    idx_tiles = idx3.reshape(N_EXPERTS * cap // tm, 1, tm)
    dst_tiles = dst3.reshape(N_EXPERTS * cap // tm, 1, tm)
    ybuf = _ffn(idx_tiles, dst_tiles, xm2, w_gate, w_up, w_down, cap, ybuf)
    y = _combine(kmax, ybuf.reshape(N_EXPERTS, stride, D_MODEL), x1, mod, gp_tok, ln_g, ln_b, seq_len)
    return y, ybuf


def kernel(x_prompt, x_sample, cache_na_k, cache_na_v, cache_gqa_k, cache_gqa_v, c, c_ctx, w_ada, b_ada, w_in,
           na_rpb, qn_g, kn_g, conv_w, conv_b, pool_w, pool_scale, w_branch, w_out, ln1_g, ln1_b, ln2_g, ln2_b,
           w_router, w_gate, w_up, w_down):
    depth = w_in.shape[0]
    bp, lp, _ = x_prompt.shape
    bs, ls, _ = x_sample.shape
    past = cache_na_k.shape[2]

    mods = _adaln(jnp.concatenate([c_ctx[None, :], c], axis=0), w_ada, b_ada)
    mods = mods.reshape(depth, 1 + bs, N_MOD, D_MODEL)
    cos, sin = _rope_tables(ls)

    yp = x_prompt.reshape(bp * lp, D_MODEL)
    ys = x_sample.reshape(bs * ls, D_MODEL)
    ybuf_p = jnp.zeros((N_EXPERTS * bp * lp, D_MODEL), F32)
    ybuf_s = jnp.zeros((N_EXPERTS * bs * ls, D_MODEL), F32)
    na_k_l, na_v_l, g_k_l, g_v_l = [], [], [], []
    for l in range(depth):
        mod_c = mods[l, 0:1]
        mod_s = mods[l, 1:]
        w_a = w_in[l, :, :ATT_W + CP_W].astype(BF16)
        w_gates = w_in[l, :, ATT_W + CP_W:].astype(BF16)
        w_br = w_branch[l].astype(BF16)
        w_o = w_out[l].astype(BF16)
        p_w = pool_w[l].astype(BF16)
        wg, wu, wd = w_gate[l].astype(BF16), w_up[l].astype(BF16), w_down[l].astype(BF16)
        merge_w = (w_gates, w_br, w_o, p_w, conv_w[l], conv_b[l], pool_scale[l], ln1_g[l], ln1_b[l], w_router[l])

        qn2 = jnp.tile(qn_g[l], 2).reshape(1, PAIR_W)
        kn2 = jnp.tile(kn_g[l], 2).reshape(1, PAIR_W)
        cache = lambda a, w: a[:, l].reshape(bs, past, w).astype(BF16)

        z_att, z_cp, na_k, na_v, g_k, g_v = _inproj(yp, mod_c, w_a, bp * lp, kn2)
        o_att = _ctx_attn(z_att, qn2, kn2, lp)
        na_k_l.append(na_k.reshape(bp, lp, NA_HEADS, HEAD_DIM))
        na_v_l.append(na_v.reshape(bp, lp, NA_HEADS, HEAD_DIM))
        g_k_l.append(g_k.reshape(bp, lp, GQA_KV_HEADS, HEAD_DIM))
        g_v_l.append(g_v.reshape(bp, lp, GQA_KV_HEADS, HEAD_DIM))
        x1, xm2, aff = _merge(yp, mod_c, z_cp, [o_att], *merge_w, lp)
        yp, ybuf_p = _moe(x1, xm2, aff, mod_c, wg, wu, wd, ln2_g[l], ln2_b[l], lp, ybuf_p)

        z_att, z_cp = _inproj(ys, mod_s, w_a, ls)
        tb = _na_bias_table(na_rpb[l])
        o_na = _lat_na(z_att, cache(cache_na_k, NA_W), cache(cache_na_v, NA_W), tb, ls)
        o_gqa = _lat_gqa(z_att, cache(cache_gqa_k, GQA_KW), cache(cache_gqa_v, GQA_KW), cos, sin, qn2, kn2, ls)
        x1, xm2, aff = _merge(ys, mod_s, z_cp, [o_na, o_gqa], *merge_w, ls)
        ys, ybuf_s = _moe(x1, xm2, aff, mod_s, wg, wu, wd, ln2_g[l], ln2_b[l], ls, ybuf_s)

    return (yp.reshape(bp, lp, D_MODEL), ys.reshape(bs, ls, D_MODEL),
            jnp.stack(na_k_l, axis=1), jnp.stack(na_v_l, axis=1),
            jnp.stack(g_k_l, axis=1), jnp.stack(g_v_l, axis=1))
```

```python
import functools

import numpy as np
import jax
import jax.numpy as jnp
from jax import lax
from jax.experimental import pallas as pl
from jax.experimental.pallas import tpu as pltpu

F32 = jnp.float32
BF16 = jnp.bfloat16
I32 = jnp.int32
HIGHEST = lax.Precision.HIGHEST

D_MODEL = 1024
NOMINAL_DEPTH = 4
GRID_W = 64
HEAD_DIM = 64
NA_HEADS = 8
NA_WIN_R = 8
NA_WIN_C = 16
GQA_HEADS = 8
GQA_KV_HEADS = 2
GQA_REP = GQA_HEADS // GQA_KV_HEADS
BRANCH_W = 512
N_BRANCH = 4
CONV_K = 3
POOL_WINDOWS = (2, 4, 8, 16)
POOL_GW = BRANCH_W // len(POOL_WINDOWS)
POOL_HALO = max(POOL_WINDOWS) // 2
N_EXPERTS = 16
EXPERT_FF = 1024
EC_CAPACITY = 2
ROPE_THETA = 10000.0
LN_EPS = 1e-6
RMS_EPS = 1e-6
N_MOD = 6
DEEPNORM_ALPHA = (2 * NOMINAL_DEPTH) ** 0.25
NA_W = NA_HEADS * HEAD_DIM
GQA_QW = GQA_HEADS * HEAD_DIM
GQA_KW = GQA_KV_HEADS * HEAD_DIM
ATT_W = 3 * NA_W + GQA_QW + 2 * GQA_KW
CP_W = 4 * BRANCH_W
GATE_W = N_BRANCH * D_MODEL
LOG2E = 1.4426950408889634
Q_SCALE = HEAD_DIM ** -0.5 * LOG2E
NEG_BIG = -1e30
PAIR_W = 2 * HEAD_DIM

LANES = 128
SUBLANES = 8
VMEM_LIMIT = 56 * 1024 * 1024

TOK_TILE_IN = 512
MERGE_TILE = 512
TOK_TILE = 256
CHUNK = LANES
FFN_TILE = 256
GQA_QBLOCK = 512


def _cparams(sem):
    return pltpu.CompilerParams(dimension_semantics=sem, vmem_limit_bytes=VMEM_LIMIT)


def _adaln_kernel(cv_ref, w_ref, b_ref, o_ref):
    cv = cv_ref[...]
    s = cv * jax.nn.sigmoid(cv)
    o_ref[0] = jnp.dot(s, w_ref[0], precision=HIGHEST, preferred_element_type=F32) + b_ref[0]


def _adaln(cvecs, w_ada, b_ada):
    depth = w_ada.shape[0]
    r = cvecs.shape[0]
    tn = 1536
    nw = N_MOD * D_MODEL
    return pl.pallas_call(
        _adaln_kernel,
        grid=(depth, nw // tn),
        in_specs=[
            pl.BlockSpec((r, D_MODEL), lambda l, j: (0, 0)),
            pl.BlockSpec((1, D_MODEL, tn), lambda l, j: (l, 0, j)),
            pl.BlockSpec((1, 1, tn), lambda l, j: (l, 0, j)),
        ],
        out_specs=pl.BlockSpec((1, r, tn), lambda l, j: (l, 0, j)),
        out_shape=jax.ShapeDtypeStruct((depth, r, nw), F32),
        compiler_params=_cparams(("arbitrary", "arbitrary")),
        name="adaln",
    )(cvecs, w_ada, b_ada.reshape(depth, 1, nw))


def _lane_lo(shape):
    return (lax.broadcasted_iota(I32, shape, len(shape) - 1) % PAIR_W) < HEAD_DIM


def _pair_rms(x2, g2):
    lo = _lane_lo(x2.shape)
    sq = x2 * x2
    ss_lo = jnp.sum(jnp.where(lo, sq, 0.0), axis=-1, keepdims=True)
    ss_hi = jnp.sum(jnp.where(lo, 0.0, sq), axis=-1, keepdims=True)
    inv = lax.rsqrt(jnp.where(lo, ss_lo, ss_hi) * (1.0 / HEAD_DIM) + RMS_EPS)
    return x2 * inv * g2


def _inproj_kernel(x_ref, mod_ref, w_ref, *rest, with_state):
    if with_state:
        kn_ref, att_ref, cp_ref, nak_ref, nav_ref, gk_ref, gv_ref = rest
    else:
        att_ref, cp_ref = rest
    shift = mod_ref[0, 0:1, :]
    scale = mod_ref[0, 1:2, :]
    xm = (x_ref[...] * (1.0 + scale) + shift).astype(BF16)
    z = jnp.dot(xm, w_ref[...], preferred_element_type=F32)
    att_ref[...] = z[:, :ATT_W].astype(BF16)
    cp_ref[...] = z[:, ATT_W:]
    if with_state:
        koff = 3 * NA_W + GQA_QW
        nak_ref[...] = z[:, NA_W:2 * NA_W]
        nav_ref[...] = z[:, 2 * NA_W:3 * NA_W]
        gk_ref[...] = _pair_rms(z[:, koff:koff + GQA_KW], kn_ref[...])
        gv_ref[...] = z[:, koff + GQA_KW:koff + 2 * GQA_KW]


def _inproj(x, mod, w_a, seq_len, kn2=None):
    n = x.shape[0]
    t = min(TOK_TILE_IN, seq_len)
    per = n // mod.shape[0] // t
    with_state = kn2 is not None
    tile = lambda w: pl.BlockSpec((t, w), lambda i: (i, 0))
    in_specs = [
        pl.BlockSpec((t, D_MODEL), lambda i: (i, 0)),
        pl.BlockSpec((1, N_MOD, D_MODEL), lambda i: (i // per, 0, 0)),
        pl.BlockSpec((D_MODEL, ATT_W + CP_W), lambda i: (0, 0)),
    ]
    out_specs = [tile(ATT_W), tile(CP_W)]
    out_shape = [jax.ShapeDtypeStruct((n, ATT_W), BF16), jax.ShapeDtypeStruct((n, CP_W), F32)]
    args = [x, mod, w_a]
    if with_state:
        in_specs.append(pl.BlockSpec((1, GQA_KW), lambda i: (0, 0)))
        args.append(kn2)
        for w in (NA_W, NA_W, GQA_KW, GQA_KW):
            out_specs.append(tile(w))
            out_shape.append(jax.ShapeDtypeStruct((n, w), F32))
    return pl.pallas_call(
        functools.partial(_inproj_kernel, with_state=with_state),
        grid=(n // t,),
        in_specs=in_specs,
        out_specs=out_specs,
        out_shape=out_shape,
        compiler_params=_cparams(("arbitrary",)),
        name="inproj",
    )(*args)


def _pair_rope(x2, cos2, sin2):
    q = HEAD_DIM // 4
    first = (lax.broadcasted_iota(I32, x2.shape, 1) % (2 * q)) < q
    swapped = jnp.where(first, pltpu.roll(x2, PAIR_W - q, 1), pltpu.roll(x2, q, 1))
    return x2 * cos2 + swapped * sin2


def _dup_head(x2, g):
    lo = _lane_lo(x2.shape)
    other = pltpu.roll(x2, HEAD_DIM, 1)
    return jnp.where(lo, x2, other) if g == 0 else jnp.where(lo, other, x2)


def _dot_nt(a, b):
    return lax.dot_general(a, b, (((1,), (1,)), ((), ())), preferred_element_type=F32)


def _softmax_pv(scores, values):
    m = scores[0].max(axis=-1, keepdims=True)
    for s in scores[1:]:
        m = jnp.maximum(m, s.max(axis=-1, keepdims=True))
    l = None
    o = None
    for s, v in zip(scores, values):
        p = jnp.exp2(s - m)
        ls = p.sum(axis=-1, keepdims=True)
        os_ = jnp.dot(p.astype(BF16), v, preferred_element_type=F32)
        l = ls if l is None else l + ls
        o = os_ if o is None else o + os_
    return o / l


def _pair_attn(q2, keys, values, biases=None, stack=False):
    lo = _lane_lo(q2.shape)
    m_rows = q2.shape[0]
    q_halves = [jnp.where(lo, q2, 0.0).astype(BF16), jnp.where(lo, 0.0, q2).astype(BF16)]

    def run(q, row0):
        scores = []
        for bi, k2 in enumerate(keys):
            s = _dot_nt(q, k2)
            if biases is not None and biases[bi] is not None:
                s = s + biases[bi][row0:row0 + q.shape[0]]
            scores.append(s)
        return _softmax_pv(scores, values)

    if stack:
        o = run(jnp.concatenate(q_halves, axis=0), 0)
        return jnp.where(lo, o[:m_rows], o[m_rows:])
    return jnp.where(lo, run(q_halves[0], 0), run(q_halves[1], m_rows))


def _ctx_attn_kernel(z_ref, qn_ref, kn_ref, o_ref):
    for p in range(NA_HEADS // 2):
        ps = slice(p * PAIR_W, (p + 1) * PAIR_W)
        q2 = z_ref[:, ps].astype(F32) * Q_SCALE
        k2 = z_ref[:, NA_W + p * PAIR_W:NA_W + (p + 1) * PAIR_W]
        v2 = z_ref[:, 2 * NA_W + p * PAIR_W:2 * NA_W + (p + 1) * PAIR_W]
        o_ref[:, ps] = _pair_attn(q2, [k2], [v2], stack=True).astype(BF16)
    qoff = 3 * NA_W
    koff = qoff + GQA_QW
    voff = koff + GQA_KW
    kn2 = _pair_rms(z_ref[:, koff:koff + GQA_KW].astype(F32), kn_ref[...])
    v2 = z_ref[:, voff:voff + GQA_KW].astype(F32)
    kd = [_dup_head(kn2, g).astype(BF16) for g in range(GQA_KV_HEADS)]
    vd = [_dup_head(v2, g).astype(BF16) for g in range(GQA_KV_HEADS)]
    for p in range(GQA_HEADS // 2):
        g = (2 * p) // GQA_REP
        q2 = _pair_rms(z_ref[:, qoff + p * PAIR_W:qoff + (p + 1) * PAIR_W].astype(F32), qn_ref[...]) * Q_SCALE
        o_ref[:, NA_W + p * PAIR_W:NA_W + (p + 1) * PAIR_W] = _pair_attn(q2, [kd[g]], [vd[g]],
                                                                          stack=True).astype(BF16)


def _ctx_attn(z_att, qn2, kn2, seq_len):
    n = z_att.shape[0]
    return pl.pallas_call(
        _ctx_attn_kernel,
        grid=(n // seq_len,),
        in_specs=[
            pl.BlockSpec((seq_len, ATT_W), lambda b: (b, 0)),
            pl.BlockSpec((1, PAIR_W), lambda b: (0, 0)),
            pl.BlockSpec((1, PAIR_W), lambda b: (0, 0)),
        ],
        out_specs=pl.BlockSpec((seq_len, NA_W + GQA_QW), lambda b: (b, 0)),
        out_shape=jax.ShapeDtypeStruct((n, NA_W + GQA_QW), BF16),
        compiler_params=_cparams(("arbitrary",)),
        name="ctx_attn",
    )(z_att, qn2, kn2)


def _na_row_start(r, rows):
    return jnp.clip(r - NA_WIN_R // 2, 0, rows - NA_WIN_R)


def _lat_na_kernel(q_ref, k_ref, v_ref, kc_ref, vc_ref, tb_ref, o_ref, *, rows):
    r = pl.program_id(1)
    wr = NA_WIN_R
    rs = _na_row_start(r, rows)
    start = pl.multiple_of(rs * GRID_W, GRID_W)
    for p in range(NA_HEADS // 2):
        ps = slice(p * PAIR_W, (p + 1) * PAIR_W)
        q2 = q_ref[:, ps].astype(F32) * Q_SCALE
        k2 = k_ref[pl.ds(start, wr * GRID_W), ps]
        v2 = v_ref[pl.ds(start, wr * GRID_W), ps]
        kc2 = kc_ref[0, :, ps]
        vc2 = vc_ref[0, :, ps]
        bias = tb_ref[0, 2 * p:2 * p + 2].reshape(2 * GRID_W, wr * GRID_W)
        o_ref[:, ps] = _pair_attn(q2, [k2, kc2], [v2, vc2], [bias, None], stack=True).astype(BF16)


def _na_bias_table(rpb):
    wr = NA_WIN_R
    cols = np.arange(GRID_W)
    col_start = np.clip(cols - NA_WIN_C // 2, 0, GRID_W - NA_WIN_C)
    ck = np.arange(GRID_W)[None, :]
    valid = (ck >= col_start[:, None]) & (ck < col_start[:, None] + NA_WIN_C)
    rel = np.clip(ck - cols[:, None] + (NA_WIN_C - 1), 0, 2 * NA_WIN_C - 2)
    d = np.arange(wr)[:, None] + np.arange(wr)[None, :]
    t = rpb[:, d][:, :, :, rel] * LOG2E
    t = jnp.where(valid[None, None, None], t, NEG_BIG)
    t = t.transpose(1, 0, 3, 2, 4)
    return t.reshape(wr, NA_HEADS, GRID_W, wr * GRID_W)


def _lat_na(z_att, kc, vc, tb, seq_len):
    n = z_att.shape[0]
    nb = n // seq_len
    rows = seq_len // GRID_W
    assert rows >= NA_WIN_R
    wr = NA_WIN_R
    p = kc.shape[1]
    blk = rows
    return pl.pallas_call(
        functools.partial(_lat_na_kernel, rows=rows),
        grid=(nb, rows),
        in_specs=[
            pl.BlockSpec((GRID_W, NA_W), lambda b, r: (b * blk + r, 0)),
            pl.BlockSpec((seq_len, NA_W), lambda b, r: (b, 1)),
            pl.BlockSpec((seq_len, NA_W), lambda b, r: (b, 2)),
            pl.BlockSpec((1, p, NA_W), lambda b, r: (b, 0, 0)),
            pl.BlockSpec((1, p, NA_W), lambda b, r: (b, 0, 0)),
            pl.BlockSpec((1, NA_HEADS, GRID_W, wr * GRID_W),
                         lambda b, r: (_na_row_start(r, rows) - r + (wr - 1), 0, 0, 0)),
        ],
        out_specs=pl.BlockSpec((GRID_W, NA_W), lambda b, r: (b * blk + r, 0)),
        out_shape=jax.ShapeDtypeStruct((n, NA_W), BF16),
        compiler_params=_cparams(("arbitrary", "arbitrary")),
        name="lat_na",
    )(z_att, z_att, z_att, kc, vc, tb)


def _lat_gqa_kernel(q_ref, k_ref, v_ref, kc_ref, vc_ref, cq_ref, sq_ref, ck_ref, sk_ref, qn_ref, kn_ref, o_ref):
    kn2 = _pair_rope(_pair_rms(k_ref[...].astype(F32), kn_ref[...]), ck_ref[...], sk_ref[...])
    v2 = v_ref[...].astype(F32)
    kc2 = kc_ref[0].astype(F32)
    vc2 = vc_ref[0].astype(F32)
    kd, vd, kcd, vcd = [], [], [], []
    for g in range(GQA_KV_HEADS):
        kd.append(_dup_head(kn2, g).astype(BF16))
        vd.append(_dup_head(v2, g).astype(BF16))
        kcd.append(_dup_head(kc2, g).astype(BF16))
        vcd.append(_dup_head(vc2, g).astype(BF16))
    for p in range(GQA_HEADS // 2):
        g = (2 * p) // GQA_REP
        ps = slice(p * PAIR_W, (p + 1) * PAIR_W)
        q2 = _pair_rope(_pair_rms(q_ref[:, ps].astype(F32), qn_ref[...]), cq_ref[...], sq_ref[...]) * Q_SCALE
        o_ref[:, ps] = _pair_attn(q2, [kd[g], kcd[g]], [vd[g], vcd[g]]).astype(BF16)


def _rope_tables(seq_len):
    t = jnp.arange(seq_len)
    n_freq = HEAD_DIM // 4
    inv = ROPE_THETA ** (-jnp.arange(n_freq, dtype=F32) / n_freq)
    ang_r = (t // GRID_W).astype(F32)[:, None] * inv
    ang_c = (t % GRID_W).astype(F32)[:, None] * inv
    cr, sr, cc, sc = jnp.cos(ang_r), jnp.sin(ang_r), jnp.cos(ang_c), jnp.sin(ang_c)
    cos = jnp.concatenate([cr, cr, cc, cc] * 2, axis=-1)
    sin = jnp.concatenate([-sr, sr, -sc, sc] * 2, axis=-1)
    return cos, sin


def _lat_gqa(z_att, kc, vc, cos, sin, qn2, kn2, seq_len):
    n = z_att.shape[0]
    nb = n // seq_len
    tq = min(GQA_QBLOCK, seq_len)
    nqb = seq_len // tq
    p = kc.shape[1]
    qcol = (3 * NA_W) // GQA_QW
    kcol = (3 * NA_W + GQA_QW) // GQA_KW
    return pl.pallas_call(
        _lat_gqa_kernel,
        grid=(nb, nqb),
        in_specs=[
            pl.BlockSpec((tq, GQA_QW), lambda b, i: (b * nqb + i, qcol)),
            pl.BlockSpec((seq_len, GQA_KW), lambda b, i: (b, kcol)),
            pl.BlockSpec((seq_len, GQA_KW), lambda b, i: (b, kcol + 1)),
            pl.BlockSpec((1, p, GQA_KW), lambda b, i: (b, 0, 0)),
            pl.BlockSpec((1, p, GQA_KW), lambda b, i: (b, 0, 0)),
            pl.BlockSpec((tq, PAIR_W), lambda b, i: (i, 0)),
            pl.BlockSpec((tq, PAIR_W), lambda b, i: (i, 0)),
            pl.BlockSpec((seq_len, PAIR_W), lambda b, i: (0, 0)),
            pl.BlockSpec((seq_len, PAIR_W), lambda b, i: (0, 0)),
            pl.BlockSpec((1, PAIR_W), lambda b, i: (0, 0)),
            pl.BlockSpec((1, PAIR_W), lambda b, i: (0, 0)),
        ],
        out_specs=pl.BlockSpec((tq, GQA_QW), lambda b, i: (b * nqb + i, 0)),
        out_shape=jax.ShapeDtypeStruct((n, GQA_QW), BF16),
        compiler_params=_cparams(("arbitrary", "arbitrary")),
        name="lat_gqa",
    )(z_att, z_att, z_att, kc, vc, cos, sin, cos, sin, qn2, kn2)


def _layer_norm(x, g, b):
    mu = jnp.mean(x, axis=-1, keepdims=True)
    xc = x - mu
    var = jnp.mean(xc * xc, axis=-1, keepdims=True)
    return xc * lax.rsqrt(var + LN_EPS) * g + b


def _merge_kernel(x_ref, mod_ref, cp_ref, cpp_ref, cpn_ref, *rest, seq_len, n_att):
    att_refs = rest[:n_att]
    (wg_ref, wb_ref, wo_ref, pw_ref, cw_ref, cb_ref, ps_ref, lg_ref, lb_ref, wr_ref,
     x1_ref, xm2_ref, aff_ref, ucv_ref, upl_ref) = rest[n_att:]
    t = x_ref.shape[0]
    halo = POOL_HALO
    i = pl.program_id(0)
    pos0 = (i * t) % seq_len
    has_prev = pos0 > 0
    has_next = pos0 + t < seq_len

    x = x_ref[...]
    shift1, scale1, gate1 = mod_ref[0, 0:1, :], mod_ref[0, 1:2, :], mod_ref[0, 2:3, :]
    shift2, scale2 = mod_ref[0, 3:4, :], mod_ref[0, 4:5, :]
    xm = (x * (1.0 + scale1) + shift1).astype(BF16)

    bw = BRANCH_W
    cv_b = cp_ref[:, 0:bw]
    ucv_ref[halo:halo + t, :] = cp_ref[:, bw:2 * bw] * cp_ref[:, 2 * bw:3 * bw]
    upl_ref[halo:halo + t, :] = cp_ref[:, 3 * bw:4 * bw]
    ucv_ref[0:halo, :] = jnp.where(has_prev, cpp_ref[:, bw:2 * bw] * cpp_ref[:, 2 * bw:3 * bw], 0.0)
    upl_ref[0:halo, :] = jnp.where(has_prev, cpp_ref[:, 3 * bw:4 * bw], 0.0)
    ucv_ref[halo + t:, :] = jnp.where(has_next, cpn_ref[:, bw:2 * bw] * cpn_ref[:, 2 * bw:3 * bw], 0.0)
    upl_ref[halo + t:, :] = jnp.where(has_next, cpn_ref[:, 3 * bw:4 * bw], 0.0)

    conv = (cw_ref[0:1, :] * ucv_ref[halo - 1:halo - 1 + t, :]
            + cw_ref[1:2, :] * ucv_ref[halo:halo + t, :]
            + cw_ref[2:3, :] * ucv_ref[halo + 1:halo + 1 + t, :]) + cb_ref[...]
    o_conv = (cv_b * conv).astype(BF16)

    pos = (pos0 + lax.broadcasted_iota(I32, (t, 1), 0))
    mixed = []
    for g, win in enumerate(POOL_WINDOWS):
        gs = slice(g * POOL_GW, (g + 1) * POOL_GW)
        acc = None
        for dlt in range(-(win // 2), win // 2):
            term = upl_ref[halo + dlt:halo + dlt + t, gs]
            acc = term if acc is None else acc + term
        lo = jnp.maximum(pos - win // 2, 0)
        hi = jnp.minimum(pos + win // 2, seq_len)
        cnt = (hi - lo).astype(F32)
        pooled = acc / cnt - upl_ref[halo:halo + t, gs]
        mixed.append(jnp.dot(pooled.astype(BF16), pw_ref[g], preferred_element_type=F32))
    o_pool = (jnp.concatenate(mixed, axis=-1) * ps_ref[...]).astype(BF16)

    if n_att == 1:
        branches = [att_refs[0][:, 0:bw], att_refs[0][:, bw:2 * bw], o_conv, o_pool]
    else:
        branches = [att_refs[0][...], att_refs[1][...], o_conv, o_pool]

    merged = None
    for nb in range(N_BRANCH):
        zg = jnp.dot(xm, wg_ref[:, nb * D_MODEL:(nb + 1) * D_MODEL], preferred_element_type=F32)
        proj = jnp.dot(branches[nb], wb_ref[nb], preferred_element_type=F32)
        term = jax.nn.sigmoid(zg) * proj
        merged = term if merged is None else merged + term
    y = jnp.dot(merged.astype(BF16), wo_ref[...], preferred_element_type=F32)
    x1 = _layer_norm(DEEPNORM_ALPHA * x + gate1 * y, lg_ref[...], lb_ref[...])
    x1_ref[...] = x1
    xm2 = x1 * (1.0 + scale2) + shift2
    xm2_ref[...] = xm2
    wr = wr_ref[...]
    xh = xm2.astype(BF16)
    xl = (xm2 - xh.astype(F32)).astype(BF16)
    wh = wr.astype(BF16)
    wl = (wr - wh.astype(F32)).astype(BF16)
    logits = (jnp.dot(xh, wh, preferred_element_type=F32) + jnp.dot(xl, wh, preferred_element_type=F32)
              + jnp.dot(xh, wl, preferred_element_type=F32))
    m = logits.max(axis=-1, keepdims=True)
    e = jnp.exp(logits - m)
    aff_ref[...] = e / e.sum(axis=-1, keepdims=True)


def _merge(x, mod, z_cp, atts, w_gates, w_branch, w_out, pool_w, conv_w, conv_b, pool_scale, ln_g, ln_b,
           w_router, seq_len):
    n = x.shape[0]
    t = min(MERGE_TILE, seq_len)
    per = n // mod.shape[0] // t
    hb = t // POOL_HALO
    nhb = n // POOL_HALO
    row = lambda a: a.reshape(1, -1)
    const2 = lambda i: (0, 0)
    const3 = lambda i: (0, 0, 0)
    once = lambda shp, imap: pl.BlockSpec(shp, imap, pipeline_mode=pl.Buffered(1))
    att_specs = [pl.BlockSpec((t, a.shape[1]), lambda i: (i, 0)) for a in atts]
    return pl.pallas_call(
        functools.partial(_merge_kernel, seq_len=seq_len, n_att=len(atts)),
        grid=(n // t,),
        in_specs=[
            pl.BlockSpec((t, D_MODEL), lambda i: (i, 0)),
            pl.BlockSpec((1, N_MOD, D_MODEL), lambda i: (i // per, 0, 0)),
            pl.BlockSpec((t, CP_W), lambda i: (i, 0)),
            pl.BlockSpec((POOL_HALO, CP_W), lambda i: (jnp.maximum(i * hb - 1, 0), 0)),
            pl.BlockSpec((POOL_HALO, CP_W), lambda i: (jnp.minimum((i + 1) * hb, nhb - 1), 0)),
            *att_specs,
            once((D_MODEL, GATE_W), const2),
            once((N_BRANCH, BRANCH_W, D_MODEL), const3),
            once((D_MODEL, D_MODEL), const2),
            pl.BlockSpec((len(POOL_WINDOWS), POOL_GW, POOL_GW), const3),
            pl.BlockSpec((CONV_K, BRANCH_W), const2),
            pl.BlockSpec((1, BRANCH_W), const2),
            pl.BlockSpec((1, BRANCH_W), const2),
            pl.BlockSpec((1, D_MODEL), const2),
            pl.BlockSpec((1, D_MODEL), const2),
            pl.BlockSpec((D_MODEL, N_EXPERTS), const2),
        ],
        out_specs=[
            pl.BlockSpec((t, D_MODEL), lambda i: (i, 0)),
            pl.BlockSpec((t, D_MODEL), lambda i: (i, 0)),
            pl.BlockSpec((t, N_EXPERTS), lambda i: (i, 0)),
        ],
        out_shape=[jax.ShapeDtypeStruct((n, D_MODEL), F32), jax.ShapeDtypeStruct((n, D_MODEL), F32),
                   jax.ShapeDtypeStruct((n, N_EXPERTS), F32)],
        scratch_shapes=[pltpu.VMEM((t + 2 * POOL_HALO, BRANCH_W), F32),
                        pltpu.VMEM((t + 2 * POOL_HALO, BRANCH_W), F32)],
        compiler_params=_cparams(("arbitrary",)),
        name="merge",
    )(x, mod, z_cp, z_cp, z_cp, *atts, w_gates, w_branch, w_out, pool_w, conv_w, row(conv_b), row(pool_scale),
      row(ln_g), row(ln_b), w_router)


def _topk_kernel(aff_ref, idx_ref, dst_ref, gp_ref, cnt_ref, thr_ref, rank_ref, *, cap, plane_stride):
    ne, nc, _ = aff_ref.shape
    capf = float(cap)
    bits_all = pltpu.bitcast(aff_ref[...], I32)

    def search(i, cur):
        cand = cur | (jnp.int32(1) << (30 - i))
        cnt = jnp.sum((bits_all >= cand).astype(F32), axis=(1, 2), keepdims=True)
        return jnp.where(cnt >= capf, cand, cur)

    thr = lax.fori_loop(0, 31, search, jnp.zeros((ne, 1, 1), I32))
    thr_ref[...] = jnp.broadcast_to(thr, thr_ref.shape)

    jj = lax.broadcasted_iota(I32, (CHUNK, CHUNK), 0)
    kk = lax.broadcasted_iota(I32, (CHUNK, CHUNK), 1)
    ut_incl = (jj <= kk).astype(BF16)
    lt_incl = (kk <= jj).astype(BF16)
    cc = lax.broadcasted_iota(I32, (nc, nc), 0)
    dd = lax.broadcasted_iota(I32, (nc, nc), 1)
    cl_excl = (dd < cc).astype(BF16)
    cl_incl = (dd <= cc).astype(BF16)
    eye = (jj == kk).astype(BF16)
    wide = 4 if cap % (4 * CHUNK) == 0 else 1
    lane_w = lax.broadcasted_iota(I32, (1, wide * CHUNK), 1)
    sub_cw = lax.broadcasted_iota(I32, (nc, wide * CHUNK), 0).astype(F32)
    sub_jw = lax.broadcasted_iota(I32, (CHUNK, wide * CHUNK), 0).astype(F32)
    rank_ref[...] = jnp.zeros(rank_ref.shape, F32)
    gp_ref[...] = jnp.full(gp_ref.shape, -1.0, F32)

    def per_expert(e, carry):
        a = aff_ref[e]
        b = pltpu.bitcast(a, I32)
        t = thr_ref[e][0:1, :]
        gt = b > t
        eq = b == t
        need = capf - jnp.sum(gt.astype(F32), keepdims=True)
        eqf = eq.astype(F32)
        incl_eq = jnp.dot(eqf.astype(BF16), ut_incl, preferred_element_type=F32)
        tot_eq = jnp.broadcast_to(incl_eq[:, CHUNK - 1:CHUNK], (nc, CHUNK))
        cum_eq = jnp.dot(cl_excl, tot_eq.astype(BF16), preferred_element_type=F32)
        rank = cum_eq + incl_eq - eqf
        sel = jnp.logical_or(gt, jnp.logical_and(eq, rank < need))
        self_ = sel.astype(F32)

        plane = rank_ref[...]
        rank_ref[...] = plane + self_
        for k in range(ne):
            gp_ref[k] = jnp.where(jnp.logical_and(sel, plane == float(k)), a, gp_ref[k])
        plane_tb = _dot_nt(eye, plane.astype(BF16)).astype(BF16)

        selb = self_.astype(BF16)
        incl = jnp.dot(selb, ut_incl, preferred_element_type=F32)
        tot = jnp.broadcast_to(incl[:, CHUNK - 1:CHUNK], (nc, CHUNK))
        cum_incl = jnp.dot(cl_incl, tot.astype(BF16), preferred_element_type=F32)
        incl_tb = _dot_nt(lt_incl, selb).astype(BF16)

        def per_block(sb, c2):
            s_row = (sb * (wide * CHUNK) + lane_w).astype(F32)
            m = cum_w <= s_row
            c_of_s = jnp.sum(m.astype(F32), axis=0, keepdims=True)
            excl_s = jnp.max(jnp.where(m, cum_w, 0.0), axis=0, keepdims=True)
            onehot_t = (sub_cw == c_of_s).astype(BF16)
            rows_t = jnp.dot(incl_tb, onehot_t, preferred_element_type=F32)
            s_local = s_row - excl_s
            t_local = jnp.sum((rows_t <= s_local).astype(F32), axis=0, keepdims=True)
            tok = (c_of_s * float(CHUNK) + t_local).astype(I32)
            plane_rows = jnp.dot(plane_tb, onehot_t, preferred_element_type=F32)
            plane_s = jnp.sum(jnp.where(sub_jw == t_local, plane_rows, 0.0), axis=0, keepdims=True)
            dst = plane_s.astype(I32) * plane_stride + tok
            for u in range(wide):
                idx_ref[e, pl.ds(sb * wide + u, 1), :] = tok[:, u * CHUNK:(u + 1) * CHUNK]
                dst_ref[e, pl.ds(sb * wide + u, 1), :] = dst[:, u * CHUNK:(u + 1) * CHUNK]
            return c2

        cum_w = jnp.concatenate([cum_incl] * wide, axis=1)
        lax.fori_loop(0, cap // (wide * CHUNK), per_block, 0)
        return carry

    lax.fori_loop(0, ne, per_expert, 0)
    cnt_ref[...] = rank_ref[...].astype(I32)


def _topk(aff3, cap, plane_stride):
    ne, nc, _ = aff3.shape
    full3 = lambda shp: pl.BlockSpec(shp, lambda i: (0, 0, 0))
    return pl.pallas_call(
        functools.partial(_topk_kernel, cap=cap, plane_stride=plane_stride),
        grid=(1,),
        in_specs=[full3((ne, nc, CHUNK))],
        out_specs=[full3((ne, cap // CHUNK, CHUNK)), full3((ne, cap // CHUNK, CHUNK)), full3((ne, nc, CHUNK)),
                   pl.BlockSpec((nc, CHUNK), lambda i: (0, 0))],
        out_shape=[jax.ShapeDtypeStruct((ne, cap // CHUNK, CHUNK), I32),
                   jax.ShapeDtypeStruct((ne, cap // CHUNK, CHUNK), I32),
                   jax.ShapeDtypeStruct((ne, nc, CHUNK), F32),
                   jax.ShapeDtypeStruct((nc, CHUNK), I32)],
        scratch_shapes=[pltpu.VMEM((ne, SUBLANES, CHUNK), I32), pltpu.VMEM((nc, CHUNK), F32)],
        compiler_params=_cparams(("arbitrary",)),
        name="topk",
    )(aff3)


def _ffn_kernel(idx_ref, idxn_ref, dst_ref, x_hbm, wg_ref, wu_ref, wd_ref, y_in_hbm, y_hbm, xs_ref, ys_ref, gsem_ref,
                ssem_ref, *, steps):
    tm = xs_ref.shape[1]
    s = pl.program_id(0) * pl.num_programs(1) + pl.program_id(1)

    def gather_rows(idx_smem, buf_slot):
        for j in range(tm):
            pltpu.make_async_copy(x_hbm.at[pl.ds(idx_smem[0, 0, j], 1), :], xs_ref.at[buf_slot, pl.ds(j, 1), :],
                                  gsem_ref.at[buf_slot]).start()

    def gathered(buf_slot):
        return pltpu.make_async_copy(x_hbm.at[pl.ds(0, tm), :], xs_ref.at[buf_slot], gsem_ref.at[buf_slot])

    def scattered(buf_slot):
        return pltpu.make_async_copy(ys_ref.at[buf_slot], y_hbm.at[pl.ds(0, tm), :], ssem_ref.at[buf_slot])

    @pl.when(s == 0)
    def _():
        gather_rows(idx_ref, 0)

    def step(slot):
        @pl.when(s + 1 < steps)
        def _():
            gather_rows(idxn_ref, 1 - slot)

        gathered(slot).wait()
        xs = xs_ref[slot].astype(BF16)
        hg = jnp.dot(xs, wg_ref[0], preferred_element_type=F32)
        hu = jnp.dot(xs, wu_ref[0], preferred_element_type=F32)
        hdn = (hg * jax.nn.sigmoid(hg) * hu).astype(BF16)
        ye = jnp.dot(hdn, wd_ref[0], preferred_element_type=F32)

        @pl.when(s >= 2)
        def _():
            scattered(slot).wait()

        ys_ref[slot] = ye
        for j in range(tm):
            pltpu.make_async_copy(ys_ref.at[slot, pl.ds(j, 1), :], y_hbm.at[pl.ds(dst_ref[0, 0, j], 1), :],
                                  ssem_ref.at[slot]).start(priority=j % 2)

        @pl.when(s == steps - 1)
        def _():
            scattered(slot).wait()
            if steps > 1:
                scattered(1 - slot).wait()

    for parity in range(2):
        pl.when(s % 2 == parity)(functools.partial(step, parity))


def _ffn(idx_tiles, dst_tiles, xm2, w_gate, w_up, w_down, cap, ybuf):
    tm = idx_tiles.shape[2]
    nt = cap // tm
    steps = N_EXPERTS * nt
    wspec = lambda: pl.BlockSpec((1, D_MODEL, EXPERT_FF), lambda e, i: (e, 0, 0))
    return pl.pallas_call(
        functools.partial(_ffn_kernel, steps=steps),
        grid=(N_EXPERTS, nt),
        in_specs=[
            pl.BlockSpec((1, 1, tm), lambda e, i: (e * nt + i, 0, 0), memory_space=pltpu.SMEM),
            pl.BlockSpec((1, 1, tm), lambda e, i: (jnp.minimum(e * nt + i + 1, steps - 1), 0, 0),
                         memory_space=pltpu.SMEM),
            pl.BlockSpec((1, 1, tm), lambda e, i: (e * nt + i, 0, 0), memory_space=pltpu.SMEM),
            pl.BlockSpec(memory_space=pl.ANY),
            wspec(), wspec(),
            pl.BlockSpec((1, EXPERT_FF, D_MODEL), lambda e, i: (e, 0, 0)),
            pl.BlockSpec(memory_space=pl.ANY),
        ],
        out_specs=pl.BlockSpec(memory_space=pl.ANY),
        out_shape=jax.ShapeDtypeStruct(ybuf.shape, F32),
        input_output_aliases={7: 0},
        scratch_shapes=[pltpu.VMEM((2, tm, D_MODEL), F32), pltpu.VMEM((2, tm, D_MODEL), F32),
                        pltpu.SemaphoreType.DMA((2,)), pltpu.SemaphoreType.DMA((2,))],
        compiler_params=_cparams(("arbitrary", "arbitrary")),
        name="ffn",
    )(idx_tiles, idx_tiles, dst_tiles, xm2, w_gate, w_up, w_down, ybuf)


def _combine_kernel(kmax_ref, y_hbm, x1_ref, mod_ref, gp_ref, lg_ref, lb_ref, o_ref, planes_ref, acc_ref, sem_ref,
                    *, n_planes):
    t = x1_ref.shape[0]
    i = pl.program_id(0)
    ntile = pl.num_programs(0)
    slot = i % 2

    def for_planes(tile, buf_slot, fn):
        for k in range(n_planes):
            @pl.when(k < kmax_ref[tile])
            def _(k=k):
                fn(pltpu.make_async_copy(y_hbm.at[k, pl.ds(tile * t, t), :], planes_ref.at[buf_slot, k],
                                         sem_ref.at[buf_slot]))

    @pl.when(i == 0)
    def _():
        for_planes(0, 0, lambda cp: cp.start())

    @pl.when(i + 1 < ntile)
    def _():
        for_planes(i + 1, 1 - slot, lambda cp: cp.start())

    for_planes(i, slot, lambda cp: cp.wait())

    acc_ref[...] = jnp.zeros(acc_ref.shape, F32)
    for k in range(n_planes):
        @pl.when(k < kmax_ref[i])
        def _(k=k):
            w = gp_ref[:, k:k + 1]
            acc_ref[...] += jnp.where(w >= 0.0, w * planes_ref[slot, k], 0.0)

    gate2 = mod_ref[0, 5:6, :]
    o_ref[...] = _layer_norm(DEEPNORM_ALPHA * x1_ref[...] + gate2 * acc_ref[...], lg_ref[...], lb_ref[...])


def _combine(kmax, y_planes, x1, mod, gp_tok, ln_g, ln_b, seq_len):
    n = x1.shape[0]
    n_planes = y_planes.shape[0]
    t = min(TOK_TILE, seq_len)
    per = n // mod.shape[0] // t
    row = lambda a: a.reshape(1, -1)
    grid_spec = pltpu.PrefetchScalarGridSpec(
        num_scalar_prefetch=1,
        grid=(n // t,),
        in_specs=[
            pl.BlockSpec(memory_space=pl.ANY),
            pl.BlockSpec((t, D_MODEL), lambda i, *_: (i, 0)),
            pl.BlockSpec((1, N_MOD, D_MODEL), lambda i, *_: (i // per, 0, 0)),
            pl.BlockSpec((t, n_planes), lambda i, *_: (i, 0)),
            pl.BlockSpec((1, D_MODEL), lambda i, *_: (0, 0)),
            pl.BlockSpec((1, D_MODEL), lambda i, *_: (0, 0)),
        ],
        out_specs=pl.BlockSpec((t, D_MODEL), lambda i, *_: (i, 0)),
        scratch_shapes=[pltpu.VMEM((2, n_planes, t, D_MODEL), F32), pltpu.VMEM((t, D_MODEL), F32),
                        pltpu.SemaphoreType.DMA((2,))],
    )
    return pl.pallas_call(
        functools.partial(_combine_kernel, n_planes=n_planes),
        grid_spec=grid_spec,
        out_shape=jax.ShapeDtypeStruct((n, D_MODEL), F32),
        compiler_params=_cparams(("arbitrary",)),
        name="combine",
    )(kmax, y_planes, x1, mod, gp_tok, row(ln_g), row(ln_b))


def _moe(x1, xm2, aff, mod, w_gate, w_up, w_down, ln_g, ln_b, seq_len, ybuf):
    n = x1.shape[0]
    cap = EC_CAPACITY * n // N_EXPERTS
    nc = n // CHUNK
    t = min(TOK_TILE, seq_len)
    stride = ybuf.shape[0] // N_EXPERTS
    aff3 = aff.T.reshape(N_EXPERTS, nc, CHUNK)
    idx3, dst3, gp3, cnt = _topk(aff3, cap, stride)
    gp_tok = gp3.reshape(N_EXPERTS, n).T
    kmax = jnp.max(cnt.reshape(n // t, t), axis=1)
    tm = min(FFN_TILE, cap)
    idx_tiles = idx3.reshape(N_EXPERTS * cap // tm, 1, tm)
    dst_tiles = dst3.reshape(N_EXPERTS * cap // tm, 1, tm)
    ybuf = _ffn(idx_tiles, dst_tiles, xm2, w_gate, w_up, w_down, cap, ybuf)
    y = _combine(kmax, ybuf.reshape(N_EXPERTS, stride, D_MODEL), x1, mod, gp_tok, ln_g, ln_b, seq_len)
    return y, ybuf


def kernel(x_prompt, x_sample, cache_na_k, cache_na_v, cache_gqa_k, cache_gqa_v, c, c_ctx, w_ada, b_ada, w_in,
           na_rpb, qn_g, kn_g, conv_w, conv_b, pool_w, pool_scale, w_branch, w_out, ln1_g, ln1_b, ln2_g, ln2_b,
           w_router, w_gate, w_up, w_down):
    depth = w_in.shape[0]
    bp, lp, _ = x_prompt.shape
    bs, ls, _ = x_sample.shape
    past = cache_na_k.shape[2]

    mods = _adaln(jnp.concatenate([c_ctx[None, :], c], axis=0), w_ada, b_ada)
    mods = mods.reshape(depth, 1 + bs, N_MOD, D_MODEL)
    cos, sin = _rope_tables(ls)

    yp = x_prompt.reshape(bp * lp, D_MODEL)
    ys = x_sample.reshape(bs * ls, D_MODEL)
    ybuf_p = jnp.zeros((N_EXPERTS * bp * lp, D_MODEL), F32)
    ybuf_s = jnp.zeros((N_EXPERTS * bs * ls, D_MODEL), F32)
    na_k_l, na_v_l, g_k_l, g_v_l = [], [], [], []
    for l in range(depth):
        mod_c = mods[l, 0:1]
        mod_s = mods[l, 1:]
        w_a = w_in[l, :, :ATT_W + CP_W].astype(BF16)
        w_gates = w_in[l, :, ATT_W + CP_W:].astype(BF16)
        w_br = w_branch[l].astype(BF16)
        w_o = w_out[l].astype(BF16)
        p_w = pool_w[l].astype(BF16)
        wg, wu, wd = w_gate[l].astype(BF16), w_up[l].astype(BF16), w_down[l].astype(BF16)
        merge_w = (w_gates, w_br, w_o, p_w, conv_w[l], conv_b[l], pool_scale[l], ln1_g[l], ln1_b[l], w_router[l])

        qn2 = jnp.tile(qn_g[l], 2).reshape(1, PAIR_W)
        kn2 = jnp.tile(kn_g[l], 2).reshape(1, PAIR_W)
        cache = lambda a, w: a[:, l].reshape(bs, past, w).astype(BF16)

        z_att, z_cp, na_k, na_v, g_k, g_v = _inproj(yp, mod_c, w_a, bp * lp, kn2)
        o_att = _ctx_attn(z_att, qn2, kn2, lp)
        na_k_l.append(na_k.reshape(bp, lp, NA_HEADS, HEAD_DIM))
        na_v_l.append(na_v.reshape(bp, lp, NA_HEADS, HEAD_DIM))
        g_k_l.append(g_k.reshape(bp, lp, GQA_KV_HEADS, HEAD_DIM))
        g_v_l.append(g_v.reshape(bp, lp, GQA_KV_HEADS, HEAD_DIM))
        x1, xm2, aff = _merge(yp, mod_c, z_cp, [o_att], *merge_w, lp)
        yp, ybuf_p = _moe(x1, xm2, aff, mod_c, wg, wu, wd, ln2_g[l], ln2_b[l], lp, ybuf_p)

        z_att, z_cp = _inproj(ys, mod_s, w_a, ls)
        tb = _na_bias_table(na_rpb[l])
        o_na = _lat_na(z_att, cache(cache_na_k, NA_W), cache(cache_na_v, NA_W), tb, ls)
        o_gqa = _lat_gqa(z_att, cache(cache_gqa_k, GQA_KW), cache(cache_gqa_v, GQA_KW), cos, sin, qn2, kn2, ls)
        x1, xm2, aff = _merge(ys, mod_s, z_cp, [o_na, o_gqa], *merge_w, ls)
        ys, ybuf_s = _moe(x1, xm2, aff, mod_s, wg, wu, wd, ln2_g[l], ln2_b[l], ls, ybuf_s)

    return (yp.reshape(bp, lp, D_MODEL), ys.reshape(bs, ls, D_MODEL),
            jnp.stack(na_k_l, axis=1), jnp.stack(na_v_l, axis=1),
            jnp.stack(g_k_l, axis=1), jnp.stack(g_v_l, axis=1))
```

```python
import functools

import numpy as np
import jax
import jax.numpy as jnp
from jax import lax
from jax.experimental import pallas as pl
from jax.experimental.pallas import tpu as pltpu

F32 = jnp.float32
BF16 = jnp.bfloat16
I32 = jnp.int32
HIGHEST = lax.Precision.HIGHEST

D_MODEL = 1024
NOMINAL_DEPTH = 4
GRID_W = 64
HEAD_DIM = 64
NA_HEADS = 8
NA_WIN_R = 8
NA_WIN_C = 16
GQA_HEADS = 8
GQA_KV_HEADS = 2
GQA_REP = GQA_HEADS // GQA_KV_HEADS
BRANCH_W = 512
N_BRANCH = 4
CONV_K = 3
POOL_WINDOWS = (2, 4, 8, 16)
POOL_GW = BRANCH_W // len(POOL_WINDOWS)
POOL_HALO = max(POOL_WINDOWS) // 2
N_EXPERTS = 16
EXPERT_FF = 1024
EC_CAPACITY = 2
ROPE_THETA = 10000.0
LN_EPS = 1e-6
RMS_EPS = 1e-6
N_MOD = 6
DEEPNORM_ALPHA = (2 * NOMINAL_DEPTH) ** 0.25
NA_W = NA_HEADS * HEAD_DIM
GQA_QW = GQA_HEADS * HEAD_DIM
GQA_KW = GQA_KV_HEADS * HEAD_DIM
ATT_W = 3 * NA_W + GQA_QW + 2 * GQA_KW
CP_W = 4 * BRANCH_W
GATE_W = N_BRANCH * D_MODEL
LOG2E = 1.4426950408889634
Q_SCALE = HEAD_DIM ** -0.5 * LOG2E
NEG_BIG = -1e30
PAIR_W = 2 * HEAD_DIM

LANES = 128
SUBLANES = 8
VMEM_LIMIT = 56 * 1024 * 1024

TOK_TILE_IN = 512
MERGE_TILE = 512
MERGE_SUB = 256
TOK_TILE = 256
CHUNK = LANES
FFN_TILE = 256
GQA_QBLOCK = 512
NA_ROWS_PER_STEP = 4


def _cparams(sem):
    return pltpu.CompilerParams(dimension_semantics=sem, vmem_limit_bytes=VMEM_LIMIT)


def _adaln_kernel(cv_ref, w_ref, b_ref, o_ref):
    cv = cv_ref[...]
    s = cv * jax.nn.sigmoid(cv)
    o_ref[0] = jnp.dot(s, w_ref[0], precision=HIGHEST, preferred_element_type=F32) + b_ref[0]


def _adaln(cvecs, w_ada, b_ada):
    depth = w_ada.shape[0]
    r = cvecs.shape[0]
    tn = 1536
    nw = N_MOD * D_MODEL
    return pl.pallas_call(
        _adaln_kernel,
        grid=(depth, nw // tn),
        in_specs=[
            pl.BlockSpec((r, D_MODEL), lambda l, j: (0, 0)),
            pl.BlockSpec((1, D_MODEL, tn), lambda l, j: (l, 0, j)),
            pl.BlockSpec((1, 1, tn), lambda l, j: (l, 0, j)),
        ],
        out_specs=pl.BlockSpec((1, r, tn), lambda l, j: (l, 0, j)),
        out_shape=jax.ShapeDtypeStruct((depth, r, nw), F32),
        compiler_params=_cparams(("arbitrary", "arbitrary")),
        name="adaln",
    )(cvecs, w_ada, b_ada.reshape(depth, 1, nw))


def _lane_lo(shape):
    return (lax.broadcasted_iota(I32, shape, len(shape) - 1) % PAIR_W) < HEAD_DIM


def _pair_rms(x2, g2):
    lo = _lane_lo(x2.shape)
    sq = x2 * x2
    ss_lo = jnp.sum(jnp.where(lo, sq, 0.0), axis=-1, keepdims=True)
    ss_hi = jnp.sum(jnp.where(lo, 0.0, sq), axis=-1, keepdims=True)
    inv = lax.rsqrt(jnp.where(lo, ss_lo, ss_hi) * (1.0 / HEAD_DIM) + RMS_EPS)
    return x2 * inv * g2


def _inproj_kernel(x_ref, mod_ref, w_ref, *rest, with_state):
    if with_state:
        kn_ref, att_ref, cp_ref, nak_ref, nav_ref, gk_ref, gv_ref = rest
    else:
        att_ref, cp_ref = rest
    shift = mod_ref[0, 0:1, :]
    scale = mod_ref[0, 1:2, :]
    xm = (x_ref[...] * (1.0 + scale) + shift).astype(BF16)
    z = jnp.dot(xm, w_ref[...], preferred_element_type=F32)
    att_ref[...] = z[:, :ATT_W].astype(BF16)
    cp_ref[...] = z[:, ATT_W:]
    if with_state:
        koff = 3 * NA_W + GQA_QW
        nak_ref[...] = z[:, NA_W:2 * NA_W]
        nav_ref[...] = z[:, 2 * NA_W:3 * NA_W]
        gk_ref[...] = _pair_rms(z[:, koff:koff + GQA_KW], kn_ref[...])
        gv_ref[...] = z[:, koff + GQA_KW:koff + 2 * GQA_KW]


def _inproj(x, mod, w_a, seq_len, kn2=None):
    n = x.shape[0]
    t = min(TOK_TILE_IN, seq_len)
    per = n // mod.shape[0] // t
    with_state = kn2 is not None
    tile = lambda w: pl.BlockSpec((t, w), lambda i: (i, 0))
    in_specs = [
        pl.BlockSpec((t, D_MODEL), lambda i: (i, 0)),
        pl.BlockSpec((1, N_MOD, D_MODEL), lambda i: (i // per, 0, 0)),
        pl.BlockSpec((D_MODEL, ATT_W + CP_W), lambda i: (0, 0)),
    ]
    out_specs = [tile(ATT_W), tile(CP_W)]
    out_shape = [jax.ShapeDtypeStruct((n, ATT_W), BF16), jax.ShapeDtypeStruct((n, CP_W), F32)]
    args = [x, mod, w_a]
    if with_state:
        in_specs.append(pl.BlockSpec((1, GQA_KW), lambda i: (0, 0)))
        args.append(kn2)
        for w in (NA_W, NA_W, GQA_KW, GQA_KW):
            out_specs.append(tile(w))
            out_shape.append(jax.ShapeDtypeStruct((n, w), F32))
    return pl.pallas_call(
        functools.partial(_inproj_kernel, with_state=with_state),
        grid=(n // t,),
        in_specs=in_specs,
        out_specs=out_specs,
        out_shape=out_shape,
        compiler_params=_cparams(("arbitrary",)),
        name="inproj",
    )(*args)


def _pair_rope(x2, cos2, sin2):
    q = HEAD_DIM // 4
    first = (lax.broadcasted_iota(I32, x2.shape, 1) % (2 * q)) < q
    swapped = jnp.where(first, pltpu.roll(x2, PAIR_W - q, 1), pltpu.roll(x2, q, 1))
    return x2 * cos2 + swapped * sin2


def _dup_head(x2, g):
    lo = _lane_lo(x2.shape)
    other = pltpu.roll(x2, HEAD_DIM, 1)
    return jnp.where(lo, x2, other) if g == 0 else jnp.where(lo, other, x2)


def _dot_nt(a, b):
    return lax.dot_general(a, b, (((1,), (1,)), ((), ())), preferred_element_type=F32)


def _softmax_pv(scores, values):
    m = scores[0].max(axis=-1, keepdims=True)
    for s in scores[1:]:
        m = jnp.maximum(m, s.max(axis=-1, keepdims=True))
    l = None
    o = None
    for s, v in zip(scores, values):
        p = jnp.exp2(s - m)
        ls = p.sum(axis=-1, keepdims=True)
        os_ = jnp.dot(p.astype(BF16), v, preferred_element_type=F32)
        l = ls if l is None else l + ls
        o = os_ if o is None else o + os_
    return o / l


def _pair_attn(q2, keys, values, biases=None, stack=False, values_hi=None):
    lo = _lane_lo(q2.shape)
    m_rows = q2.shape[0]
    q_halves = [jnp.where(lo, q2, 0.0).astype(BF16), jnp.where(lo, 0.0, q2).astype(BF16)]

    def run(q, row0):
        scores = []
        for bi, k2 in enumerate(keys):
            s = _dot_nt(q, k2)
            if biases is not None and biases[bi] is not None:
                s = s + biases[bi][row0:row0 + q.shape[0]]
            scores.append(s)
        return _softmax_pv(scores, values)

    if stack:
        o = run(jnp.concatenate(q_halves, axis=0), 0)
        return jnp.where(lo, o[:m_rows], o[m_rows:])
    if values_hi is None:
        return jnp.where(lo, run(q_halves[0], 0), run(q_halves[1], m_rows))

    def run_mxu_sum(q, vals):
        scores = [_dot_nt(q, k2) for k2 in keys]
        m = scores[0].max(axis=-1, keepdims=True)
        for s in scores[1:]:
            m = jnp.maximum(m, s.max(axis=-1, keepdims=True))
        o = None
        for s, v in zip(scores, vals):
            os_ = jnp.dot(jnp.exp2(s - m).astype(BF16), v, preferred_element_type=F32)
            o = os_ if o is None else o + os_
        return o / pltpu.roll(o, HEAD_DIM, 1)

    return jnp.where(lo, run_mxu_sum(q_halves[0], values), run_mxu_sum(q_halves[1], values_hi))


def _ctx_attn_kernel(z_ref, qn_ref, kn_ref, o_ref):
    for p in range(NA_HEADS // 2):
        ps = slice(p * PAIR_W, (p + 1) * PAIR_W)
        q2 = z_ref[:, ps].astype(F32) * Q_SCALE
        k2 = z_ref[:, NA_W + p * PAIR_W:NA_W + (p + 1) * PAIR_W]
        v2 = z_ref[:, 2 * NA_W + p * PAIR_W:2 * NA_W + (p + 1) * PAIR_W]
        o_ref[:, ps] = _pair_attn(q2, [k2], [v2], stack=True).astype(BF16)
    qoff = 3 * NA_W
    koff = qoff + GQA_QW
    voff = koff + GQA_KW
    kn2 = _pair_rms(z_ref[:, koff:koff + GQA_KW].astype(F32), kn_ref[...])
    v2 = z_ref[:, voff:voff + GQA_KW].astype(F32)
    kd = [_dup_head(kn2, g).astype(BF16) for g in range(GQA_KV_HEADS)]
    vd = [_dup_head(v2, g).astype(BF16) for g in range(GQA_KV_HEADS)]
    for p in range(GQA_HEADS // 2):
        g = (2 * p) // GQA_REP
        q2 = _pair_rms(z_ref[:, qoff + p * PAIR_W:qoff + (p + 1) * PAIR_W].astype(F32), qn_ref[...]) * Q_SCALE
        o_ref[:, NA_W + p * PAIR_W:NA_W + (p + 1) * PAIR_W] = _pair_attn(q2, [kd[g]], [vd[g]],
                                                                          stack=True).astype(BF16)


def _ctx_attn(z_att, qn2, kn2, seq_len):
    n = z_att.shape[0]
    return pl.pallas_call(
        _ctx_attn_kernel,
        grid=(n // seq_len,),
        in_specs=[
            pl.BlockSpec((seq_len, ATT_W), lambda b: (b, 0)),
            pl.BlockSpec((1, PAIR_W), lambda b: (0, 0)),
            pl.BlockSpec((1, PAIR_W), lambda b: (0, 0)),
        ],
        out_specs=pl.BlockSpec((seq_len, NA_W + GQA_QW), lambda b: (b, 0)),
        out_shape=jax.ShapeDtypeStruct((n, NA_W + GQA_QW), BF16),
        compiler_params=_cparams(("arbitrary",)),
        name="ctx_attn",
    )(z_att, qn2, kn2)


def _na_row_start(r, rows):
    return jnp.clip(r - NA_WIN_R // 2, 0, rows - NA_WIN_R)


def _lat_na_kernel(q_ref, k_ref, v_ref, kc_ref, vc_ref, tb_ref, o_ref, *, rows, g_rows):
    r0 = pl.program_id(1) * g_rows
    wr = NA_WIN_R
    m2 = 2 * GRID_W
    lo = _lane_lo((GRID_W, PAIR_W))
    for p in range(NA_HEADS // 2):
        ps = slice(p * PAIR_W, (p + 1) * PAIR_W)
        kc2 = kc_ref[0, :, ps]
        vc2 = vc_ref[0, :, ps]
        qs = []
        for a in range(g_rows):
            q2 = q_ref[a * GRID_W:(a + 1) * GRID_W, ps].astype(F32) * Q_SCALE
            qs += [jnp.where(lo, q2, 0.0).astype(BF16), jnp.where(lo, 0.0, q2).astype(BF16)]
        qs_all = jnp.concatenate(qs, axis=0)
        s_ctx_all = _dot_nt(qs_all, kc2)
        o_loc, p_ctx, l_sum = [], [], []
        for a in range(g_rows):
            r = r0 + a
            rs = _na_row_start(r, rows)
            start = pl.multiple_of(rs * GRID_W, GRID_W)
            k2 = k_ref[pl.ds(start, wr * GRID_W), ps]
            v2 = v_ref[pl.ds(start, wr * GRID_W), ps]
            bias = tb_ref[rs - r + (wr - 1), 2 * p:2 * p + 2].reshape(m2, wr * GRID_W)
            s_loc = _dot_nt(qs_all[a * m2:(a + 1) * m2], k2) + bias
            s_ctx = s_ctx_all[a * m2:(a + 1) * m2]
            m = jnp.maximum(s_loc.max(axis=-1, keepdims=True), s_ctx.max(axis=-1, keepdims=True))
            e_loc = jnp.exp2(s_loc - m)
            e_ctx = jnp.exp2(s_ctx - m)
            l_sum.append(e_loc.sum(axis=-1, keepdims=True) + e_ctx.sum(axis=-1, keepdims=True))
            o_loc.append(jnp.dot(e_loc.astype(BF16), v2, preferred_element_type=F32))
            p_ctx.append(e_ctx.astype(BF16))
        o_ctx_all = jnp.dot(jnp.concatenate(p_ctx, axis=0), vc2, preferred_element_type=F32)
        for a in range(g_rows):
            o = (o_loc[a] + o_ctx_all[a * m2:(a + 1) * m2]) / l_sum[a]
            o_ref[a * GRID_W:(a + 1) * GRID_W, ps] = jnp.where(lo, o[:GRID_W], o[GRID_W:]).astype(BF16)


def _na_bias_table(rpb):
    wr = NA_WIN_R
    cols = np.arange(GRID_W)
    col_start = np.clip(cols - NA_WIN_C // 2, 0, GRID_W - NA_WIN_C)
    ck = np.arange(GRID_W)[None, :]
    valid = (ck >= col_start[:, None]) & (ck < col_start[:, None] + NA_WIN_C)
    rel = np.clip(ck - cols[:, None] + (NA_WIN_C - 1), 0, 2 * NA_WIN_C - 2)
    d = np.arange(wr)[:, None] + np.arange(wr)[None, :]
    t = rpb[:, d][:, :, :, rel] * LOG2E
    t = jnp.where(valid[None, None, None], t, NEG_BIG)
    t = t.transpose(1, 0, 3, 2, 4)
    return t.reshape(wr, NA_HEADS, GRID_W, wr * GRID_W)


def _lat_na(z_att, kc, vc, tb, seq_len):
    n = z_att.shape[0]
    nb = n // seq_len
    rows = seq_len // GRID_W
    assert rows >= NA_WIN_R
    wr = NA_WIN_R
    p = kc.shape[1]
    g_rows = NA_ROWS_PER_STEP
    blk = rows // g_rows
    return pl.pallas_call(
        functools.partial(_lat_na_kernel, rows=rows, g_rows=g_rows),
        grid=(nb, blk),
        in_specs=[
            pl.BlockSpec((g_rows * GRID_W, NA_W), lambda b, r: (b * blk + r, 0)),
            pl.BlockSpec((seq_len, NA_W), lambda b, r: (b, 1)),
            pl.BlockSpec((seq_len, NA_W), lambda b, r: (b, 2)),
            pl.BlockSpec((1, p, NA_W), lambda b, r: (b, 0, 0)),
            pl.BlockSpec((1, p, NA_W), lambda b, r: (b, 0, 0)),
            pl.BlockSpec((wr, NA_HEADS, GRID_W, wr * GRID_W), lambda b, r: (0, 0, 0, 0),
                         pipeline_mode=pl.Buffered(1)),
        ],
        out_specs=pl.BlockSpec((g_rows * GRID_W, NA_W), lambda b, r: (b * blk + r, 0)),
        out_shape=jax.ShapeDtypeStruct((n, NA_W), BF16),
        compiler_params=_cparams(("arbitrary", "arbitrary")),
        name="lat_na",
    )(z_att, z_att, z_att, kc, vc, tb)


def _lat_gqa_kernel(q_ref, k_ref, v_ref, kc_ref, vc_ref, cq_ref, sq_ref, ck_ref, sk_ref, qn_ref, kn_ref, o_ref):
    kn2 = _pair_rope(_pair_rms(k_ref[...].astype(F32), kn_ref[...]), ck_ref[...], sk_ref[...])
    v2 = v_ref[...].astype(F32)
    kc2 = kc_ref[0].astype(F32)
    vc2 = vc_ref[0].astype(F32)
    def with_ones(x2, g):
        d = _dup_head(x2, g)
        lo = _lane_lo(d.shape)
        return jnp.where(lo, d, 1.0).astype(BF16), jnp.where(lo, 1.0, d).astype(BF16)

    kd, kcd, v_lo, v_hi = [], [], [], []
    for g in range(GQA_KV_HEADS):
        kd.append(_dup_head(kn2, g).astype(BF16))
        kcd.append(_dup_head(kc2, g).astype(BF16))
        (a0, a1), (b0, b1) = with_ones(v2, g), with_ones(vc2, g)
        v_lo.append([a0, b0])
        v_hi.append([a1, b1])
    for p in range(GQA_HEADS // 2):
        g = (2 * p) // GQA_REP
        ps = slice(p * PAIR_W, (p + 1) * PAIR_W)
        q2 = _pair_rope(_pair_rms(q_ref[:, ps].astype(F32), qn_ref[...]), cq_ref[...], sq_ref[...]) * Q_SCALE
        o_ref[:, ps] = _pair_attn(q2, [kd[g], kcd[g]], v_lo[g], values_hi=v_hi[g]).astype(BF16)


def _rope_tables(seq_len):
    t = jnp.arange(seq_len)
    n_freq = HEAD_DIM // 4
    inv = ROPE_THETA ** (-jnp.arange(n_freq, dtype=F32) / n_freq)
    ang_r = (t // GRID_W).astype(F32)[:, None] * inv
    ang_c = (t % GRID_W).astype(F32)[:, None] * inv
    cr, sr, cc, sc = jnp.cos(ang_r), jnp.sin(ang_r), jnp.cos(ang_c), jnp.sin(ang_c)
    cos = jnp.concatenate([cr, cr, cc, cc] * 2, axis=-1)
    sin = jnp.concatenate([-sr, sr, -sc, sc] * 2, axis=-1)
    return cos, sin


def _lat_gqa(z_att, kc, vc, cos, sin, qn2, kn2, seq_len):
    n = z_att.shape[0]
    nb = n // seq_len
    tq = min(GQA_QBLOCK, seq_len)
    nqb = seq_len // tq
    p = kc.shape[1]
    qcol = (3 * NA_W) // GQA_QW
    kcol = (3 * NA_W + GQA_QW) // GQA_KW
    return pl.pallas_call(
        _lat_gqa_kernel,
        grid=(nb, nqb),
        in_specs=[
            pl.BlockSpec((tq, GQA_QW), lambda b, i: (b * nqb + i, qcol)),
            pl.BlockSpec((seq_len, GQA_KW), lambda b, i: (b, kcol)),
            pl.BlockSpec((seq_len, GQA_KW), lambda b, i: (b, kcol + 1)),
            pl.BlockSpec((1, p, GQA_KW), lambda b, i: (b, 0, 0)),
            pl.BlockSpec((1, p, GQA_KW), lambda b, i: (b, 0, 0)),
            pl.BlockSpec((tq, PAIR_W), lambda b, i: (i, 0)),
            pl.BlockSpec((tq, PAIR_W), lambda b, i: (i, 0)),
            pl.BlockSpec((seq_len, PAIR_W), lambda b, i: (0, 0)),
            pl.BlockSpec((seq_len, PAIR_W), lambda b, i: (0, 0)),
            pl.BlockSpec((1, PAIR_W), lambda b, i: (0, 0)),
            pl.BlockSpec((1, PAIR_W), lambda b, i: (0, 0)),
        ],
        out_specs=pl.BlockSpec((tq, GQA_QW), lambda b, i: (b * nqb + i, 0)),
        out_shape=jax.ShapeDtypeStruct((n, GQA_QW), BF16),
        compiler_params=_cparams(("arbitrary", "arbitrary")),
        name="lat_gqa",
    )(z_att, z_att, z_att, kc, vc, cos, sin, cos, sin, qn2, kn2)


def _layer_norm(x, g, b):
    mu = jnp.mean(x, axis=-1, keepdims=True)
    xc = x - mu
    var = jnp.mean(xc * xc, axis=-1, keepdims=True)
    return xc * lax.rsqrt(var + LN_EPS) * g + b


def _merge_kernel(x_ref, mod_ref, cp_ref, cpp_ref, cpn_ref, *rest, seq_len, n_att):
    att_refs = rest[:n_att]
    (wg_ref, wb_ref, wo_ref, pw_ref, cw_ref, cb_ref, ps_ref, lg_ref, lb_ref, wr_ref,
     x1_ref, xm2_ref, aff_ref, ucv_ref, upl_ref) = rest[n_att:]
    t = x_ref.shape[0]
    halo = POOL_HALO
    i = pl.program_id(0)
    pos0 = (i * t) % seq_len
    has_prev = pos0 > 0
    has_next = pos0 + t < seq_len

    shift1, scale1, gate1 = mod_ref[0, 0:1, :], mod_ref[0, 1:2, :], mod_ref[0, 2:3, :]
    shift2, scale2 = mod_ref[0, 3:4, :], mod_ref[0, 4:5, :]

    bw = BRANCH_W
    ucv_ref[halo:halo + t, :] = cp_ref[:, bw:2 * bw] * cp_ref[:, 2 * bw:3 * bw]
    upl_ref[halo:halo + t, :] = cp_ref[:, 3 * bw:4 * bw]
    ucv_ref[0:halo, :] = jnp.where(has_prev, cpp_ref[:, bw:2 * bw] * cpp_ref[:, 2 * bw:3 * bw], 0.0)
    upl_ref[0:halo, :] = jnp.where(has_prev, cpp_ref[:, 3 * bw:4 * bw], 0.0)
    ucv_ref[halo + t:, :] = jnp.where(has_next, cpn_ref[:, bw:2 * bw] * cpn_ref[:, 2 * bw:3 * bw], 0.0)
    upl_ref[halo + t:, :] = jnp.where(has_next, cpn_ref[:, 3 * bw:4 * bw], 0.0)

    wr = wr_ref[...]
    wh = wr.astype(BF16)
    wl = (wr - wh.astype(F32)).astype(BF16)

    th = min(t, MERGE_SUB)
    for r0 in range(0, t, th):
        rows = slice(r0, r0 + th)
        x = x_ref[rows, :]
        xm = (x * (1.0 + scale1) + shift1).astype(BF16)
        conv = (cw_ref[0:1, :] * ucv_ref[halo - 1 + r0:halo - 1 + r0 + th, :]
                + cw_ref[1:2, :] * ucv_ref[halo + r0:halo + r0 + th, :]
                + cw_ref[2:3, :] * ucv_ref[halo + 1 + r0:halo + 1 + r0 + th, :]) + cb_ref[...]
        o_conv = (cp_ref[rows, 0:bw] * conv).astype(BF16)

        pos = pos0 + r0 + lax.broadcasted_iota(I32, (th, 1), 0)
        mixed = []
        for g, win in enumerate(POOL_WINDOWS):
            gs = slice(g * POOL_GW, (g + 1) * POOL_GW)
            acc = None
            for dlt in range(-(win // 2), win // 2):
                term = upl_ref[halo + r0 + dlt:halo + r0 + dlt + th, gs]
                acc = term if acc is None else acc + term
            lo = jnp.maximum(pos - win // 2, 0)
            hi = jnp.minimum(pos + win // 2, seq_len)
            cnt = (hi - lo).astype(F32)
            pooled = acc / cnt - upl_ref[halo + r0:halo + r0 + th, gs]
            mixed.append(jnp.dot(pooled.astype(BF16), pw_ref[g], preferred_element_type=F32))
        o_pool = (jnp.concatenate(mixed, axis=-1) * ps_ref[...]).astype(BF16)

        if n_att == 1:
            branches = [att_refs[0][rows, 0:bw], att_refs[0][rows, bw:2 * bw], o_conv, o_pool]
        else:
            branches = [att_refs[0][rows, :], att_refs[1][rows, :], o_conv, o_pool]

        merged = None
        for nb in range(N_BRANCH):
            zg = jnp.dot(xm, wg_ref[:, nb * D_MODEL:(nb + 1) * D_MODEL], preferred_element_type=F32)
            proj = jnp.dot(branches[nb], wb_ref[nb], preferred_element_type=F32)
            term = jax.nn.sigmoid(zg) * proj
            merged = term if merged is None else merged + term
        y = jnp.dot(merged.astype(BF16), wo_ref[...], preferred_element_type=F32)
        x1 = _layer_norm(DEEPNORM_ALPHA * x + gate1 * y, lg_ref[...], lb_ref[...])
        x1_ref[rows, :] = x1
        xm2 = x1 * (1.0 + scale2) + shift2
        xm2_ref[rows, :] = xm2
        xh = xm2.astype(BF16)
        xl = (xm2 - xh.astype(F32)).astype(BF16)
        logits = (jnp.dot(xh, wh, preferred_element_type=F32) + jnp.dot(xl, wh, preferred_element_type=F32)
                  + jnp.dot(xh, wl, preferred_element_type=F32))
        m = logits.max(axis=-1, keepdims=True)
        e = jnp.exp(logits - m)
        aff_ref[rows, :] = e / e.sum(axis=-1, keepdims=True)


def _merge(x, mod, z_cp, atts, w_gates, w_branch, w_out, pool_w, conv_w, conv_b, pool_scale, ln_g, ln_b,
           w_router, seq_len):
    n = x.shape[0]
    t = min(MERGE_TILE, seq_len)
    per = n // mod.shape[0] // t
    hb = t // POOL_HALO
    nhb = n // POOL_HALO
    row = lambda a: a.reshape(1, -1)
    const2 = lambda i: (0, 0)
    const3 = lambda i: (0, 0, 0)
    once = lambda shp, imap: pl.BlockSpec(shp, imap, pipeline_mode=pl.Buffered(1))
    att_specs = [pl.BlockSpec((t, a.shape[1]), lambda i: (i, 0)) for a in atts]
    return pl.pallas_call(
        functools.partial(_merge_kernel, seq_len=seq_len, n_att=len(atts)),
        grid=(n // t,),
        in_specs=[
            pl.BlockSpec((t, D_MODEL), lambda i: (i, 0)),
            pl.BlockSpec((1, N_MOD, D_MODEL), lambda i: (i // per, 0, 0)),
            pl.BlockSpec((t, CP_W), lambda i: (i, 0)),
            pl.BlockSpec((POOL_HALO, CP_W), lambda i: (jnp.maximum(i * hb - 1, 0), 0)),
            pl.BlockSpec((POOL_HALO, CP_W), lambda i: (jnp.minimum((i + 1) * hb, nhb - 1), 0)),
            *att_specs,
            once((D_MODEL, GATE_W), const2),
            once((N_BRANCH, BRANCH_W, D_MODEL), const3),
            once((D_MODEL, D_MODEL), const2),
            pl.BlockSpec((len(POOL_WINDOWS), POOL_GW, POOL_GW), const3),
            pl.BlockSpec((CONV_K, BRANCH_W), const2),
            pl.BlockSpec((1, BRANCH_W), const2),
            pl.BlockSpec((1, BRANCH_W), const2),
            pl.BlockSpec((1, D_MODEL), const2),
            pl.BlockSpec((1, D_MODEL), const2),
            pl.BlockSpec((D_MODEL, N_EXPERTS), const2),
        ],
        out_specs=[
            pl.BlockSpec((t, D_MODEL), lambda i: (i, 0)),
            pl.BlockSpec((t, D_MODEL), lambda i: (i, 0)),
            pl.BlockSpec((t, N_EXPERTS), lambda i: (i, 0)),
        ],
        out_shape=[jax.ShapeDtypeStruct((n, D_MODEL), F32), jax.ShapeDtypeStruct((n, D_MODEL), F32),
                   jax.ShapeDtypeStruct((n, N_EXPERTS), F32)],
        scratch_shapes=[pltpu.VMEM((t + 2 * POOL_HALO, BRANCH_W), F32),
                        pltpu.VMEM((t + 2 * POOL_HALO, BRANCH_W), F32)],
        compiler_params=_cparams(("arbitrary",)),
        name="merge",
    )(x, mod, z_cp, z_cp, z_cp, *atts, w_gates, w_branch, w_out, pool_w, conv_w, row(conv_b), row(pool_scale),
      row(ln_g), row(ln_b), w_router)


def _topk_kernel(aff_ref, idx_ref, dst_ref, gp_ref, cnt_ref, thr_ref, rank_ref, *, cap, plane_stride):
    ne, nc, _ = aff_ref.shape
    capf = float(cap)
    bits_all = pltpu.bitcast(aff_ref[...], I32)

    def search(i, cur):
        cand = cur | (jnp.int32(1) << (30 - i))
        cnt = jnp.sum((bits_all >= cand).astype(F32), axis=(1, 2), keepdims=True)
        return jnp.where(cnt >= capf, cand, cur)

    thr = lax.fori_loop(0, 31, search, jnp.zeros((ne, 1, 1), I32))
    thr_ref[...] = jnp.broadcast_to(thr, thr_ref.shape)

    jj = lax.broadcasted_iota(I32, (CHUNK, CHUNK), 0)
    kk = lax.broadcasted_iota(I32, (CHUNK, CHUNK), 1)
    ut_incl = (jj <= kk).astype(BF16)
    lt_incl = (kk <= jj).astype(BF16)
    cc = lax.broadcasted_iota(I32, (nc, nc), 0)
    dd = lax.broadcasted_iota(I32, (nc, nc), 1)
    cl_excl = (dd < cc).astype(BF16)
    cl_incl = (dd <= cc).astype(BF16)
    eye = (jj == kk).astype(BF16)
    wide = 4 if cap % (4 * CHUNK) == 0 else 1
    lane_w = lax.broadcasted_iota(I32, (1, wide * CHUNK), 1)
    sub_cw = lax.broadcasted_iota(I32, (nc, wide * CHUNK), 0).astype(F32)
    sub_jw = lax.broadcasted_iota(I32, (CHUNK, wide * CHUNK), 0).astype(F32)
    rank_ref[...] = jnp.zeros(rank_ref.shape, F32)
    gp_ref[...] = jnp.full(gp_ref.shape, -1.0, F32)

    def per_expert(e, carry):
        a = aff_ref[e]
        b = pltpu.bitcast(a, I32)
        t = thr_ref[e][0:1, :]
        gt = b > t
        eq = b == t
        need = capf - jnp.sum(gt.astype(F32), keepdims=True)
        eqf = eq.astype(F32)
        incl_eq = jnp.dot(eqf.astype(BF16), ut_incl, preferred_element_type=F32)
        tot_eq = jnp.broadcast_to(incl_eq[:, CHUNK - 1:CHUNK], (nc, CHUNK))
        cum_eq = jnp.dot(cl_excl, tot_eq.astype(BF16), preferred_element_type=F32)
        rank = cum_eq + incl_eq - eqf
        sel = jnp.logical_or(gt, jnp.logical_and(eq, rank < need))
        self_ = sel.astype(F32)

        plane = rank_ref[...]
        rank_ref[...] = plane + self_
        for k in range(ne):
            gp_ref[k] = jnp.where(jnp.logical_and(sel, plane == float(k)), a, gp_ref[k])
        plane_tb = _dot_nt(eye, plane.astype(BF16)).astype(BF16)

        selb = self_.astype(BF16)
        incl = jnp.dot(selb, ut_incl, preferred_element_type=F32)
        tot = jnp.broadcast_to(incl[:, CHUNK - 1:CHUNK], (nc, CHUNK))
        cum_incl = jnp.dot(cl_incl, tot.astype(BF16), preferred_element_type=F32)
        incl_tb = _dot_nt(lt_incl, selb).astype(BF16)

        def per_block(sb, c2):
            s_row = (sb * (wide * CHUNK) + lane_w).astype(F32)
            m = cum_w <= s_row
            c_of_s = jnp.sum(m.astype(F32), axis=0, keepdims=True)
            excl_s = jnp.max(jnp.where(m, cum_w, 0.0), axis=0, keepdims=True)
            onehot_t = (sub_cw == c_of_s).astype(BF16)
            rows_t = jnp.dot(incl_tb, onehot_t, preferred_element_type=F32)
            s_local = s_row - excl_s
            t_local = jnp.sum((rows_t <= s_local).astype(F32), axis=0, keepdims=True)
            tok = (c_of_s * float(CHUNK) + t_local).astype(I32)
            plane_rows = jnp.dot(plane_tb, onehot_t, preferred_element_type=F32)
            plane_s = jnp.sum(jnp.where(sub_jw == t_local, plane_rows, 0.0), axis=0, keepdims=True)
            dst = plane_s.astype(I32) * plane_stride + tok
            for u in range(wide):
                idx_ref[e, pl.ds(sb * wide + u, 1), :] = tok[:, u * CHUNK:(u + 1) * CHUNK]
                dst_ref[e, pl.ds(sb * wide + u, 1), :] = dst[:, u * CHUNK:(u + 1) * CHUNK]
            return c2

        cum_w = jnp.concatenate([cum_incl] * wide, axis=1)
        lax.fori_loop(0, cap // (wide * CHUNK), per_block, 0)
        return carry

    lax.fori_loop(0, ne, per_expert, 0)
    cnt_ref[...] = jnp.broadcast_to(jnp.max(rank_ref[...], axis=1, keepdims=True), cnt_ref.shape).astype(I32)


def _topk(aff3, cap, plane_stride):
    ne, nc, _ = aff3.shape
    full3 = lambda shp: pl.BlockSpec(shp, lambda i: (0, 0, 0))
    return pl.pallas_call(
        functools.partial(_topk_kernel, cap=cap, plane_stride=plane_stride),
        grid=(1,),
        in_specs=[full3((ne, nc, CHUNK))],
        out_specs=[full3((ne, cap // CHUNK, CHUNK)), full3((ne, cap // CHUNK, CHUNK)), full3((ne, nc, CHUNK)),
                   pl.BlockSpec((nc, CHUNK), lambda i: (0, 0))],
        out_shape=[jax.ShapeDtypeStruct((ne, cap // CHUNK, CHUNK), I32),
                   jax.ShapeDtypeStruct((ne, cap // CHUNK, CHUNK), I32),
                   jax.ShapeDtypeStruct((ne, nc, CHUNK), F32),
                   jax.ShapeDtypeStruct((nc, CHUNK), I32)],
        scratch_shapes=[pltpu.VMEM((ne, SUBLANES, CHUNK), I32), pltpu.VMEM((nc, CHUNK), F32)],
        compiler_params=_cparams(("arbitrary",)),
        name="topk",
    )(aff3)


def _ffn_kernel(idx_ref, idxn_ref, dst_ref, x_hbm, wg_ref, wu_ref, wd_ref, y_in_hbm, y_hbm, xs_ref, ys_ref, gsem_ref,
                ssem_ref, *, steps):
    tm = xs_ref.shape[1]
    s = pl.program_id(0) * pl.num_programs(1) + pl.program_id(1)

    def gather_rows(idx_smem, buf_slot):
        for j in range(tm):
            pltpu.make_async_copy(x_hbm.at[pl.ds(idx_smem[0, 0, j], 1), :], xs_ref.at[buf_slot, pl.ds(j, 1), :],
                                  gsem_ref.at[buf_slot]).start()

    def gathered(buf_slot):
        return pltpu.make_async_copy(x_hbm.at[pl.ds(0, tm), :], xs_ref.at[buf_slot], gsem_ref.at[buf_slot])

    def scattered(buf_slot):
        return pltpu.make_async_copy(ys_ref.at[buf_slot], y_hbm.at[pl.ds(0, tm), :], ssem_ref.at[buf_slot])

    @pl.when(s == 0)
    def _():
        gather_rows(idx_ref, 0)

    def step(slot):
        @pl.when(s + 1 < steps)
        def _():
            gather_rows(idxn_ref, 1 - slot)

        gathered(slot).wait()
        xs = xs_ref[slot].astype(BF16)
        hg = jnp.dot(xs, wg_ref[0], preferred_element_type=F32)
        hu = jnp.dot(xs, wu_ref[0], preferred_element_type=F32)
        hdn = (hg * jax.nn.sigmoid(hg) * hu).astype(BF16)
        ye = jnp.dot(hdn, wd_ref[0], preferred_element_type=F32)

        @pl.when(s >= 2)
        def _():
            scattered(slot).wait()

        ys_ref[slot] = ye
        for j in range(tm):
            pltpu.make_async_copy(ys_ref.at[slot, pl.ds(j, 1), :], y_hbm.at[pl.ds(dst_ref[0, 0, j], 1), :],
                                  ssem_ref.at[slot]).start(priority=j % 2)

        @pl.when(s == steps - 1)
        def _():
            scattered(slot).wait()
            if steps > 1:
                scattered(1 - slot).wait()

    for parity in range(2):
        pl.when(s % 2 == parity)(functools.partial(step, parity))


def _ffn(idx_tiles, dst_tiles, xm2, w_gate, w_up, w_down, cap, ybuf):
    tm = idx_tiles.shape[2]
    nt = cap // tm
    steps = N_EXPERTS * nt
    wspec = lambda: pl.BlockSpec((1, D_MODEL, EXPERT_FF), lambda e, i: (e, 0, 0))
    return pl.pallas_call(
        functools.partial(_ffn_kernel, steps=steps),
        grid=(N_EXPERTS, nt),
        in_specs=[
            pl.BlockSpec((1, 1, tm), lambda e, i: (e * nt + i, 0, 0), memory_space=pltpu.SMEM),
            pl.BlockSpec((1, 1, tm), lambda e, i: (jnp.minimum(e * nt + i + 1, steps - 1), 0, 0),
                         memory_space=pltpu.SMEM),
            pl.BlockSpec((1, 1, tm), lambda e, i: (e * nt + i, 0, 0), memory_space=pltpu.SMEM),
            pl.BlockSpec(memory_space=pl.ANY),
            wspec(), wspec(),
            pl.BlockSpec((1, EXPERT_FF, D_MODEL), lambda e, i: (e, 0, 0)),
            pl.BlockSpec(memory_space=pl.ANY),
        ],
        out_specs=pl.BlockSpec(memory_space=pl.ANY),
        out_shape=jax.ShapeDtypeStruct(ybuf.shape, F32),
        input_output_aliases={7: 0},
        scratch_shapes=[pltpu.VMEM((2, tm, D_MODEL), F32), pltpu.VMEM((2, tm, D_MODEL), F32),
                        pltpu.SemaphoreType.DMA((2,)), pltpu.SemaphoreType.DMA((2,))],
        compiler_params=_cparams(("arbitrary", "arbitrary")),
        name="ffn",
    )(idx_tiles, idx_tiles, dst_tiles, xm2, w_gate, w_up, w_down, ybuf)


def _combine_kernel(kmax_ref, y_hbm, x1_ref, mod_ref, gp_ref, lg_ref, lb_ref, o_ref, planes_ref, acc_ref, sem_ref,
                    *, n_planes):
    t = x1_ref.shape[0]
    i = pl.program_id(0)
    ntile = pl.num_programs(0)
    slot = i % 2

    def for_planes(tile, buf_slot, fn):
        for k in range(n_planes):
            @pl.when(k < kmax_ref[tile])
            def _(k=k):
                fn(pltpu.make_async_copy(y_hbm.at[k, pl.ds(tile * t, t), :], planes_ref.at[buf_slot, k],
                                         sem_ref.at[buf_slot]))

    @pl.when(i == 0)
    def _():
        for_planes(0, 0, lambda cp: cp.start())

    @pl.when(i + 1 < ntile)
    def _():
        for_planes(i + 1, 1 - slot, lambda cp: cp.start())

    for_planes(i, slot, lambda cp: cp.wait())

    acc_ref[...] = jnp.zeros(acc_ref.shape, F32)
    for k in range(n_planes):
        @pl.when(k < kmax_ref[i])
        def _(k=k):
            w = gp_ref[:, k:k + 1]
            acc_ref[...] += jnp.where(w >= 0.0, w * planes_ref[slot, k], 0.0)

    gate2 = mod_ref[0, 5:6, :]
    o_ref[...] = _layer_norm(DEEPNORM_ALPHA * x1_ref[...] + gate2 * acc_ref[...], lg_ref[...], lb_ref[...])


def _combine(kmax, y_planes, x1, mod, gp_tok, ln_g, ln_b, seq_len):
    n = x1.shape[0]
    n_planes = y_planes.shape[0]
    t = min(TOK_TILE, seq_len)
    per = n // mod.shape[0] // t
    row = lambda a: a.reshape(1, -1)
    grid_spec = pltpu.PrefetchScalarGridSpec(
        num_scalar_prefetch=1,
        grid=(n // t,),
        in_specs=[
            pl.BlockSpec(memory_space=pl.ANY),
            pl.BlockSpec((t, D_MODEL), lambda i, *_: (i, 0)),
            pl.BlockSpec((1, N_MOD, D_MODEL), lambda i, *_: (i // per, 0, 0)),
            pl.BlockSpec((t, n_planes), lambda i, *_: (i, 0)),
            pl.BlockSpec((1, D_MODEL), lambda i, *_: (0, 0)),
            pl.BlockSpec((1, D_MODEL), lambda i, *_: (0, 0)),
        ],
        out_specs=pl.BlockSpec((t, D_MODEL), lambda i, *_: (i, 0)),
        scratch_shapes=[pltpu.VMEM((2, n_planes, t, D_MODEL), F32), pltpu.VMEM((t, D_MODEL), F32),
                        pltpu.SemaphoreType.DMA((2,))],
    )
    return pl.pallas_call(
        functools.partial(_combine_kernel, n_planes=n_planes),
        grid_spec=grid_spec,
        out_shape=jax.ShapeDtypeStruct((n, D_MODEL), F32),
        compiler_params=_cparams(("arbitrary",)),
        name="combine",
    )(kmax, y_planes, x1, mod, gp_tok, row(ln_g), row(ln_b))


def _moe(x1, xm2, aff, mod, w_gate, w_up, w_down, ln_g, ln_b, seq_len, ybuf):
    n = x1.shape[0]
    cap = EC_CAPACITY * n // N_EXPERTS
    nc = n // CHUNK
    t = min(TOK_TILE, seq_len)
    stride = ybuf.shape[0] // N_EXPERTS
    aff3 = aff.T.reshape(N_EXPERTS, nc, CHUNK)
    idx3, dst3, gp3, cnt = _topk(aff3, cap, stride)
    gp_tok = gp3.reshape(N_EXPERTS, n).T
    kmax = jnp.max(cnt[:, 0].reshape(n // t, t // CHUNK), axis=1)
    tm = min(FFN_TILE, cap)
    idx_tiles = idx3.reshape(N_EXPERTS * cap // tm, 1, tm)
    dst_tiles = dst3.reshape(N_EXPERTS * cap // tm, 1, tm)
    ybuf = _ffn(idx_tiles, dst_tiles, xm2, w_gate, w_up, w_down, cap, ybuf)
    y = _combine(kmax, ybuf.reshape(N_EXPERTS, stride, D_MODEL), x1, mod, gp_tok, ln_g, ln_b, seq_len)
    return y, ybuf


def kernel(x_prompt, x_sample, cache_na_k, cache_na_v, cache_gqa_k, cache_gqa_v, c, c_ctx, w_ada, b_ada, w_in,
           na_rpb, qn_g, kn_g, conv_w, conv_b, pool_w, pool_scale, w_branch, w_out, ln1_g, ln1_b, ln2_g, ln2_b,
           w_router, w_gate, w_up, w_down):
    depth = w_in.shape[0]
    bp, lp, _ = x_prompt.shape
    bs, ls, _ = x_sample.shape
    past = cache_na_k.shape[2]

    mods = _adaln(jnp.concatenate([c_ctx[None, :], c], axis=0), w_ada, b_ada)
    mods = mods.reshape(depth, 1 + bs, N_MOD, D_MODEL)
    cos, sin = _rope_tables(ls)

    yp = x_prompt.reshape(bp * lp, D_MODEL)
    ys = x_sample.reshape(bs * ls, D_MODEL)
    ybuf_p = jnp.zeros((N_EXPERTS * bp * lp, D_MODEL), F32)
    ybuf_s = jnp.zeros((N_EXPERTS * bs * ls, D_MODEL), F32)
    na_k_l, na_v_l, g_k_l, g_v_l = [], [], [], []
    for l in range(depth):
        mod_c = mods[l, 0:1]
        mod_s = mods[l, 1:]
        w_a = w_in[l, :, :ATT_W + CP_W].astype(BF16)
        w_gates = w_in[l, :, ATT_W + CP_W:].astype(BF16)
        w_br = w_branch[l].astype(BF16)
        w_o = w_out[l].astype(BF16)
        p_w = pool_w[l].astype(BF16)
        wg, wu, wd = w_gate[l].astype(BF16), w_up[l].astype(BF16), w_down[l].astype(BF16)
        merge_w = (w_gates, w_br, w_o, p_w, conv_w[l], conv_b[l], pool_scale[l], ln1_g[l], ln1_b[l], w_router[l])

        qn2 = jnp.tile(qn_g[l], 2).reshape(1, PAIR_W)
        kn2 = jnp.tile(kn_g[l], 2).reshape(1, PAIR_W)
        cache = lambda a, w: a[:, l].reshape(bs, past, w).astype(BF16)

        z_att, z_cp, na_k, na_v, g_k, g_v = _inproj(yp, mod_c, w_a, bp * lp, kn2)
        o_att = _ctx_attn(z_att, qn2, kn2, lp)
        na_k_l.append(na_k.reshape(bp, lp, NA_HEADS, HEAD_DIM))
        na_v_l.append(na_v.reshape(bp, lp, NA_HEADS, HEAD_DIM))
        g_k_l.append(g_k.reshape(bp, lp, GQA_KV_HEADS, HEAD_DIM))
        g_v_l.append(g_v.reshape(bp, lp, GQA_KV_HEADS, HEAD_DIM))
        x1, xm2, aff = _merge(yp, mod_c, z_cp, [o_att], *merge_w, lp)
        yp, ybuf_p = _moe(x1, xm2, aff, mod_c, wg, wu, wd, ln2_g[l], ln2_b[l], lp, ybuf_p)

        z_att, z_cp = _inproj(ys, mod_s, w_a, ls)
        tb = _na_bias_table(na_rpb[l])
        o_na = _lat_na(z_att, cache(cache_na_k, NA_W), cache(cache_na_v, NA_W), tb, ls)
        o_gqa = _lat_gqa(z_att, cache(cache_gqa_k, GQA_KW), cache(cache_gqa_v, GQA_KW), cos, sin, qn2, kn2, ls)
        x1, xm2, aff = _merge(ys, mod_s, z_cp, [o_na, o_gqa], *merge_w, ls)
        ys, ybuf_s = _moe(x1, xm2, aff, mod_s, wg, wu, wd, ln2_g[l], ln2_b[l], ls, ybuf_s)

    return (yp.reshape(bp, lp, D_MODEL), ys.reshape(bs, ls, D_MODEL),
            jnp.stack(na_k_l, axis=1), jnp.stack(na_v_l, axis=1),
            jnp.stack(g_k_l, axis=1), jnp.stack(g_v_l, axis=1))
```

```python
import functools

import numpy as np
import jax
import jax.numpy as jnp
from jax import lax
from jax.experimental import pallas as pl
from jax.experimental.pallas import tpu as pltpu

F32 = jnp.float32
BF16 = jnp.bfloat16
I32 = jnp.int32
HIGHEST = lax.Precision.HIGHEST

D_MODEL = 1024
NOMINAL_DEPTH = 4
GRID_W = 64
HEAD_DIM = 64
NA_HEADS = 8
NA_WIN_R = 8
NA_WIN_C = 16
GQA_HEADS = 8
GQA_KV_HEADS = 2
GQA_REP = GQA_HEADS // GQA_KV_HEADS
BRANCH_W = 512
N_BRANCH = 4
CONV_K = 3
POOL_WINDOWS = (2, 4, 8, 16)
POOL_GW = BRANCH_W // len(POOL_WINDOWS)
POOL_HALO = max(POOL_WINDOWS) // 2
N_EXPERTS = 16
EXPERT_FF = 1024
EC_CAPACITY = 2
ROPE_THETA = 10000.0
LN_EPS = 1e-6
RMS_EPS = 1e-6
N_MOD = 6
DEEPNORM_ALPHA = (2 * NOMINAL_DEPTH) ** 0.25
NA_W = NA_HEADS * HEAD_DIM
GQA_QW = GQA_HEADS * HEAD_DIM
GQA_KW = GQA_KV_HEADS * HEAD_DIM
ATT_W = 3 * NA_W + GQA_QW + 2 * GQA_KW
CP_W = 4 * BRANCH_W
GATE_W = N_BRANCH * D_MODEL
LOG2E = 1.4426950408889634
Q_SCALE = HEAD_DIM ** -0.5 * LOG2E
NEG_BIG = -1e30
PAIR_W = 2 * HEAD_DIM

LANES = 128
SUBLANES = 8
VMEM_LIMIT = 56 * 1024 * 1024

TOK_TILE_IN = 512
MERGE_TILE = 512
MERGE_SUB = 256
TOK_TILE = 256
CHUNK = LANES
FFN_TILE = 256
GQA_QBLOCK = 512
NA_ROWS_PER_STEP = 4


def _cparams(sem):
    return pltpu.CompilerParams(dimension_semantics=sem, vmem_limit_bytes=VMEM_LIMIT)


def _sigmoid(x):
    return 0.5 * jnp.tanh(0.5 * x) + 0.5


def _adaln_kernel(cv_ref, w_ref, b_ref, o_ref):
    cv = cv_ref[...]
    s = cv * jax.nn.sigmoid(cv)
    o_ref[0] = jnp.dot(s, w_ref[0], precision=HIGHEST, preferred_element_type=F32) + b_ref[0]


def _adaln(cvecs, w_ada, b_ada):
    depth = w_ada.shape[0]
    r = cvecs.shape[0]
    tn = 1536
    nw = N_MOD * D_MODEL
    return pl.pallas_call(
        _adaln_kernel,
        grid=(depth, nw // tn),
        in_specs=[
            pl.BlockSpec((r, D_MODEL), lambda l, j: (0, 0)),
            pl.BlockSpec((1, D_MODEL, tn), lambda l, j: (l, 0, j)),
            pl.BlockSpec((1, 1, tn), lambda l, j: (l, 0, j)),
        ],
        out_specs=pl.BlockSpec((1, r, tn), lambda l, j: (l, 0, j)),
        out_shape=jax.ShapeDtypeStruct((depth, r, nw), F32),
        compiler_params=_cparams(("arbitrary", "arbitrary")),
        name="adaln",
    )(cvecs, w_ada, b_ada.reshape(depth, 1, nw))


def _lane_lo(shape):
    return (lax.broadcasted_iota(I32, shape, len(shape) - 1) % PAIR_W) < HEAD_DIM


def _pair_rms(x2, g2):
    lo = _lane_lo(x2.shape)
    sq = x2 * x2
    ss_lo = jnp.sum(jnp.where(lo, sq, 0.0), axis=-1, keepdims=True)
    ss_hi = jnp.sum(jnp.where(lo, 0.0, sq), axis=-1, keepdims=True)
    inv = lax.rsqrt(jnp.where(lo, ss_lo, ss_hi) * (1.0 / HEAD_DIM) + RMS_EPS)
    return x2 * inv * g2


def _pair_rope(x2, cos2, sin2):
    q = HEAD_DIM // 4
    first = (lax.broadcasted_iota(I32, x2.shape, 1) % (2 * q)) < q
    swapped = jnp.where(first, pltpu.roll(x2, PAIR_W - q, 1), pltpu.roll(x2, q, 1))
    return x2 * cos2 + swapped * sin2


def _inproj_kernel(x_ref, mod_ref, w_ref, *rest, with_state):
    if with_state:
        kn_ref, att_ref, cp_ref, nak_ref, nav_ref, gk_ref, gv_ref = rest
    else:
        qn_ref, kn_ref, cos_ref, sin_ref, att_ref, cp_ref = rest
    shift = mod_ref[0, 0:1, :]
    scale = mod_ref[0, 1:2, :]
    xm = (x_ref[...] * (1.0 + scale) + shift).astype(BF16)
    z = jnp.dot(xm, w_ref[...], preferred_element_type=F32)
    cp_ref[...] = z[:, ATT_W:]
    qoff = 3 * NA_W
    koff = qoff + GQA_QW
    if with_state:
        att_ref[...] = z[:, :ATT_W].astype(BF16)
        nak_ref[...] = z[:, NA_W:2 * NA_W]
        nav_ref[...] = z[:, 2 * NA_W:3 * NA_W]
        gk_ref[...] = _pair_rms(z[:, koff:koff + GQA_KW], kn_ref[...])
        gv_ref[...] = z[:, koff + GQA_KW:koff + 2 * GQA_KW]
    else:
        att_ref[:, :qoff] = z[:, :qoff].astype(BF16)
        for p in range(GQA_HEADS // 2):
            ps = slice(qoff + p * PAIR_W, qoff + (p + 1) * PAIR_W)
            q2 = _pair_rope(_pair_rms(z[:, ps], qn_ref[...]), cos_ref[...], sin_ref[...]) * Q_SCALE
            att_ref[:, ps] = q2.astype(BF16)
        k2 = _pair_rope(_pair_rms(z[:, koff:koff + GQA_KW], kn_ref[...]), cos_ref[...], sin_ref[...])
        att_ref[:, koff:koff + GQA_KW] = k2.astype(BF16)
        att_ref[:, koff + GQA_KW:] = z[:, koff + GQA_KW:ATT_W].astype(BF16)


def _inproj(x, mod, w_a, seq_len, kn2, qn2=None, cos=None, sin=None):
    n = x.shape[0]
    t = min(TOK_TILE_IN, seq_len)
    per = n // mod.shape[0] // t
    with_state = qn2 is None
    tile = lambda w: pl.BlockSpec((t, w), lambda i: (i, 0))
    vec = pl.BlockSpec((1, PAIR_W), lambda i: (0, 0))
    in_specs = [
        pl.BlockSpec((t, D_MODEL), lambda i: (i, 0)),
        pl.BlockSpec((1, N_MOD, D_MODEL), lambda i: (i // per, 0, 0)),
        pl.BlockSpec((D_MODEL, ATT_W + CP_W), lambda i: (0, 0)),
    ]
    out_specs = [tile(ATT_W), tile(CP_W)]
    out_shape = [jax.ShapeDtypeStruct((n, ATT_W), BF16), jax.ShapeDtypeStruct((n, CP_W), F32)]
    if with_state:
        in_specs.append(vec)
        args = [x, mod, w_a, kn2]
        for w in (NA_W, NA_W, GQA_KW, GQA_KW):
            out_specs.append(tile(w))
            out_shape.append(jax.ShapeDtypeStruct((n, w), F32))
    else:
        nblk = seq_len // t
        pos = pl.BlockSpec((t, PAIR_W), lambda i: (i % nblk, 0))
        in_specs += [vec, vec, pos, pos]
        args = [x, mod, w_a, qn2, kn2, cos, sin]
    return pl.pallas_call(
        functools.partial(_inproj_kernel, with_state=with_state),
        grid=(n // t,),
        in_specs=in_specs,
        out_specs=out_specs,
        out_shape=out_shape,
        compiler_params=_cparams(("arbitrary",)),
        name="inproj",
    )(*args)


def _dup_head(x2, g):
    lo = _lane_lo(x2.shape)
    other = pltpu.roll(x2, HEAD_DIM, 1)
    return jnp.where(lo, x2, other) if g == 0 else jnp.where(lo, other, x2)


def _dot_nt(a, b):
    return lax.dot_general(a, b, (((1,), (1,)), ((), ())), preferred_element_type=F32)


def _softmax_pv(scores, values):
    m = scores[0].max(axis=-1, keepdims=True)
    for s in scores[1:]:
        m = jnp.maximum(m, s.max(axis=-1, keepdims=True))
    l = None
    o = None
    for s, v in zip(scores, values):
        p = jnp.exp2(s - m)
        ls = p.sum(axis=-1, keepdims=True)
        os_ = jnp.dot(p.astype(BF16), v, preferred_element_type=F32)
        l = ls if l is None else l + ls
        o = os_ if o is None else o + os_
    return o / l


def _pair_attn(q2, keys, values, biases=None, stack=False, values_hi=None):
    lo = _lane_lo(q2.shape)
    m_rows = q2.shape[0]
    q_halves = [jnp.where(lo, q2, 0.0).astype(BF16), jnp.where(lo, 0.0, q2).astype(BF16)]

    def run(q, row0):
        scores = []
        for bi, k2 in enumerate(keys):
            s = _dot_nt(q, k2)
            if biases is not None and biases[bi] is not None:
                s = s + biases[bi][row0:row0 + q.shape[0]]
            scores.append(s)
        return _softmax_pv(scores, values)

    if stack:
        o = run(jnp.concatenate(q_halves, axis=0), 0)
        return jnp.where(lo, o[:m_rows], o[m_rows:])
    if values_hi is None:
        return jnp.where(lo, run(q_halves[0], 0), run(q_halves[1], m_rows))

    def run_mxu_sum(q, vals):
        scores = [_dot_nt(q, k2) for k2 in keys]
        m = scores[0].max(axis=-1, keepdims=True)
        for s in scores[1:]:
            m = jnp.maximum(m, s.max(axis=-1, keepdims=True))
        o = None
        for s, v in zip(scores, vals):
            os_ = jnp.dot(jnp.exp2(s - m).astype(BF16), v, preferred_element_type=F32)
            o = os_ if o is None else o + os_
        return o / pltpu.roll(o, HEAD_DIM, 1)

    return jnp.where(lo, run_mxu_sum(q_halves[0], values), run_mxu_sum(q_halves[1], values_hi))


def _ctx_attn_kernel(z_ref, qn_ref, kn_ref, o_ref):
    for p in range(NA_HEADS // 2):
        ps = slice(p * PAIR_W, (p + 1) * PAIR_W)
        q2 = z_ref[:, ps].astype(F32) * Q_SCALE
        k2 = z_ref[:, NA_W + p * PAIR_W:NA_W + (p + 1) * PAIR_W]
        v2 = z_ref[:, 2 * NA_W + p * PAIR_W:2 * NA_W + (p + 1) * PAIR_W]
        o_ref[:, ps] = _pair_attn(q2, [k2], [v2], stack=True).astype(BF16)
    qoff = 3 * NA_W
    koff = qoff + GQA_QW
    voff = koff + GQA_KW
    kn2 = _pair_rms(z_ref[:, koff:koff + GQA_KW].astype(F32), kn_ref[...])
    v2 = z_ref[:, voff:voff + GQA_KW].astype(F32)
    kd = [_dup_head(kn2, g).astype(BF16) for g in range(GQA_KV_HEADS)]
    vd = [_dup_head(v2, g).astype(BF16) for g in range(GQA_KV_HEADS)]
    for p in range(GQA_HEADS // 2):
        g = (2 * p) // GQA_REP
        q2 = _pair_rms(z_ref[:, qoff + p * PAIR_W:qoff + (p + 1) * PAIR_W].astype(F32), qn_ref[...]) * Q_SCALE
        o_ref[:, NA_W + p * PAIR_W:NA_W + (p + 1) * PAIR_W] = _pair_attn(q2, [kd[g]], [vd[g]],
                                                                          stack=True).astype(BF16)


def _ctx_attn(z_att, qn2, kn2, seq_len):
    n = z_att.shape[0]
    return pl.pallas_call(
        _ctx_attn_kernel,
        grid=(n // seq_len,),
        in_specs=[
            pl.BlockSpec((seq_len, ATT_W), lambda b: (b, 0)),
            pl.BlockSpec((1, PAIR_W), lambda b: (0, 0)),
            pl.BlockSpec((1, PAIR_W), lambda b: (0, 0)),
        ],
        out_specs=pl.BlockSpec((seq_len, NA_W + GQA_QW), lambda b: (b, 0)),
        out_shape=jax.ShapeDtypeStruct((n, NA_W + GQA_QW), BF16),
        compiler_params=_cparams(("arbitrary",)),
        name="ctx_attn",
    )(z_att, qn2, kn2)


def _na_row_start(r, rows):
    return jnp.clip(r - NA_WIN_R // 2, 0, rows - NA_WIN_R)


def _lat_na_kernel(q_ref, k_ref, v_ref, kc_ref, vc_ref, tb_ref, o_ref, *, rows, g_rows):
    r0 = pl.program_id(1) * g_rows
    wr = NA_WIN_R
    m2 = 2 * GRID_W
    lo = _lane_lo((GRID_W, PAIR_W))
    for p in range(NA_HEADS // 2):
        ps = slice(p * PAIR_W, (p + 1) * PAIR_W)
        kc2 = kc_ref[0, :, ps]
        vc2 = vc_ref[0, :, ps]
        qs = []
        for a in range(g_rows):
            q2 = q_ref[a * GRID_W:(a + 1) * GRID_W, ps].astype(F32) * Q_SCALE
            qs += [jnp.where(lo, q2, 0.0).astype(BF16), jnp.where(lo, 0.0, q2).astype(BF16)]
        qs_all = jnp.concatenate(qs, axis=0)
        s_ctx_all = _dot_nt(qs_all, kc2)
        o_loc, p_ctx, l_sum = [], [], []
        for a in range(g_rows):
            r = r0 + a
            rs = _na_row_start(r, rows)
            start = pl.multiple_of(rs * GRID_W, GRID_W)
            k2 = k_ref[pl.ds(start, wr * GRID_W), ps]
            v2 = v_ref[pl.ds(start, wr * GRID_W), ps]
            bias = tb_ref[rs - r + (wr - 1), 2 * p:2 * p + 2].reshape(m2, wr * GRID_W)
            s_loc = _dot_nt(qs_all[a * m2:(a + 1) * m2], k2) + bias
            s_ctx = s_ctx_all[a * m2:(a + 1) * m2]
            m = jnp.maximum(s_loc.max(axis=-1, keepdims=True), s_ctx.max(axis=-1, keepdims=True))
            e_loc = jnp.exp2(s_loc - m)
            e_ctx = jnp.exp2(s_ctx - m)
            l_sum.append(e_loc.sum(axis=-1, keepdims=True) + e_ctx.sum(axis=-1, keepdims=True))
            o_loc.append(jnp.dot(e_loc.astype(BF16), v2, preferred_element_type=F32))
            p_ctx.append(e_ctx.astype(BF16))
        o_ctx_all = jnp.dot(jnp.concatenate(p_ctx, axis=0), vc2, preferred_element_type=F32)
        for a in range(g_rows):
            o = (o_loc[a] + o_ctx_all[a * m2:(a + 1) * m2]) / l_sum[a]
            o_ref[a * GRID_W:(a + 1) * GRID_W, ps] = jnp.where(lo, o[:GRID_W], o[GRID_W:]).astype(BF16)


def _na_bias_tables(rpb):
    wr = NA_WIN_R
    nrel = 2 * NA_WIN_C - 1
    cols = np.arange(GRID_W)
    col_start = np.clip(cols - NA_WIN_C // 2, 0, GRID_W - NA_WIN_C)
    ck = np.arange(GRID_W)[None, :]
    valid = (ck >= col_start[:, None]) & (ck < col_start[:, None] + NA_WIN_C)
    rel = ck - cols[:, None] + (NA_WIN_C - 1)
    onehot = (np.arange(nrel)[:, None, None] == rel[None]) & valid[None]
    onehot = jnp.asarray(onehot.reshape(nrel, GRID_W * GRID_W), F32)
    d = np.arange(wr)[:, None] + np.arange(wr)[None, :]
    t = jnp.einsum("lhdir,rq->lhdiq", rpb[:, :, d] * LOG2E, onehot, precision=HIGHEST)
    t = t.reshape(rpb.shape[0], NA_HEADS, wr, wr, GRID_W, GRID_W)
    t = jnp.where(valid[None, None, None, None], t, NEG_BIG)
    t = t.transpose(0, 2, 1, 4, 3, 5)
    return t.reshape(rpb.shape[0], wr, NA_HEADS, GRID_W, wr * GRID_W)


def _lat_na(z_att, kc, vc, tb, seq_len):
    n = z_att.shape[0]
    nb = n // seq_len
    rows = seq_len // GRID_W
    assert rows >= NA_WIN_R
    wr = NA_WIN_R
    p = kc.shape[1]
    g_rows = NA_ROWS_PER_STEP
    blk = rows // g_rows
    return pl.pallas_call(
        functools.partial(_lat_na_kernel, rows=rows, g_rows=g_rows),
        grid=(nb, blk),
        in_specs=[
            pl.BlockSpec((g_rows * GRID_W, NA_W), lambda b, r: (b * blk + r, 0)),
            pl.BlockSpec((seq_len, NA_W), lambda b, r: (b, 1)),
            pl.BlockSpec((seq_len, NA_W), lambda b, r: (b, 2)),
            pl.BlockSpec((1, p, NA_W), lambda b, r: (b, 0, 0)),
            pl.BlockSpec((1, p, NA_W), lambda b, r: (b, 0, 0)),
            pl.BlockSpec((wr, NA_HEADS, GRID_W, wr * GRID_W), lambda b, r: (0, 0, 0, 0),
                         pipeline_mode=pl.Buffered(1)),
        ],
        out_specs=pl.BlockSpec((g_rows * GRID_W, NA_W), lambda b, r: (b * blk + r, 0)),
        out_shape=jax.ShapeDtypeStruct((n, NA_W), BF16),
        compiler_params=_cparams(("arbitrary", "arbitrary")),
        name="lat_na",
    )(z_att, z_att, z_att, kc, vc, tb)


def _lat_gqa_kernel(q_ref, k_ref, v_ref, kc_ref, vc_ref, o_ref):
    kn2 = k_ref[...].astype(F32)
    v2 = v_ref[...].astype(F32)
    kc2 = kc_ref[0].astype(F32)
    vc2 = vc_ref[0].astype(F32)
    def with_ones(x2, g):
        d = _dup_head(x2, g)
        lo = _lane_lo(d.shape)
        return jnp.where(lo, d, 1.0).astype(BF16), jnp.where(lo, 1.0, d).astype(BF16)

    kd, kcd, v_lo, v_hi = [], [], [], []
    for g in range(GQA_KV_HEADS):
        kd.append(_dup_head(kn2, g).astype(BF16))
        kcd.append(_dup_head(kc2, g).astype(BF16))
        (a0, a1), (b0, b1) = with_ones(v2, g), with_ones(vc2, g)
        v_lo.append([a0, b0])
        v_hi.append([a1, b1])
    for p in range(GQA_HEADS // 2):
        g = (2 * p) // GQA_REP
        ps = slice(p * PAIR_W, (p + 1) * PAIR_W)
        q2 = q_ref[:, ps].astype(F32)
        o_ref[:, ps] = _pair_attn(q2, [kd[g], kcd[g]], v_lo[g], values_hi=v_hi[g]).astype(BF16)


def _rope_tables(seq_len):
    t = jnp.arange(seq_len)
    n_freq = HEAD_DIM // 4
    inv = ROPE_THETA ** (-jnp.arange(n_freq, dtype=F32) / n_freq)
    ang_r = (t // GRID_W).astype(F32)[:, None] * inv
    ang_c = (t % GRID_W).astype(F32)[:, None] * inv
    cr, sr, cc, sc = jnp.cos(ang_r), jnp.sin(ang_r), jnp.cos(ang_c), jnp.sin(ang_c)
    cos = jnp.concatenate([cr, cr, cc, cc] * 2, axis=-1)
    sin = jnp.concatenate([-sr, sr, -sc, sc] * 2, axis=-1)
    return cos, sin


def _lat_gqa(z_att, kc, vc, seq_len):
    n = z_att.shape[0]
    nb = n // seq_len
    tq = min(GQA_QBLOCK, seq_len)
    nqb = seq_len // tq
    p = kc.shape[1]
    qcol = (3 * NA_W) // GQA_QW
    kcol = (3 * NA_W + GQA_QW) // GQA_KW
    return pl.pallas_call(
        _lat_gqa_kernel,
        grid=(nb, nqb),
        in_specs=[
            pl.BlockSpec((tq, GQA_QW), lambda b, i: (b * nqb + i, qcol)),
            pl.BlockSpec((seq_len, GQA_KW), lambda b, i: (b, kcol)),
            pl.BlockSpec((seq_len, GQA_KW), lambda b, i: (b, kcol + 1)),
            pl.BlockSpec((1, p, GQA_KW), lambda b, i: (b, 0, 0)),
            pl.BlockSpec((1, p, GQA_KW), lambda b, i: (b, 0, 0)),
        ],
        out_specs=pl.BlockSpec((tq, GQA_QW), lambda b, i: (b * nqb + i, 0)),
        out_shape=jax.ShapeDtypeStruct((n, GQA_QW), BF16),
        compiler_params=_cparams(("arbitrary", "arbitrary")),
        name="lat_gqa",
    )(z_att, z_att, z_att, kc, vc)


def _layer_norm(x, g, b):
    mu = jnp.mean(x, axis=-1, keepdims=True)
    xc = x - mu
    var = jnp.mean(xc * xc, axis=-1, keepdims=True)
    return xc * lax.rsqrt(var + LN_EPS) * g + b


def _merge_kernel(x_ref, mod_ref, cp_ref, cpp_ref, cpn_ref, *rest, seq_len, n_att):
    att_refs = rest[:n_att]
    (wg_ref, wb_ref, wo_ref, pw_ref, cw_ref, cb_ref, ps_ref, lg_ref, lb_ref, wr_ref,
     x1_ref, xm2_ref, aff_ref, ucv_ref, upl_ref) = rest[n_att:]
    t = x_ref.shape[0]
    halo = POOL_HALO
    i = pl.program_id(0)
    pos0 = (i * t) % seq_len
    has_prev = pos0 > 0
    has_next = pos0 + t < seq_len

    shift1, scale1, gate1 = mod_ref[0, 0:1, :], mod_ref[0, 1:2, :], mod_ref[0, 2:3, :]
    shift2, scale2 = mod_ref[0, 3:4, :], mod_ref[0, 4:5, :]

    bw = BRANCH_W
    ucv_ref[halo:halo + t, :] = cp_ref[:, bw:2 * bw] * cp_ref[:, 2 * bw:3 * bw]
    upl_ref[halo:halo + t, :] = cp_ref[:, 3 * bw:4 * bw]
    ucv_ref[0:halo, :] = jnp.where(has_prev, cpp_ref[:, bw:2 * bw] * cpp_ref[:, 2 * bw:3 * bw], 0.0)
    upl_ref[0:halo, :] = jnp.where(has_prev, cpp_ref[:, 3 * bw:4 * bw], 0.0)
    ucv_ref[halo + t:, :] = jnp.where(has_next, cpn_ref[:, bw:2 * bw] * cpn_ref[:, 2 * bw:3 * bw], 0.0)
    upl_ref[halo + t:, :] = jnp.where(has_next, cpn_ref[:, 3 * bw:4 * bw], 0.0)

    wr = wr_ref[...]
    wh = wr.astype(BF16)
    wl = (wr - wh.astype(F32)).astype(BF16)

    th = min(t, MERGE_SUB)
    blocks = [dict(r0=r0, rows=slice(r0, r0 + th)) for r0 in range(0, t, th)]

    def prologue(b):
        r0, rows = b["r0"], b["rows"]
        x = x_ref[rows, :]
        b["x"] = x
        b["xm"] = (x * (1.0 + scale1) + shift1).astype(BF16)
        conv = (cw_ref[0:1, :] * ucv_ref[halo - 1 + r0:halo - 1 + r0 + th, :]
                + cw_ref[1:2, :] * ucv_ref[halo + r0:halo + r0 + th, :]
                + cw_ref[2:3, :] * ucv_ref[halo + 1 + r0:halo + 1 + r0 + th, :]) + cb_ref[...]
        o_conv = (cp_ref[rows, 0:bw] * conv).astype(BF16)
        pos = pos0 + r0 + lax.broadcasted_iota(I32, (th, 1), 0)
        mixed = []
        for g, win in enumerate(POOL_WINDOWS):
            gs = slice(g * POOL_GW, (g + 1) * POOL_GW)
            acc = None
            for dlt in range(-(win // 2), win // 2):
                term = upl_ref[halo + r0 + dlt:halo + r0 + dlt + th, gs]
                acc = term if acc is None else acc + term
            lo = jnp.maximum(pos - win // 2, 0)
            hi = jnp.minimum(pos + win // 2, seq_len)
            cnt = (hi - lo).astype(F32)
            pooled = acc / cnt - upl_ref[halo + r0:halo + r0 + th, gs]
            mixed.append(jnp.dot(pooled.astype(BF16), pw_ref[g], preferred_element_type=F32))
        o_pool = (jnp.concatenate(mixed, axis=-1) * ps_ref[...]).astype(BF16)
        if n_att == 1:
            b["branches"] = [att_refs[0][rows, 0:bw], att_refs[0][rows, bw:2 * bw], o_conv, o_pool]
        else:
            b["branches"] = [att_refs[0][rows, :], att_refs[1][rows, :], o_conv, o_pool]
        b["merged"] = None

    def dots(b, nb):
        b["zg"] = jnp.dot(b["xm"], wg_ref[:, nb * D_MODEL:(nb + 1) * D_MODEL], preferred_element_type=F32)
        b["proj"] = jnp.dot(b["branches"][nb], wb_ref[nb], preferred_element_type=F32)

    def gate(b):
        term = _sigmoid(b["zg"]) * b["proj"]
        b["merged"] = term if b["merged"] is None else b["merged"] + term

    def out_proj(b):
        b["y"] = jnp.dot(b["merged"].astype(BF16), wo_ref[...], preferred_element_type=F32)

    def tail(b):
        rows = b["rows"]
        x1 = _layer_norm(DEEPNORM_ALPHA * b["x"] + gate1 * b["y"], lg_ref[...], lb_ref[...])
        x1_ref[rows, :] = x1
        xm2 = x1 * (1.0 + scale2) + shift2
        xm2_ref[rows, :] = xm2
        xh = xm2.astype(BF16)
        xl = (xm2 - xh.astype(F32)).astype(BF16)
        logits = (jnp.dot(xh, wh, preferred_element_type=F32) + jnp.dot(xl, wh, preferred_element_type=F32)
                  + jnp.dot(xh, wl, preferred_element_type=F32))
        m = logits.max(axis=-1, keepdims=True)
        e = jnp.exp(logits - m)
        aff_ref[rows, :] = e / e.sum(axis=-1, keepdims=True)

    for b in blocks:
        prologue(b)
        for nb in range(N_BRANCH):
            dots(b, nb)
            gate(b)
        out_proj(b)
        tail(b)


def _merge(x, mod, z_cp, atts, w_gates, w_branch, w_out, pool_w, conv_w, conv_b, pool_scale, ln_g, ln_b,
           w_router, seq_len):
    n = x.shape[0]
    t = min(MERGE_TILE, seq_len)
    per = n // mod.shape[0] // t
    hb = t // POOL_HALO
    nhb = n // POOL_HALO
    row = lambda a: a.reshape(1, -1)
    const2 = lambda i: (0, 0)
    const3 = lambda i: (0, 0, 0)
    once = lambda shp, imap: pl.BlockSpec(shp, imap, pipeline_mode=pl.Buffered(1))
    att_specs = [pl.BlockSpec((t, a.shape[1]), lambda i: (i, 0)) for a in atts]
    return pl.pallas_call(
        functools.partial(_merge_kernel, seq_len=seq_len, n_att=len(atts)),
        grid=(n // t,),
        in_specs=[
            pl.BlockSpec((t, D_MODEL), lambda i: (i, 0)),
            pl.BlockSpec((1, N_MOD, D_MODEL), lambda i: (i // per, 0, 0)),
            pl.BlockSpec((t, CP_W), lambda i: (i, 0)),
            pl.BlockSpec((POOL_HALO, CP_W), lambda i: (jnp.maximum(i * hb - 1, 0), 0)),
            pl.BlockSpec((POOL_HALO, CP_W), lambda i: (jnp.minimum((i + 1) * hb, nhb - 1), 0)),
            *att_specs,
            once((D_MODEL, GATE_W), const2),
            once((N_BRANCH, BRANCH_W, D_MODEL), const3),
            once((D_MODEL, D_MODEL), const2),
            pl.BlockSpec((len(POOL_WINDOWS), POOL_GW, POOL_GW), const3),
            pl.BlockSpec((CONV_K, BRANCH_W), const2),
            pl.BlockSpec((1, BRANCH_W), const2),
            pl.BlockSpec((1, BRANCH_W), const2),
            pl.BlockSpec((1, D_MODEL), const2),
            pl.BlockSpec((1, D_MODEL), const2),
            pl.BlockSpec((D_MODEL, N_EXPERTS), const2),
        ],
        out_specs=[
            pl.BlockSpec((t, D_MODEL), lambda i: (i, 0)),
            pl.BlockSpec((t, D_MODEL), lambda i: (i, 0)),
            pl.BlockSpec((t, N_EXPERTS), lambda i: (i, 0)),
        ],
        out_shape=[jax.ShapeDtypeStruct((n, D_MODEL), F32), jax.ShapeDtypeStruct((n, D_MODEL), F32),
                   jax.ShapeDtypeStruct((n, N_EXPERTS), F32)],
        scratch_shapes=[pltpu.VMEM((t + 2 * POOL_HALO, BRANCH_W), F32),
                        pltpu.VMEM((t + 2 * POOL_HALO, BRANCH_W), F32)],
        compiler_params=_cparams(("arbitrary",)),
        name="merge",
    )(x, mod, z_cp, z_cp, z_cp, *atts, w_gates, w_branch, w_out, pool_w, conv_w, row(conv_b), row(pool_scale),
      row(ln_g), row(ln_b), w_router)


def _topk_kernel(aff_ref, idx_ref, dst_ref, gp_ref, cnt_ref, thr_ref, rank_ref, *, cap, plane_stride):
    ne, nc, _ = aff_ref.shape
    capf = float(cap)
    bits_all = pltpu.bitcast(aff_ref[...], I32)

    def search(i, cur):
        cand = cur | (jnp.int32(1) << (30 - i))
        cnt = jnp.sum((bits_all >= cand).astype(F32), axis=(1, 2), keepdims=True)
        return jnp.where(cnt >= capf, cand, cur)

    thr = lax.fori_loop(0, 31, search, jnp.zeros((ne, 1, 1), I32))
    thr_ref[...] = jnp.broadcast_to(thr, thr_ref.shape)

    jj = lax.broadcasted_iota(I32, (CHUNK, CHUNK), 0)
    kk = lax.broadcasted_iota(I32, (CHUNK, CHUNK), 1)
    ut_incl = (jj <= kk).astype(BF16)
    lt_incl = (kk <= jj).astype(BF16)
    cc = lax.broadcasted_iota(I32, (nc, nc), 0)
    dd = lax.broadcasted_iota(I32, (nc, nc), 1)
    cl_excl = (dd < cc).astype(BF16)
    cl_incl = (dd <= cc).astype(BF16)
    eye = (jj == kk).astype(BF16)
    wide = 4 if cap % (4 * CHUNK) == 0 else 1
    lane_w = lax.broadcasted_iota(I32, (1, wide * CHUNK), 1)
    sub_cw = lax.broadcasted_iota(I32, (nc, wide * CHUNK), 0).astype(F32)
    sub_jw = lax.broadcasted_iota(I32, (CHUNK, wide * CHUNK), 0).astype(F32)
    rank_ref[...] = jnp.zeros(rank_ref.shape, F32)
    gp_ref[...] = jnp.full(gp_ref.shape, -1.0, F32)

    def per_expert(e, carry):
        a = aff_ref[e]
        b = pltpu.bitcast(a, I32)
        t = thr_ref[e][0:1, :]
        gt = b > t
        eq = b == t
        need = capf - jnp.sum(gt.astype(F32), keepdims=True)
        eqf = eq.astype(F32)
        incl_eq = jnp.dot(eqf.astype(BF16), ut_incl, preferred_element_type=F32)
        tot_eq = jnp.broadcast_to(incl_eq[:, CHUNK - 1:CHUNK], (nc, CHUNK))
        cum_eq = jnp.dot(cl_excl, tot_eq.astype(BF16), preferred_element_type=F32)
        rank = cum_eq + incl_eq - eqf
        sel = jnp.logical_or(gt, jnp.logical_and(eq, rank < need))
        self_ = sel.astype(F32)

        plane = rank_ref[...]
        rank_ref[...] = plane + self_
        for k in range(ne):
            gp_ref[k] = jnp.where(jnp.logical_and(sel, plane == float(k)), a, gp_ref[k])
        plane_tb = _dot_nt(eye, plane.astype(BF16)).astype(BF16)

        selb = self_.astype(BF16)
        incl = jnp.dot(selb, ut_incl, preferred_element_type=F32)
        tot = jnp.broadcast_to(incl[:, CHUNK - 1:CHUNK], (nc, CHUNK))
        cum_incl = jnp.dot(cl_incl, tot.astype(BF16), preferred_element_type=F32)
        incl_tb = _dot_nt(lt_incl, selb).astype(BF16)

        def per_block(sb, c2):
            s_row = (sb * (wide * CHUNK) + lane_w).astype(F32)
            m = cum_w <= s_row
            c_of_s = jnp.sum(m.astype(F32), axis=0, keepdims=True)
            excl_s = jnp.max(jnp.where(m, cum_w, 0.0), axis=0, keepdims=True)
            onehot_t = (sub_cw == c_of_s).astype(BF16)
            rows_t = jnp.dot(incl_tb, onehot_t, preferred_element_type=F32)
            s_local = s_row - excl_s
            t_local = jnp.sum((rows_t <= s_local).astype(F32), axis=0, keepdims=True)
            tok = (c_of_s * float(CHUNK) + t_local).astype(I32)
            plane_rows = jnp.dot(plane_tb, onehot_t, preferred_element_type=F32)
            plane_s = jnp.sum(jnp.where(sub_jw == t_local, plane_rows, 0.0), axis=0, keepdims=True)
            dst = plane_s.astype(I32) * plane_stride + tok
            for u in range(wide):
                idx_ref[e, pl.ds(sb * wide + u, 1), :] = tok[:, u * CHUNK:(u + 1) * CHUNK]
                dst_ref[e, pl.ds(sb * wide + u, 1), :] = dst[:, u * CHUNK:(u + 1) * CHUNK]
            return c2

        cum_w = jnp.concatenate([cum_incl] * wide, axis=1)
        lax.fori_loop(0, cap // (wide * CHUNK), per_block, 0)
        return carry

    lax.fori_loop(0, ne, per_expert, 0)
    cnt_ref[...] = jnp.broadcast_to(jnp.max(rank_ref[...], axis=1, keepdims=True), cnt_ref.shape).astype(I32)


def _topk(aff3, cap, plane_stride):
    ne, nc, _ = aff3.shape
    full3 = lambda shp: pl.BlockSpec(shp, lambda i: (0, 0, 0))
    return pl.pallas_call(
        functools.partial(_topk_kernel, cap=cap, plane_stride=plane_stride),
        grid=(1,),
        in_specs=[full3((ne, nc, CHUNK))],
        out_specs=[full3((ne, cap // CHUNK, CHUNK)), full3((ne, cap // CHUNK, CHUNK)), full3((ne, nc, CHUNK)),
                   pl.BlockSpec((nc, CHUNK), lambda i: (0, 0))],
        out_shape=[jax.ShapeDtypeStruct((ne, cap // CHUNK, CHUNK), I32),
                   jax.ShapeDtypeStruct((ne, cap // CHUNK, CHUNK), I32),
                   jax.ShapeDtypeStruct((ne, nc, CHUNK), F32),
                   jax.ShapeDtypeStruct((nc, CHUNK), I32)],
        scratch_shapes=[pltpu.VMEM((ne, SUBLANES, CHUNK), I32), pltpu.VMEM((nc, CHUNK), F32)],
        compiler_params=_cparams(("arbitrary",)),
        name="topk",
    )(aff3)


def _ffn_kernel(idx_ref, idxn_ref, dst_ref, x_hbm, wg_ref, wu_ref, wd_ref, y_in_hbm, y_hbm, xs_ref, ys_ref, gsem_ref,
                ssem_ref, *, steps):
    tm = xs_ref.shape[1]
    s = pl.program_id(0) * pl.num_programs(1) + pl.program_id(1)

    def gather_rows(idx_smem, buf_slot):
        for j in range(tm):
            pltpu.make_async_copy(x_hbm.at[pl.ds(idx_smem[0, 0, j], 1), :], xs_ref.at[buf_slot, pl.ds(j, 1), :],
                                  gsem_ref.at[buf_slot]).start()

    def gathered(buf_slot):
        return pltpu.make_async_copy(x_hbm.at[pl.ds(0, tm), :], xs_ref.at[buf_slot], gsem_ref.at[buf_slot])

    def scattered(buf_slot):
        return pltpu.make_async_copy(ys_ref.at[buf_slot], y_hbm.at[pl.ds(0, tm), :], ssem_ref.at[buf_slot])

    @pl.when(s == 0)
    def _():
        gather_rows(idx_ref, 0)

    def step(slot):
        @pl.when(s + 1 < steps)
        def _():
            gather_rows(idxn_ref, 1 - slot)

        gathered(slot).wait()
        xs = xs_ref[slot].astype(BF16)
        hg = jnp.dot(xs, wg_ref[0], preferred_element_type=F32)
        hu = jnp.dot(xs, wu_ref[0], preferred_element_type=F32)
        hdn = (hg * _sigmoid(hg) * hu).astype(BF16)
        ye = jnp.dot(hdn, wd_ref[0], preferred_element_type=F32)

        @pl.when(s >= 2)
        def _():
            scattered(slot).wait()

        ys_ref[slot] = ye
        for j in range(tm):
            pltpu.make_async_copy(ys_ref.at[slot, pl.ds(j, 1), :], y_hbm.at[pl.ds(dst_ref[0, 0, j], 1), :],
                                  ssem_ref.at[slot]).start(priority=j % 2)

        @pl.when(s == steps - 1)
        def _():
            scattered(slot).wait()
            if steps > 1:
                scattered(1 - slot).wait()

    for parity in range(2):
        pl.when(s % 2 == parity)(functools.partial(step, parity))


def _ffn(idx_tiles, dst_tiles, xm2, w_gate, w_up, w_down, cap, ybuf):
    tm = idx_tiles.shape[2]
    nt = cap // tm
    steps = N_EXPERTS * nt
    wspec = lambda: pl.BlockSpec((1, D_MODEL, EXPERT_FF), lambda e, i: (e, 0, 0))
    return pl.pallas_call(
        functools.partial(_ffn_kernel, steps=steps),
        grid=(N_EXPERTS, nt),
        in_specs=[
            pl.BlockSpec((1, 1, tm), lambda e, i: (e * nt + i, 0, 0), memory_space=pltpu.SMEM),
            pl.BlockSpec((1, 1, tm), lambda e, i: (jnp.minimum(e * nt + i + 1, steps - 1), 0, 0),
                         memory_space=pltpu.SMEM),
            pl.BlockSpec((1, 1, tm), lambda e, i: (e * nt + i, 0, 0), memory_space=pltpu.SMEM),
            pl.BlockSpec(memory_space=pl.ANY),
            wspec(), wspec(),
            pl.BlockSpec((1, EXPERT_FF, D_MODEL), lambda e, i: (e, 0, 0)),
            pl.BlockSpec(memory_space=pl.ANY),
        ],
        out_specs=pl.BlockSpec(memory_space=pl.ANY),
        out_shape=jax.ShapeDtypeStruct(ybuf.shape, F32),
        input_output_aliases={7: 0},
        scratch_shapes=[pltpu.VMEM((2, tm, D_MODEL), F32), pltpu.VMEM((2, tm, D_MODEL), F32),
                        pltpu.SemaphoreType.DMA((2,)), pltpu.SemaphoreType.DMA((2,))],
        compiler_params=_cparams(("arbitrary", "arbitrary")),
        name="ffn",
    )(idx_tiles, idx_tiles, dst_tiles, xm2, w_gate, w_up, w_down, ybuf)


def _combine_kernel(kmax_ref, y_hbm, x1_ref, mod_ref, gp_ref, lg_ref, lb_ref, o_ref, planes_ref, acc_ref, sem_ref,
                    *, n_planes):
    t = x1_ref.shape[0]
    i = pl.program_id(0)
    ntile = pl.num_programs(0)
    slot = i % 2

    def for_planes(tile, buf_slot, fn):
        for k in range(n_planes):
            @pl.when(k < kmax_ref[tile])
            def _(k=k):
                fn(pltpu.make_async_copy(y_hbm.at[k, pl.ds(tile * t, t), :], planes_ref.at[buf_slot, k],
                                         sem_ref.at[buf_slot]))

    @pl.when(i == 0)
    def _():
        for_planes(0, 0, lambda cp: cp.start())

    @pl.when(i + 1 < ntile)
    def _():
        for_planes(i + 1, 1 - slot, lambda cp: cp.start())

    for_planes(i, slot, lambda cp: cp.wait())

    acc_ref[...] = jnp.zeros(acc_ref.shape, F32)
    for k in range(n_planes):
        @pl.when(k < kmax_ref[i])
        def _(k=k):
            w = gp_ref[:, k:k + 1]
            acc_ref[...] += jnp.where(w >= 0.0, w * planes_ref[slot, k], 0.0)

    gate2 = mod_ref[0, 5:6, :]
    o_ref[...] = _layer_norm(DEEPNORM_ALPHA * x1_ref[...] + gate2 * acc_ref[...], lg_ref[...], lb_ref[...])


def _combine(kmax, y_planes, x1, mod, gp_tok, ln_g, ln_b, seq_len):
    n = x1.shape[0]
    n_planes = y_planes.shape[0]
    t = min(TOK_TILE, seq_len)
    per = n // mod.shape[0] // t
    row = lambda a: a.reshape(1, -1)
    grid_spec = pltpu.PrefetchScalarGridSpec(
        num_scalar_prefetch=1,
        grid=(n // t,),
        in_specs=[
            pl.BlockSpec(memory_space=pl.ANY),
            pl.BlockSpec((t, D_MODEL), lambda i, *_: (i, 0)),
            pl.BlockSpec((1, N_MOD, D_MODEL), lambda i, *_: (i // per, 0, 0)),
            pl.BlockSpec((t, n_planes), lambda i, *_: (i, 0)),
            pl.BlockSpec((1, D_MODEL), lambda i, *_: (0, 0)),
            pl.BlockSpec((1, D_MODEL), lambda i, *_: (0, 0)),
        ],
        out_specs=pl.BlockSpec((t, D_MODEL), lambda i, *_: (i, 0)),
        scratch_shapes=[pltpu.VMEM((2, n_planes, t, D_MODEL), F32), pltpu.VMEM((t, D_MODEL), F32),
                        pltpu.SemaphoreType.DMA((2,))],
    )
    return pl.pallas_call(
        functools.partial(_combine_kernel, n_planes=n_planes),
        grid_spec=grid_spec,
        out_shape=jax.ShapeDtypeStruct((n, D_MODEL), F32),
        compiler_params=_cparams(("arbitrary",)),
        name="combine",
    )(kmax, y_planes, x1, mod, gp_tok, row(ln_g), row(ln_b))


def _moe(x1, xm2, aff, mod, w_gate, w_up, w_down, ln_g, ln_b, seq_len, ybuf):
    n = x1.shape[0]
    cap = EC_CAPACITY * n // N_EXPERTS
    nc = n // CHUNK
    t = min(TOK_TILE, seq_len)
    stride = ybuf.shape[0] // N_EXPERTS
    aff3 = aff.T.reshape(N_EXPERTS, nc, CHUNK)
    idx3, dst3, gp3, cnt = _topk(aff3, cap, stride)
    gp_tok = gp3.reshape(N_EXPERTS, n).T
    kmax = jnp.max(cnt[:, 0].reshape(n // t, t // CHUNK), axis=1)
    tm = min(FFN_TILE, cap)
    idx_tiles = idx3.reshape(N_EXPERTS * cap // tm, 1, tm)
    dst_tiles = dst3.reshape(N_EXPERTS * cap // tm, 1, tm)
    ybuf = _ffn(idx_tiles, dst_tiles, xm2, w_gate, w_up, w_down, cap, ybuf)
    y = _combine(kmax, ybuf.reshape(N_EXPERTS, stride, D_MODEL), x1, mod, gp_tok, ln_g, ln_b, seq_len)
    return y, ybuf


def kernel(x_prompt, x_sample, cache_na_k, cache_na_v, cache_gqa_k, cache_gqa_v, c, c_ctx, w_ada, b_ada, w_in,
           na_rpb, qn_g, kn_g, conv_w, conv_b, pool_w, pool_scale, w_branch, w_out, ln1_g, ln1_b, ln2_g, ln2_b,
           w_router, w_gate, w_up, w_down):
    depth = w_in.shape[0]
    bp, lp, _ = x_prompt.shape
    bs, ls, _ = x_sample.shape
    past = cache_na_k.shape[2]

    mods = _adaln(jnp.concatenate([c_ctx[None, :], c], axis=0), w_ada, b_ada)
    mods = mods.reshape(depth, 1 + bs, N_MOD, D_MODEL)
    cos, sin = _rope_tables(ls)
    tbs = _na_bias_tables(na_rpb)

    yp = x_prompt.reshape(bp * lp, D_MODEL)
    ys = x_sample.reshape(bs * ls, D_MODEL)
    ybuf_p = jnp.zeros((N_EXPERTS * bp * lp, D_MODEL), F32)
    ybuf_s = jnp.zeros((N_EXPERTS * bs * ls, D_MODEL), F32)
    na_k_l, na_v_l, g_k_l, g_v_l = [], [], [], []
    for l in range(depth):
        mod_c = mods[l, 0:1]
        mod_s = mods[l, 1:]
        w_a = w_in[l, :, :ATT_W + CP_W].astype(BF16)
        w_gates = w_in[l, :, ATT_W + CP_W:].astype(BF16)
        w_br = w_branch[l].astype(BF16)
        w_o = w_out[l].astype(BF16)
        p_w = pool_w[l].astype(BF16)
        wg, wu, wd = w_gate[l].astype(BF16), w_up[l].astype(BF16), w_down[l].astype(BF16)
        merge_w = (w_gates, w_br, w_o, p_w, conv_w[l], conv_b[l], pool_scale[l], ln1_g[l], ln1_b[l], w_router[l])

        qn2 = jnp.tile(qn_g[l], 2).reshape(1, PAIR_W)
        kn2 = jnp.tile(kn_g[l], 2).reshape(1, PAIR_W)
        cache = lambda a, w: a[:, l].reshape(bs, past, w).astype(BF16)

        z_att, z_cp, na_k, na_v, g_k, g_v = _inproj(yp, mod_c, w_a, bp * lp, kn2)
        o_att = _ctx_attn(z_att, qn2, kn2, lp)
        na_k_l.append(na_k.reshape(bp, lp, NA_HEADS, HEAD_DIM))
        na_v_l.append(na_v.reshape(bp, lp, NA_HEADS, HEAD_DIM))
        g_k_l.append(g_k.reshape(bp, lp, GQA_KV_HEADS, HEAD_DIM))
        g_v_l.append(g_v.reshape(bp, lp, GQA_KV_HEADS, HEAD_DIM))
        x1, xm2, aff = _merge(yp, mod_c, z_cp, [o_att], *merge_w, lp)
        yp, ybuf_p = _moe(x1, xm2, aff, mod_c, wg, wu, wd, ln2_g[l], ln2_b[l], lp, ybuf_p)

        z_att, z_cp = _inproj(ys, mod_s, w_a, ls, kn2, qn2, cos, sin)
        o_na = _lat_na(z_att, cache(cache_na_k, NA_W), cache(cache_na_v, NA_W), tbs[l], ls)
        o_gqa = _lat_gqa(z_att, cache(cache_gqa_k, GQA_KW), cache(cache_gqa_v, GQA_KW), ls)
        x1, xm2, aff = _merge(ys, mod_s, z_cp, [o_na, o_gqa], *merge_w, ls)
        ys, ybuf_s = _moe(x1, xm2, aff, mod_s, wg, wu, wd, ln2_g[l], ln2_b[l], ls, ybuf_s)

    return (yp.reshape(bp, lp, D_MODEL), ys.reshape(bs, ls, D_MODEL),
            jnp.stack(na_k_l, axis=1), jnp.stack(na_v_l, axis=1),
            jnp.stack(g_k_l, axis=1), jnp.stack(g_v_l, axis=1))
```

```python
import functools

import numpy as np
import jax
import jax.numpy as jnp
from jax import lax
from jax.experimental import pallas as pl
from jax.experimental.pallas import tpu as pltpu

F32 = jnp.float32
BF16 = jnp.bfloat16
I32 = jnp.int32
U32 = jnp.uint32
HIGHEST = lax.Precision.HIGHEST

D_MODEL = 1024
NOMINAL_DEPTH = 4
GRID_W = 64
HEAD_DIM = 64
NA_HEADS = 8
NA_WIN_R = 8
NA_WIN_C = 16
GQA_HEADS = 8
GQA_KV_HEADS = 2
GQA_REP = GQA_HEADS // GQA_KV_HEADS
BRANCH_W = 512
N_BRANCH = 4
CONV_K = 3
POOL_WINDOWS = (2, 4, 8, 16)
POOL_GW = BRANCH_W // len(POOL_WINDOWS)
POOL_HALO = max(POOL_WINDOWS) // 2
N_EXPERTS = 16
EXPERT_FF = 1024
EC_CAPACITY = 2
ROPE_THETA = 10000.0
LN_EPS = 1e-6
RMS_EPS = 1e-6
N_MOD = 6
DEEPNORM_ALPHA = (2 * NOMINAL_DEPTH) ** 0.25
NA_W = NA_HEADS * HEAD_DIM
GQA_QW = GQA_HEADS * HEAD_DIM
GQA_KW = GQA_KV_HEADS * HEAD_DIM
ATT_W = 3 * NA_W + GQA_QW + 2 * GQA_KW
CP_W = 4 * BRANCH_W
GATE_W = N_BRANCH * D_MODEL
LOG2E = 1.4426950408889634
Q_SCALE = HEAD_DIM ** -0.5 * LOG2E
NEG_BIG = -1e30
PAIR_W = 2 * HEAD_DIM
PACK_W = D_MODEL // 2

LANES = 128
SUBLANES = 8
VMEM_LIMIT = 56 * 1024 * 1024

TOK_TILE_IN = 512
MERGE_TILE = 512
MERGE_SUB = 256
TOK_TILE = 256
CHUNK = LANES
FFN_TILE = 256
GQA_QBLOCK = 512
NA_ROWS_PER_STEP = 4


def _cparams(sem):
    return pltpu.CompilerParams(dimension_semantics=sem, vmem_limit_bytes=VMEM_LIMIT)


def _pack_bf16_pair(a, b):
    lo = pltpu.bitcast(a.astype(BF16).astype(F32), U32)
    hi = pltpu.bitcast(b.astype(BF16).astype(F32), U32)
    return (lo >> 16) | (hi & jnp.uint32(0xFFFF0000))


def _unpack_bf16_pair(w):
    return pltpu.bitcast(w << 16, F32), pltpu.bitcast(w & jnp.uint32(0xFFFF0000), F32)


def _sigmoid(x):
    return 0.5 * jnp.tanh(0.5 * x) + 0.5


def _adaln_kernel(cv_ref, w_ref, b_ref, o_ref):
    cv = cv_ref[...]
    s = cv * jax.nn.sigmoid(cv)
    o_ref[0] = jnp.dot(s, w_ref[0], precision=HIGHEST, preferred_element_type=F32) + b_ref[0]


def _adaln(cvecs, w_ada, b_ada):
    depth = w_ada.shape[0]
    r = cvecs.shape[0]
    tn = 1536
    nw = N_MOD * D_MODEL
    return pl.pallas_call(
        _adaln_kernel,
        grid=(depth, nw // tn),
        in_specs=[
            pl.BlockSpec((r, D_MODEL), lambda l, j: (0, 0)),
            pl.BlockSpec((1, D_MODEL, tn), lambda l, j: (l, 0, j)),
            pl.BlockSpec((1, 1, tn), lambda l, j: (l, 0, j)),
        ],
        out_specs=pl.BlockSpec((1, r, tn), lambda l, j: (l, 0, j)),
        out_shape=jax.ShapeDtypeStruct((depth, r, nw), F32),
        compiler_params=_cparams(("arbitrary", "arbitrary")),
        name="adaln",
    )(cvecs, w_ada, b_ada.reshape(depth, 1, nw))


def _lane_lo(shape):
    return (lax.broadcasted_iota(I32, shape, len(shape) - 1) % PAIR_W) < HEAD_DIM


def _pair_rms(x2, g2):
    lo = _lane_lo(x2.shape)
    sq = x2 * x2
    ss_lo = jnp.sum(jnp.where(lo, sq, 0.0), axis=-1, keepdims=True)
    ss_hi = jnp.sum(jnp.where(lo, 0.0, sq), axis=-1, keepdims=True)
    inv = lax.rsqrt(jnp.where(lo, ss_lo, ss_hi) * (1.0 / HEAD_DIM) + RMS_EPS)
    return x2 * inv * g2


def _pair_rope(x2, cos2, sin2):
    q = HEAD_DIM // 4
    first = (lax.broadcasted_iota(I32, x2.shape, 1) % (2 * q)) < q
    swapped = jnp.where(first, pltpu.roll(x2, PAIR_W - q, 1), pltpu.roll(x2, q, 1))
    return x2 * cos2 + swapped * sin2


def _inproj_kernel(x_ref, mod_ref, w_ref, *rest, with_state):
    if with_state:
        kn_ref, att_ref, cp_ref, nak_ref, nav_ref, gk_ref, gv_ref = rest
    else:
        qn_ref, kn_ref, cos_ref, sin_ref, att_ref, cp_ref = rest
    shift = mod_ref[0, 0:1, :]
    scale = mod_ref[0, 1:2, :]
    xm = (x_ref[...] * (1.0 + scale) + shift).astype(BF16)
    z = jnp.dot(xm, w_ref[...], preferred_element_type=F32)
    cp_ref[...] = z[:, ATT_W:]
    qoff = 3 * NA_W
    koff = qoff + GQA_QW
    if with_state:
        att_ref[...] = z[:, :ATT_W].astype(BF16)
        nak_ref[...] = z[:, NA_W:2 * NA_W]
        nav_ref[...] = z[:, 2 * NA_W:3 * NA_W]
        gk_ref[...] = _pair_rms(z[:, koff:koff + GQA_KW], kn_ref[...])
        gv_ref[...] = z[:, koff + GQA_KW:koff + 2 * GQA_KW]
    else:
        att_ref[:, :qoff] = z[:, :qoff].astype(BF16)
        for p in range(GQA_HEADS // 2):
            ps = slice(qoff + p * PAIR_W, qoff + (p + 1) * PAIR_W)
            q2 = _pair_rope(_pair_rms(z[:, ps], qn_ref[...]), cos_ref[...], sin_ref[...]) * Q_SCALE
            att_ref[:, ps] = q2.astype(BF16)
        k2 = _pair_rope(_pair_rms(z[:, koff:koff + GQA_KW], kn_ref[...]), cos_ref[...], sin_ref[...])
        att_ref[:, koff:koff + GQA_KW] = k2.astype(BF16)
        att_ref[:, koff + GQA_KW:] = z[:, koff + GQA_KW:ATT_W].astype(BF16)


def _inproj(x, mod, w_a, seq_len, kn2, qn2=None, cos=None, sin=None):
    n = x.shape[0]
    t = min(TOK_TILE_IN, seq_len)
    per = n // mod.shape[0] // t
    with_state = qn2 is None
    tile = lambda w: pl.BlockSpec((t, w), lambda i: (i, 0))
    vec = pl.BlockSpec((1, PAIR_W), lambda i: (0, 0))
    in_specs = [
        pl.BlockSpec((t, D_MODEL), lambda i: (i, 0)),
        pl.BlockSpec((1, N_MOD, D_MODEL), lambda i: (i // per, 0, 0)),
        pl.BlockSpec((D_MODEL, ATT_W + CP_W), lambda i: (0, 0)),
    ]
    out_specs = [tile(ATT_W), tile(CP_W)]
    out_shape = [jax.ShapeDtypeStruct((n, ATT_W), BF16), jax.ShapeDtypeStruct((n, CP_W), F32)]
    if with_state:
        in_specs.append(vec)
        args = [x, mod, w_a, kn2]
        for w in (NA_W, NA_W, GQA_KW, GQA_KW):
            out_specs.append(tile(w))
            out_shape.append(jax.ShapeDtypeStruct((n, w), F32))
    else:
        nblk = seq_len // t
        pos = pl.BlockSpec((t, PAIR_W), lambda i: (i % nblk, 0))
        in_specs += [vec, vec, pos, pos]
        args = [x, mod, w_a, qn2, kn2, cos, sin]
    return pl.pallas_call(
        functools.partial(_inproj_kernel, with_state=with_state),
        grid=(n // t,),
        in_specs=in_specs,
        out_specs=out_specs,
        out_shape=out_shape,
        compiler_params=_cparams(("arbitrary",)),
        name="inproj",
    )(*args)


def _dup_head(x2, g):
    lo = _lane_lo(x2.shape)
    other = pltpu.roll(x2, HEAD_DIM, 1)
    return jnp.where(lo, x2, other) if g == 0 else jnp.where(lo, other, x2)


def _dot_nt(a, b):
    return lax.dot_general(a, b, (((1,), (1,)), ((), ())), preferred_element_type=F32)


def _softmax_pv(scores, values):
    m = scores[0].max(axis=-1, keepdims=True)
    for s in scores[1:]:
        m = jnp.maximum(m, s.max(axis=-1, keepdims=True))
    l = None
    o = None
    for s, v in zip(scores, values):
        p = jnp.exp2(s - m)
        ls = p.sum(axis=-1, keepdims=True)
        os_ = jnp.dot(p.astype(BF16), v, preferred_element_type=F32)
        l = ls if l is None else l + ls
        o = os_ if o is None else o + os_
    return o / l


def _pair_attn(q2, keys, values, biases=None, stack=False, values_hi=None):
    lo = _lane_lo(q2.shape)
    m_rows = q2.shape[0]
    q_halves = [jnp.where(lo, q2, 0.0).astype(BF16), jnp.where(lo, 0.0, q2).astype(BF16)]

    def run(q, row0):
        scores = []
        for bi, k2 in enumerate(keys):
            s = _dot_nt(q, k2)
            if biases is not None and biases[bi] is not None:
                s = s + biases[bi][row0:row0 + q.shape[0]]
            scores.append(s)
        return _softmax_pv(scores, values)

    if stack:
        o = run(jnp.concatenate(q_halves, axis=0), 0)
        return jnp.where(lo, o[:m_rows], o[m_rows:])
    if values_hi is None:
        return jnp.where(lo, run(q_halves[0], 0), run(q_halves[1], m_rows))

    def run_mxu_sum(q, vals):
        scores = [_dot_nt(q, k2) for k2 in keys]
        m = scores[0].max(axis=-1, keepdims=True)
        for s in scores[1:]:
            m = jnp.maximum(m, s.max(axis=-1, keepdims=True))
        o = None
        for s, v in zip(scores, vals):
            os_ = jnp.dot(jnp.exp2(s - m).astype(BF16), v, preferred_element_type=F32)
            o = os_ if o is None else o + os_
        return o / pltpu.roll(o, HEAD_DIM, 1)

    return jnp.where(lo, run_mxu_sum(q_halves[0], values), run_mxu_sum(q_halves[1], values_hi))


def _ctx_attn_kernel(z_ref, qn_ref, kn_ref, o_ref):
    for p in range(NA_HEADS // 2):
        ps = slice(p * PAIR_W, (p + 1) * PAIR_W)
        q2 = z_ref[:, ps].astype(F32) * Q_SCALE
        k2 = z_ref[:, NA_W + p * PAIR_W:NA_W + (p + 1) * PAIR_W]
        v2 = z_ref[:, 2 * NA_W + p * PAIR_W:2 * NA_W + (p + 1) * PAIR_W]
        o_ref[:, ps] = _pair_attn(q2, [k2], [v2], stack=True).astype(BF16)
    qoff = 3 * NA_W
    koff = qoff + GQA_QW
    voff = koff + GQA_KW
    kn2 = _pair_rms(z_ref[:, koff:koff + GQA_KW].astype(F32), kn_ref[...])
    v2 = z_ref[:, voff:voff + GQA_KW].astype(F32)
    kd = [_dup_head(kn2, g).astype(BF16) for g in range(GQA_KV_HEADS)]
    vd = [_dup_head(v2, g).astype(BF16) for g in range(GQA_KV_HEADS)]
    for p in range(GQA_HEADS // 2):
        g = (2 * p) // GQA_REP
        q2 = _pair_rms(z_ref[:, qoff + p * PAIR_W:qoff + (p + 1) * PAIR_W].astype(F32), qn_ref[...]) * Q_SCALE
        o_ref[:, NA_W + p * PAIR_W:NA_W + (p + 1) * PAIR_W] = _pair_attn(q2, [kd[g]], [vd[g]],
                                                                          stack=True).astype(BF16)


def _ctx_attn(z_att, qn2, kn2, seq_len):
    n = z_att.shape[0]
    return pl.pallas_call(
        _ctx_attn_kernel,
        grid=(n // seq_len,),
        in_specs=[
            pl.BlockSpec((seq_len, ATT_W), lambda b: (b, 0)),
            pl.BlockSpec((1, PAIR_W), lambda b: (0, 0)),
            pl.BlockSpec((1, PAIR_W), lambda b: (0, 0)),
        ],
        out_specs=pl.BlockSpec((seq_len, NA_W + GQA_QW), lambda b: (b, 0)),
        out_shape=jax.ShapeDtypeStruct((n, NA_W + GQA_QW), BF16),
        compiler_params=_cparams(("arbitrary",)),
        name="ctx_attn",
    )(z_att, qn2, kn2)


def _na_row_start(r, rows):
    return jnp.clip(r - NA_WIN_R // 2, 0, rows - NA_WIN_R)


def _lat_na_kernel(q_ref, k_ref, v_ref, kc_ref, vc_ref, tb_ref, o_ref, *, rows, g_rows):
    r0 = pl.program_id(1) * g_rows
    wr = NA_WIN_R
    m2 = 2 * GRID_W
    lo = _lane_lo((GRID_W, PAIR_W))
    for p in range(NA_HEADS // 2):
        ps = slice(p * PAIR_W, (p + 1) * PAIR_W)
        kc2 = kc_ref[0, :, ps]
        vc2 = vc_ref[0, :, ps]
        qs = []
        for a in range(g_rows):
            q2 = q_ref[a * GRID_W:(a + 1) * GRID_W, ps].astype(F32) * Q_SCALE
            qs += [jnp.where(lo, q2, 0.0).astype(BF16), jnp.where(lo, 0.0, q2).astype(BF16)]
        qs_all = jnp.concatenate(qs, axis=0)
        s_ctx_all = _dot_nt(qs_all, kc2)
        o_loc, p_ctx, l_sum = [], [], []
        for a in range(g_rows):
            r = r0 + a
            rs = _na_row_start(r, rows)
            start = pl.multiple_of(rs * GRID_W, GRID_W)
            k2 = k_ref[pl.ds(start, wr * GRID_W), ps]
            v2 = v_ref[pl.ds(start, wr * GRID_W), ps]
            bias = tb_ref[rs - r + (wr - 1), 2 * p:2 * p + 2].reshape(m2, wr * GRID_W)
            s_loc = _dot_nt(qs_all[a * m2:(a + 1) * m2], k2) + bias
            s_ctx = s_ctx_all[a * m2:(a + 1) * m2]
            m = jnp.maximum(s_loc.max(axis=-1, keepdims=True), s_ctx.max(axis=-1, keepdims=True))
            e_loc = jnp.exp2(s_loc - m)
            e_ctx = jnp.exp2(s_ctx - m)
            l_sum.append(e_loc.sum(axis=-1, keepdims=True) + e_ctx.sum(axis=-1, keepdims=True))
            o_loc.append(jnp.dot(e_loc.astype(BF16), v2, preferred_element_type=F32))
            p_ctx.append(e_ctx.astype(BF16))
        o_ctx_all = jnp.dot(jnp.concatenate(p_ctx, axis=0), vc2, preferred_element_type=F32)
        for a in range(g_rows):
            o = (o_loc[a] + o_ctx_all[a * m2:(a + 1) * m2]) / l_sum[a]
            o_ref[a * GRID_W:(a + 1) * GRID_W, ps] = jnp.where(lo, o[:GRID_W], o[GRID_W:]).astype(BF16)


def _na_bias_tables(rpb):
    wr = NA_WIN_R
    nrel = 2 * NA_WIN_C - 1
    cols = np.arange(GRID_W)
    col_start = np.clip(cols - NA_WIN_C // 2, 0, GRID_W - NA_WIN_C)
    ck = np.arange(GRID_W)[None, :]
    valid = (ck >= col_start[:, None]) & (ck < col_start[:, None] + NA_WIN_C)
    rel = ck - cols[:, None] + (NA_WIN_C - 1)
    onehot = (np.arange(nrel)[:, None, None] == rel[None]) & valid[None]
    onehot = jnp.asarray(onehot.reshape(nrel, GRID_W * GRID_W), F32)
    d = np.arange(wr)[:, None] + np.arange(wr)[None, :]
    t = jnp.einsum("lhdir,rq->lhdiq", rpb[:, :, d] * LOG2E, onehot, precision=HIGHEST)
    t = t.reshape(rpb.shape[0], NA_HEADS, wr, wr, GRID_W, GRID_W)
    t = jnp.where(valid[None, None, None, None], t, NEG_BIG)
    t = t.transpose(0, 2, 1, 4, 3, 5)
    return t.reshape(rpb.shape[0], wr, NA_HEADS, GRID_W, wr * GRID_W)


def _lat_na(z_att, kc, vc, tb, seq_len):
    n = z_att.shape[0]
    nb = n // seq_len
    rows = seq_len // GRID_W
    assert rows >= NA_WIN_R
    wr = NA_WIN_R
    p = kc.shape[1]
    g_rows = NA_ROWS_PER_STEP
    blk = rows // g_rows
    return pl.pallas_call(
        functools.partial(_lat_na_kernel, rows=rows, g_rows=g_rows),
        grid=(nb, blk),
        in_specs=[
            pl.BlockSpec((g_rows * GRID_W, NA_W), lambda b, r: (b * blk + r, 0)),
            pl.BlockSpec((seq_len, NA_W), lambda b, r: (b, 1)),
            pl.BlockSpec((seq_len, NA_W), lambda b, r: (b, 2)),
            pl.BlockSpec((1, p, NA_W), lambda b, r: (b, 0, 0)),
            pl.BlockSpec((1, p, NA_W), lambda b, r: (b, 0, 0)),
            pl.BlockSpec((wr, NA_HEADS, GRID_W, wr * GRID_W), lambda b, r: (0, 0, 0, 0),
                         pipeline_mode=pl.Buffered(1)),
        ],
        out_specs=pl.BlockSpec((g_rows * GRID_W, NA_W), lambda b, r: (b * blk + r, 0)),
        out_shape=jax.ShapeDtypeStruct((n, NA_W), BF16),
        compiler_params=_cparams(("arbitrary", "arbitrary")),
        name="lat_na",
    )(z_att, z_att, z_att, kc, vc, tb)


def _lat_gqa_kernel(q_ref, k_ref, v_ref, kc_ref, vc_ref, o_ref):
    kn2 = k_ref[...].astype(F32)
    v2 = v_ref[...].astype(F32)
    kc2 = kc_ref[0].astype(F32)
    vc2 = vc_ref[0].astype(F32)
    def with_ones(x2, g):
        d = _dup_head(x2, g)
        lo = _lane_lo(d.shape)
        return jnp.where(lo, d, 1.0).astype(BF16), jnp.where(lo, 1.0, d).astype(BF16)

    kd, kcd, v_lo, v_hi = [], [], [], []
    for g in range(GQA_KV_HEADS):
        kd.append(_dup_head(kn2, g).astype(BF16))
        kcd.append(_dup_head(kc2, g).astype(BF16))
        (a0, a1), (b0, b1) = with_ones(v2, g), with_ones(vc2, g)
        v_lo.append([a0, b0])
        v_hi.append([a1, b1])
    for p in range(GQA_HEADS // 2):
        g = (2 * p) // GQA_REP
        ps = slice(p * PAIR_W, (p + 1) * PAIR_W)
        q2 = q_ref[:, ps].astype(F32)
        o_ref[:, ps] = _pair_attn(q2, [kd[g], kcd[g]], v_lo[g], values_hi=v_hi[g]).astype(BF16)


def _rope_tables(seq_len):
    t = jnp.arange(seq_len)
    n_freq = HEAD_DIM // 4
    inv = ROPE_THETA ** (-jnp.arange(n_freq, dtype=F32) / n_freq)
    ang_r = (t // GRID_W).astype(F32)[:, None] * inv
    ang_c = (t % GRID_W).astype(F32)[:, None] * inv
    cr, sr, cc, sc = jnp.cos(ang_r), jnp.sin(ang_r), jnp.cos(ang_c), jnp.sin(ang_c)
    cos = jnp.concatenate([cr, cr, cc, cc] * 2, axis=-1)
    sin = jnp.concatenate([-sr, sr, -sc, sc] * 2, axis=-1)
    return cos, sin


def _lat_gqa(z_att, kc, vc, seq_len):
    n = z_att.shape[0]
    nb = n // seq_len
    tq = min(GQA_QBLOCK, seq_len)
    nqb = seq_len // tq
    p = kc.shape[1]
    qcol = (3 * NA_W) // GQA_QW
    kcol = (3 * NA_W + GQA_QW) // GQA_KW
    return pl.pallas_call(
        _lat_gqa_kernel,
        grid=(nb, nqb),
        in_specs=[
            pl.BlockSpec((tq, GQA_QW), lambda b, i: (b * nqb + i, qcol)),
            pl.BlockSpec((seq_len, GQA_KW), lambda b, i: (b, kcol)),
            pl.BlockSpec((seq_len, GQA_KW), lambda b, i: (b, kcol + 1)),
            pl.BlockSpec((1, p, GQA_KW), lambda b, i: (b, 0, 0)),
            pl.BlockSpec((1, p, GQA_KW), lambda b, i: (b, 0, 0)),
        ],
        out_specs=pl.BlockSpec((tq, GQA_QW), lambda b, i: (b * nqb + i, 0)),
        out_shape=jax.ShapeDtypeStruct((n, GQA_QW), BF16),
        compiler_params=_cparams(("arbitrary", "arbitrary")),
        name="lat_gqa",
    )(z_att, z_att, z_att, kc, vc)


def _layer_norm(x, g, b):
    mu = jnp.mean(x, axis=-1, keepdims=True)
    xc = x - mu
    var = jnp.mean(xc * xc, axis=-1, keepdims=True)
    return xc * lax.rsqrt(var + LN_EPS) * g + b


def _merge_kernel(x_ref, mod_ref, cp_ref, cpp_ref, cpn_ref, *rest, seq_len, n_att):
    att_refs = rest[:n_att]
    (wg_ref, wb_ref, wo_ref, pw_ref, cw_ref, cb_ref, ps_ref, lg_ref, lb_ref, wr_ref,
     x1_ref, xm2_ref, aff_ref, ucv_ref, upl_ref) = rest[n_att:]
    t = x_ref.shape[0]
    halo = POOL_HALO
    i = pl.program_id(0)
    pos0 = (i * t) % seq_len
    has_prev = pos0 > 0
    has_next = pos0 + t < seq_len

    shift1, scale1, gate1 = mod_ref[0, 0:1, :], mod_ref[0, 1:2, :], mod_ref[0, 2:3, :]
    shift2, scale2 = mod_ref[0, 3:4, :], mod_ref[0, 4:5, :]

    bw = BRANCH_W
    ucv_ref[halo:halo + t, :] = cp_ref[:, bw:2 * bw] * cp_ref[:, 2 * bw:3 * bw]
    upl_ref[halo:halo + t, :] = cp_ref[:, 3 * bw:4 * bw]
    ucv_ref[0:halo, :] = jnp.where(has_prev, cpp_ref[:, bw:2 * bw] * cpp_ref[:, 2 * bw:3 * bw], 0.0)
    upl_ref[0:halo, :] = jnp.where(has_prev, cpp_ref[:, 3 * bw:4 * bw], 0.0)
    ucv_ref[halo + t:, :] = jnp.where(has_next, cpn_ref[:, bw:2 * bw] * cpn_ref[:, 2 * bw:3 * bw], 0.0)
    upl_ref[halo + t:, :] = jnp.where(has_next, cpn_ref[:, 3 * bw:4 * bw], 0.0)

    wr = wr_ref[...]
    wh = wr.astype(BF16)
    wl = (wr - wh.astype(F32)).astype(BF16)

    th = min(t, MERGE_SUB)
    blocks = [dict(r0=r0, rows=slice(r0, r0 + th)) for r0 in range(0, t, th)]

    def prologue(b):
        r0, rows = b["r0"], b["rows"]
        x = x_ref[rows, :]
        b["x"] = x
        b["xm"] = (x * (1.0 + scale1) + shift1).astype(BF16)
        conv = (cw_ref[0:1, :] * ucv_ref[halo - 1 + r0:halo - 1 + r0 + th, :]
                + cw_ref[1:2, :] * ucv_ref[halo + r0:halo + r0 + th, :]
                + cw_ref[2:3, :] * ucv_ref[halo + 1 + r0:halo + 1 + r0 + th, :]) + cb_ref[...]
        o_conv = (cp_ref[rows, 0:bw] * conv).astype(BF16)
        pos = pos0 + r0 + lax.broadcasted_iota(I32, (th, 1), 0)
        mixed = []
        for g, win in enumerate(POOL_WINDOWS):
            gs = slice(g * POOL_GW, (g + 1) * POOL_GW)
            acc = None
            for dlt in range(-(win // 2), win // 2):
                term = upl_ref[halo + r0 + dlt:halo + r0 + dlt + th, gs]
                acc = term if acc is None else acc + term
            lo = jnp.maximum(pos - win // 2, 0)
            hi = jnp.minimum(pos + win // 2, seq_len)
            cnt = (hi - lo).astype(F32)
            pooled = acc / cnt - upl_ref[halo + r0:halo + r0 + th, gs]
            mixed.append(jnp.dot(pooled.astype(BF16), pw_ref[g], preferred_element_type=F32))
        o_pool = (jnp.concatenate(mixed, axis=-1) * ps_ref[...]).astype(BF16)
        if n_att == 1:
            b["branches"] = [att_refs[0][rows, 0:bw], att_refs[0][rows, bw:2 * bw], o_conv, o_pool]
        else:
            b["branches"] = [att_refs[0][rows, :], att_refs[1][rows, :], o_conv, o_pool]
        b["merged"] = None

    def dots(b, nb):
        b["zg"] = jnp.dot(b["xm"], wg_ref[:, nb * D_MODEL:(nb + 1) * D_MODEL], preferred_element_type=F32)
        b["proj"] = jnp.dot(b["branches"][nb], wb_ref[nb], preferred_element_type=F32)

    def gate(b):
        term = _sigmoid(b["zg"]) * b["proj"]
        b["merged"] = term if b["merged"] is None else b["merged"] + term

    def out_proj(b):
        b["y"] = jnp.dot(b["merged"].astype(BF16), wo_ref[...], preferred_element_type=F32)

    def tail(b):
        rows = b["rows"]
        x1 = _layer_norm(DEEPNORM_ALPHA * b["x"] + gate1 * b["y"], lg_ref[...], lb_ref[...])
        x1_ref[rows, :] = x1
        xm2 = x1 * (1.0 + scale2) + shift2
        xm2_ref[rows, :] = xm2
        xh = xm2.astype(BF16)
        xl = (xm2 - xh.astype(F32)).astype(BF16)
        logits = (jnp.dot(xh, wh, preferred_element_type=F32) + jnp.dot(xl, wh, preferred_element_type=F32)
                  + jnp.dot(xh, wl, preferred_element_type=F32))
        m = logits.max(axis=-1, keepdims=True)
        e = jnp.exp(logits - m)
        aff_ref[rows, :] = e / e.sum(axis=-1, keepdims=True)

    for b in blocks:
        prologue(b)
        for nb in range(N_BRANCH):
            dots(b, nb)
            gate(b)
        out_proj(b)
        tail(b)


def _merge(x, mod, z_cp, atts, w_gates, w_branch, w_out, pool_w, conv_w, conv_b, pool_scale, ln_g, ln_b,
           w_router, seq_len):
    n = x.shape[0]
    t = min(MERGE_TILE, seq_len)
    per = n // mod.shape[0] // t
    hb = t // POOL_HALO
    nhb = n // POOL_HALO
    row = lambda a: a.reshape(1, -1)
    const2 = lambda i: (0, 0)
    const3 = lambda i: (0, 0, 0)
    once = lambda shp, imap: pl.BlockSpec(shp, imap, pipeline_mode=pl.Buffered(1))
    att_specs = [pl.BlockSpec((t, a.shape[1]), lambda i: (i, 0)) for a in atts]
    return pl.pallas_call(
        functools.partial(_merge_kernel, seq_len=seq_len, n_att=len(atts)),
        grid=(n // t,),
        in_specs=[
            pl.BlockSpec((t, D_MODEL), lambda i: (i, 0)),
            pl.BlockSpec((1, N_MOD, D_MODEL), lambda i: (i // per, 0, 0)),
            pl.BlockSpec((t, CP_W), lambda i: (i, 0)),
            pl.BlockSpec((POOL_HALO, CP_W), lambda i: (jnp.maximum(i * hb - 1, 0), 0)),
            pl.BlockSpec((POOL_HALO, CP_W), lambda i: (jnp.minimum((i + 1) * hb, nhb - 1), 0)),
            *att_specs,
            once((D_MODEL, GATE_W), const2),
            once((N_BRANCH, BRANCH_W, D_MODEL), const3),
            once((D_MODEL, D_MODEL), const2),
            pl.BlockSpec((len(POOL_WINDOWS), POOL_GW, POOL_GW), const3),
            pl.BlockSpec((CONV_K, BRANCH_W), const2),
            pl.BlockSpec((1, BRANCH_W), const2),
            pl.BlockSpec((1, BRANCH_W), const2),
            pl.BlockSpec((1, D_MODEL), const2),
            pl.BlockSpec((1, D_MODEL), const2),
            pl.BlockSpec((D_MODEL, N_EXPERTS), const2),
        ],
        out_specs=[
            pl.BlockSpec((t, D_MODEL), lambda i: (i, 0)),
            pl.BlockSpec((t, D_MODEL), lambda i: (i, 0)),
            pl.BlockSpec((t, N_EXPERTS), lambda i: (i, 0)),
        ],
        out_shape=[jax.ShapeDtypeStruct((n, D_MODEL), F32), jax.ShapeDtypeStruct((n, D_MODEL), F32),
                   jax.ShapeDtypeStruct((n, N_EXPERTS), F32)],
        scratch_shapes=[pltpu.VMEM((t + 2 * POOL_HALO, BRANCH_W), F32),
                        pltpu.VMEM((t + 2 * POOL_HALO, BRANCH_W), F32)],
        compiler_params=_cparams(("arbitrary",)),
        name="merge",
    )(x, mod, z_cp, z_cp, z_cp, *atts, w_gates, w_branch, w_out, pool_w, conv_w, row(conv_b), row(pool_scale),
      row(ln_g), row(ln_b), w_router)


def _topk_kernel(aff_ref, idx_ref, dst_ref, gp_ref, cnt_ref, thr_ref, rank_ref, *, cap, plane_stride):
    ne, nc, _ = aff_ref.shape
    capf = float(cap)
    bits_all = pltpu.bitcast(aff_ref[...], I32)

    def search(i, cur):
        cand = cur | (jnp.int32(1) << (30 - i))
        cnt = jnp.sum((bits_all >= cand).astype(F32), axis=(1, 2), keepdims=True)
        return jnp.where(cnt >= capf, cand, cur)

    thr = lax.fori_loop(0, 31, search, jnp.zeros((ne, 1, 1), I32))
    thr_ref[...] = jnp.broadcast_to(thr, thr_ref.shape)

    jj = lax.broadcasted_iota(I32, (CHUNK, CHUNK), 0)
    kk = lax.broadcasted_iota(I32, (CHUNK, CHUNK), 1)
    ut_incl = (jj <= kk).astype(BF16)
    lt_incl = (kk <= jj).astype(BF16)
    cc = lax.broadcasted_iota(I32, (nc, nc), 0)
    dd = lax.broadcasted_iota(I32, (nc, nc), 1)
    cl_excl = (dd < cc).astype(BF16)
    cl_incl = (dd <= cc).astype(BF16)
    eye = (jj == kk).astype(BF16)
    wide = 4 if cap % (4 * CHUNK) == 0 else 1
    lane_w = lax.broadcasted_iota(I32, (1, wide * CHUNK), 1)
    sub_cw = lax.broadcasted_iota(I32, (nc, wide * CHUNK), 0).astype(F32)
    sub_jw = lax.broadcasted_iota(I32, (CHUNK, wide * CHUNK), 0).astype(F32)
    rank_ref[...] = jnp.zeros(rank_ref.shape, F32)
    gp_ref[...] = jnp.full(gp_ref.shape, -1.0, F32)

    def per_expert(e, carry):
        a = aff_ref[e]
        b = pltpu.bitcast(a, I32)
        t = thr_ref[e][0:1, :]
        gt = b > t
        eq = b == t
        need = capf - jnp.sum(gt.astype(F32), keepdims=True)
        eqf = eq.astype(F32)
        incl_eq = jnp.dot(eqf.astype(BF16), ut_incl, preferred_element_type=F32)
        tot_eq = jnp.broadcast_to(incl_eq[:, CHUNK - 1:CHUNK], (nc, CHUNK))
        cum_eq = jnp.dot(cl_excl, tot_eq.astype(BF16), preferred_element_type=F32)
        rank = cum_eq + incl_eq - eqf
        sel = jnp.logical_or(gt, jnp.logical_and(eq, rank < need))
        self_ = sel.astype(F32)

        plane = rank_ref[...]
        rank_ref[...] = plane + self_
        for k in range(ne):
            gp_ref[k] = jnp.where(jnp.logical_and(sel, plane == float(k)), a, gp_ref[k])
        plane_tb = _dot_nt(eye, plane.astype(BF16)).astype(BF16)

        selb = self_.astype(BF16)
        incl = jnp.dot(selb, ut_incl, preferred_element_type=F32)
        tot = jnp.broadcast_to(incl[:, CHUNK - 1:CHUNK], (nc, CHUNK))
        cum_incl = jnp.dot(cl_incl, tot.astype(BF16), preferred_element_type=F32)
        incl_tb = _dot_nt(lt_incl, selb).astype(BF16)

        def per_block(sb, c2):
            s_row = (sb * (wide * CHUNK) + lane_w).astype(F32)
            m = cum_w <= s_row
            c_of_s = jnp.sum(m.astype(F32), axis=0, keepdims=True)
            excl_s = jnp.max(jnp.where(m, cum_w, 0.0), axis=0, keepdims=True)
            onehot_t = (sub_cw == c_of_s).astype(BF16)
            rows_t = jnp.dot(incl_tb, onehot_t, preferred_element_type=F32)
            s_local = s_row - excl_s
            t_local = jnp.sum((rows_t <= s_local).astype(F32), axis=0, keepdims=True)
            tok = (c_of_s * float(CHUNK) + t_local).astype(I32)
            plane_rows = jnp.dot(plane_tb, onehot_t, preferred_element_type=F32)
            plane_s = jnp.sum(jnp.where(sub_jw == t_local, plane_rows, 0.0), axis=0, keepdims=True)
            dst = plane_s.astype(I32) * plane_stride + tok
            for u in range(wide):
                idx_ref[e, pl.ds(sb * wide + u, 1), :] = tok[:, u * CHUNK:(u + 1) * CHUNK]
                dst_ref[e, pl.ds(sb * wide + u, 1), :] = dst[:, u * CHUNK:(u + 1) * CHUNK]
            return c2

        cum_w = jnp.concatenate([cum_incl] * wide, axis=1)
        lax.fori_loop(0, cap // (wide * CHUNK), per_block, 0)
        return carry

    lax.fori_loop(0, ne, per_expert, 0)
    cnt_ref[...] = jnp.broadcast_to(jnp.max(rank_ref[...], axis=1, keepdims=True), cnt_ref.shape).astype(I32)


def _topk(aff3, cap, plane_stride):
    ne, nc, _ = aff3.shape
    full3 = lambda shp: pl.BlockSpec(shp, lambda i: (0, 0, 0))
    return pl.pallas_call(
        functools.partial(_topk_kernel, cap=cap, plane_stride=plane_stride),
        grid=(1,),
        in_specs=[full3((ne, nc, CHUNK))],
        out_specs=[full3((ne, cap // CHUNK, CHUNK)), full3((ne, cap // CHUNK, CHUNK)), full3((ne, nc, CHUNK)),
                   pl.BlockSpec((nc, CHUNK), lambda i: (0, 0))],
        out_shape=[jax.ShapeDtypeStruct((ne, cap // CHUNK, CHUNK), I32),
                   jax.ShapeDtypeStruct((ne, cap // CHUNK, CHUNK), I32),
                   jax.ShapeDtypeStruct((ne, nc, CHUNK), F32),
                   jax.ShapeDtypeStruct((nc, CHUNK), I32)],
        scratch_shapes=[pltpu.VMEM((ne, SUBLANES, CHUNK), I32), pltpu.VMEM((nc, CHUNK), F32)],
        compiler_params=_cparams(("arbitrary",)),
        name="topk",
    )(aff3)


def _ffn_kernel(idx_ref, idxn_ref, dst_ref, x_hbm, wg_ref, wu_ref, wd_ref, y_in_hbm, y_hbm, xs_ref, ys_ref, wgu_ref,
                wdn_ref, gsem_ref, ssem_ref, *, steps):
    tm = xs_ref.shape[1]
    s = pl.program_id(0) * pl.num_programs(1) + pl.program_id(1)

    @pl.when(pl.program_id(1) == 0)
    def _():
        wgu_ref[0] = wg_ref[0].astype(BF16)
        wgu_ref[1] = wu_ref[0].astype(BF16)
        wdn_ref[...] = wd_ref[0].astype(BF16)

    def gather_rows(idx_smem, buf_slot):
        for j in range(tm):
            pltpu.make_async_copy(x_hbm.at[pl.ds(idx_smem[0, 0, j], 1), :], xs_ref.at[buf_slot, pl.ds(j, 1), :],
                                  gsem_ref.at[buf_slot]).start()

    def gathered(buf_slot):
        return pltpu.make_async_copy(x_hbm.at[pl.ds(0, tm), :], xs_ref.at[buf_slot], gsem_ref.at[buf_slot])

    def scattered(buf_slot):
        return pltpu.make_async_copy(ys_ref.at[buf_slot], y_hbm.at[pl.ds(0, tm), :], ssem_ref.at[buf_slot])

    @pl.when(s == 0)
    def _():
        gather_rows(idx_ref, 0)

    def step(slot):
        @pl.when(s + 1 < steps)
        def _():
            gather_rows(idxn_ref, 1 - slot)

        gathered(slot).wait()
        xs = xs_ref[slot].astype(BF16)
        hg = jnp.dot(xs, wgu_ref[0], preferred_element_type=F32)
        hu = jnp.dot(xs, wgu_ref[1], preferred_element_type=F32)
        hdn = (hg * _sigmoid(hg) * hu).astype(BF16)
        ye = jnp.dot(hdn, wdn_ref[...], preferred_element_type=F32)

        @pl.when(s >= 2)
        def _():
            scattered(slot).wait()

        ys_ref[slot] = _pack_bf16_pair(ye[:, :PACK_W], ye[:, PACK_W:])
        for j in range(tm):
            pltpu.make_async_copy(ys_ref.at[slot, pl.ds(j, 1), :], y_hbm.at[pl.ds(dst_ref[0, 0, j], 1), :],
                                  ssem_ref.at[slot]).start(priority=j % 2)

        @pl.when(s == steps - 1)
        def _():
            scattered(slot).wait()
            if steps > 1:
                scattered(1 - slot).wait()

    for parity in range(2):
        pl.when(s % 2 == parity)(functools.partial(step, parity))


def _ffn(idx_tiles, dst_tiles, xm2, w_gate, w_up, w_down, layer, cap, ybuf):
    tm = idx_tiles.shape[2]
    nt = cap // tm
    steps = N_EXPERTS * nt
    wspec = lambda: pl.BlockSpec((None, 1, D_MODEL, EXPERT_FF), lambda e, i: (layer, e, 0, 0))
    return pl.pallas_call(
        functools.partial(_ffn_kernel, steps=steps),
        grid=(N_EXPERTS, nt),
        in_specs=[
            pl.BlockSpec((1, 1, tm), lambda e, i: (e * nt + i, 0, 0), memory_space=pltpu.SMEM),
            pl.BlockSpec((1, 1, tm), lambda e, i: (jnp.minimum(e * nt + i + 1, steps - 1), 0, 0),
                         memory_space=pltpu.SMEM),
            pl.BlockSpec((1, 1, tm), lambda e, i: (e * nt + i, 0, 0), memory_space=pltpu.SMEM),
            pl.BlockSpec(memory_space=pl.ANY),
            wspec(), wspec(),
            pl.BlockSpec((None, 1, EXPERT_FF, D_MODEL), lambda e, i: (layer, e, 0, 0)),
            pl.BlockSpec(memory_space=pl.ANY),
        ],
        out_specs=pl.BlockSpec(memory_space=pl.ANY),
        out_shape=jax.ShapeDtypeStruct(ybuf.shape, U32),
        input_output_aliases={7: 0},
        scratch_shapes=[pltpu.VMEM((2, tm, D_MODEL), F32), pltpu.VMEM((2, tm, PACK_W), U32),
                        pltpu.VMEM((2, D_MODEL, EXPERT_FF), BF16), pltpu.VMEM((EXPERT_FF, D_MODEL), BF16),
                        pltpu.SemaphoreType.DMA((2,)), pltpu.SemaphoreType.DMA((2,))],
        compiler_params=_cparams(("arbitrary", "arbitrary")),
        name="ffn",
    )(idx_tiles, idx_tiles, dst_tiles, xm2, w_gate, w_up, w_down, ybuf)


def _combine_kernel(kmax_ref, y_hbm, x1_ref, mod_ref, gp_ref, lg_ref, lb_ref, o_ref, planes_ref, acc_ref, sem_ref,
                    *, n_planes):
    t = x1_ref.shape[0]
    i = pl.program_id(0)
    ntile = pl.num_programs(0)
    slot = i % 2

    def for_planes(tile, buf_slot, fn):
        for k in range(n_planes):
            @pl.when(k < kmax_ref[tile])
            def _(k=k):
                fn(pltpu.make_async_copy(y_hbm.at[k, pl.ds(tile * t, t), :], planes_ref.at[buf_slot, k],
                                         sem_ref.at[buf_slot]))

    @pl.when(i == 0)
    def _():
        for_planes(0, 0, lambda cp: cp.start())

    @pl.when(i + 1 < ntile)
    def _():
        for_planes(i + 1, 1 - slot, lambda cp: cp.start())

    for_planes(i, slot, lambda cp: cp.wait())

    acc_ref[...] = jnp.zeros(acc_ref.shape, F32)
    for k in range(n_planes):
        @pl.when(k < kmax_ref[i])
        def _(k=k):
            w = gp_ref[:, k:k + 1]
            lo, hi = _unpack_bf16_pair(planes_ref[slot, k])
            acc_ref[:, :PACK_W] += jnp.where(w >= 0.0, w * lo, 0.0)
            acc_ref[:, PACK_W:] += jnp.where(w >= 0.0, w * hi, 0.0)

    gate2 = mod_ref[0, 5:6, :]
    o_ref[...] = _layer_norm(DEEPNORM_ALPHA * x1_ref[...] + gate2 * acc_ref[...], lg_ref[...], lb_ref[...])


def _combine(kmax, y_planes, x1, mod, gp_tok, ln_g, ln_b, seq_len):
    n = x1.shape[0]
    n_planes = y_planes.shape[0]
    t = min(TOK_TILE, seq_len)
    per = n // mod.shape[0] // t
    row = lambda a: a.reshape(1, -1)
    grid_spec = pltpu.PrefetchScalarGridSpec(
        num_scalar_prefetch=1,
        grid=(n // t,),
        in_specs=[
            pl.BlockSpec(memory_space=pl.ANY),
            pl.BlockSpec((t, D_MODEL), lambda i, *_: (i, 0)),
            pl.BlockSpec((1, N_MOD, D_MODEL), lambda i, *_: (i // per, 0, 0)),
            pl.BlockSpec((t, n_planes), lambda i, *_: (i, 0)),
            pl.BlockSpec((1, D_MODEL), lambda i, *_: (0, 0)),
            pl.BlockSpec((1, D_MODEL), lambda i, *_: (0, 0)),
        ],
        out_specs=pl.BlockSpec((t, D_MODEL), lambda i, *_: (i, 0)),
        scratch_shapes=[pltpu.VMEM((2, n_planes, t, PACK_W), U32), pltpu.VMEM((t, D_MODEL), F32),
                        pltpu.SemaphoreType.DMA((2,))],
    )
    return pl.pallas_call(
        functools.partial(_combine_kernel, n_planes=n_planes),
        grid_spec=grid_spec,
        out_shape=jax.ShapeDtypeStruct((n, D_MODEL), F32),
        compiler_params=_cparams(("arbitrary",)),
        name="combine",
    )(kmax, y_planes, x1, mod, gp_tok, row(ln_g), row(ln_b))


def _moe(x1, xm2, aff, mod, w_gate, w_up, w_down, layer, ln_g, ln_b, seq_len, ybuf):
    n = x1.shape[0]
    cap = EC_CAPACITY * n // N_EXPERTS
    nc = n // CHUNK
    t = min(TOK_TILE, seq_len)
    stride = ybuf.shape[0] // N_EXPERTS
    aff3 = aff.T.reshape(N_EXPERTS, nc, CHUNK)
    idx3, dst3, gp3, cnt = _topk(aff3, cap, stride)
    gp_tok = gp3.reshape(N_EXPERTS, n).T
    kmax = jnp.max(cnt[:, 0].reshape(n // t, t // CHUNK), axis=1)
    tm = min(FFN_TILE, cap)
    idx_tiles = idx3.reshape(N_EXPERTS * cap // tm, 1, tm)
    dst_tiles = dst3.reshape(N_EXPERTS * cap // tm, 1, tm)
    ybuf = _ffn(idx_tiles, dst_tiles, xm2, w_gate, w_up, w_down, layer, cap, ybuf)
    y = _combine(kmax, ybuf.reshape(N_EXPERTS, stride, PACK_W), x1, mod, gp_tok, ln_g, ln_b, seq_len)
    return y, ybuf


def kernel(x_prompt, x_sample, cache_na_k, cache_na_v, cache_gqa_k, cache_gqa_v, c, c_ctx, w_ada, b_ada, w_in,
           na_rpb, qn_g, kn_g, conv_w, conv_b, pool_w, pool_scale, w_branch, w_out, ln1_g, ln1_b, ln2_g, ln2_b,
           w_router, w_gate, w_up, w_down):
    depth = w_in.shape[0]
    bp, lp, _ = x_prompt.shape
    bs, ls, _ = x_sample.shape
    past = cache_na_k.shape[2]

    mods = _adaln(jnp.concatenate([c_ctx[None, :], c], axis=0), w_ada, b_ada)
    mods = mods.reshape(depth, 1 + bs, N_MOD, D_MODEL)
    cos, sin = _rope_tables(ls)
    tbs = _na_bias_tables(na_rpb)

    yp = x_prompt.reshape(bp * lp, D_MODEL)
    ys = x_sample.reshape(bs * ls, D_MODEL)
    ybuf_p = jnp.zeros((N_EXPERTS * bp * lp, PACK_W), U32)
    ybuf_s = jnp.zeros((N_EXPERTS * bs * ls, PACK_W), U32)
    na_k_l, na_v_l, g_k_l, g_v_l = [], [], [], []
    for l in range(depth):
        mod_c = mods[l, 0:1]
        mod_s = mods[l, 1:]
        w_a = w_in[l, :, :ATT_W + CP_W].astype(BF16)
        w_gates = w_in[l, :, ATT_W + CP_W:].astype(BF16)
        w_br = w_branch[l].astype(BF16)
        w_o = w_out[l].astype(BF16)
        p_w = pool_w[l].astype(BF16)
        moe_w = (w_gate, w_up, w_down, l)
        merge_w = (w_gates, w_br, w_o, p_w, conv_w[l], conv_b[l], pool_scale[l], ln1_g[l], ln1_b[l], w_router[l])

        qn2 = jnp.tile(qn_g[l], 2).reshape(1, PAIR_W)
        kn2 = jnp.tile(kn_g[l], 2).reshape(1, PAIR_W)
        cache = lambda a, w: a[:, l].reshape(bs, past, w).astype(BF16)

        z_att, z_cp, na_k, na_v, g_k, g_v = _inproj(yp, mod_c, w_a, bp * lp, kn2)
        o_att = _ctx_attn(z_att, qn2, kn2, lp)
        na_k_l.append(na_k.reshape(bp, lp, NA_HEADS, HEAD_DIM))
        na_v_l.append(na_v.reshape(bp, lp, NA_HEADS, HEAD_DIM))
        g_k_l.append(g_k.reshape(bp, lp, GQA_KV_HEADS, HEAD_DIM))
        g_v_l.append(g_v.reshape(bp, lp, GQA_KV_HEADS, HEAD_DIM))
        x1, xm2, aff = _merge(yp, mod_c, z_cp, [o_att], *merge_w, lp)
        yp, ybuf_p = _moe(x1, xm2, aff, mod_c, *moe_w, ln2_g[l], ln2_b[l], lp, ybuf_p)

        z_att, z_cp = _inproj(ys, mod_s, w_a, ls, kn2, qn2, cos, sin)
        o_na = _lat_na(z_att, cache(cache_na_k, NA_W), cache(cache_na_v, NA_W), tbs[l], ls)
        o_gqa = _lat_gqa(z_att, cache(cache_gqa_k, GQA_KW), cache(cache_gqa_v, GQA_KW), ls)
        x1, xm2, aff = _merge(ys, mod_s, z_cp, [o_na, o_gqa], *merge_w, ls)
        ys, ybuf_s = _moe(x1, xm2, aff, mod_s, *moe_w, ln2_g[l], ln2_b[l], ls, ybuf_s)

    return (yp.reshape(bp, lp, D_MODEL), ys.reshape(bs, ls, D_MODEL),
            jnp.stack(na_k_l, axis=1), jnp.stack(na_v_l, axis=1),
            jnp.stack(g_k_l, axis=1), jnp.stack(g_v_l, axis=1))
```

```python
import functools

import numpy as np
import jax
import jax.numpy as jnp
from jax import lax
from jax.experimental import pallas as pl
from jax.experimental.pallas import tpu as pltpu

F32 = jnp.float32
BF16 = jnp.bfloat16
I32 = jnp.int32
U32 = jnp.uint32
HIGHEST = lax.Precision.HIGHEST

D_MODEL = 1024
NOMINAL_DEPTH = 4
GRID_W = 64
HEAD_DIM = 64
NA_HEADS = 8
NA_WIN_R = 8
NA_WIN_C = 16
GQA_HEADS = 8
GQA_KV_HEADS = 2
GQA_REP = GQA_HEADS // GQA_KV_HEADS
BRANCH_W = 512
N_BRANCH = 4
CONV_K = 3
POOL_WINDOWS = (2, 4, 8, 16)
POOL_GW = BRANCH_W // len(POOL_WINDOWS)
POOL_HALO = max(POOL_WINDOWS) // 2
N_EXPERTS = 16
EXPERT_FF = 1024
EC_CAPACITY = 2
ROPE_THETA = 10000.0
LN_EPS = 1e-6
RMS_EPS = 1e-6
N_MOD = 6
DEEPNORM_ALPHA = (2 * NOMINAL_DEPTH) ** 0.25
NA_W = NA_HEADS * HEAD_DIM
GQA_QW = GQA_HEADS * HEAD_DIM
GQA_KW = GQA_KV_HEADS * HEAD_DIM
ATT_W = 3 * NA_W + GQA_QW + 2 * GQA_KW
CP_W = 4 * BRANCH_W
GATE_W = N_BRANCH * D_MODEL
LOG2E = 1.4426950408889634
Q_SCALE = HEAD_DIM ** -0.5 * LOG2E
NEG_BIG = -1e30
PAIR_W = 2 * HEAD_DIM
PACK_W = D_MODEL // 2

LANES = 128
SUBLANES = 8
VMEM_LIMIT = 56 * 1024 * 1024

TOK_TILE_IN = 512
MERGE_TILE = 512
MERGE_SUB = 256
TOK_TILE = 512
CHUNK = LANES
FFN_TILE = 512
GQA_QBLOCK = 512
NA_ROWS_PER_STEP = 4


def _cparams(sem):
    return pltpu.CompilerParams(dimension_semantics=sem, vmem_limit_bytes=VMEM_LIMIT)


def _pack_bf16_pair(a, b):
    lo = pltpu.bitcast(a.astype(BF16).astype(F32), U32)
    hi = pltpu.bitcast(b.astype(BF16).astype(F32), U32)
    return (lo >> 16) | (hi & jnp.uint32(0xFFFF0000))


def _unpack_bf16_pair(w):
    return pltpu.bitcast(w << 16, F32), pltpu.bitcast(w & jnp.uint32(0xFFFF0000), F32)


def _sigmoid(x):
    return 0.5 * jnp.tanh(0.5 * x) + 0.5


def _adaln_kernel(cv_ref, w_ref, b_ref, o_ref):
    cv = cv_ref[...]
    s = cv * jax.nn.sigmoid(cv)
    o_ref[0] = jnp.dot(s, w_ref[0], precision=HIGHEST, preferred_element_type=F32) + b_ref[0]


def _adaln(cvecs, w_ada, b_ada):
    depth = w_ada.shape[0]
    r = cvecs.shape[0]
    tn = 1536
    nw = N_MOD * D_MODEL
    return pl.pallas_call(
        _adaln_kernel,
        grid=(depth, nw // tn),
        in_specs=[
            pl.BlockSpec((r, D_MODEL), lambda l, j: (0, 0)),
            pl.BlockSpec((1, D_MODEL, tn), lambda l, j: (l, 0, j)),
            pl.BlockSpec((1, 1, tn), lambda l, j: (l, 0, j)),
        ],
        out_specs=pl.BlockSpec((1, r, tn), lambda l, j: (l, 0, j)),
        out_shape=jax.ShapeDtypeStruct((depth, r, nw), F32),
        compiler_params=_cparams(("arbitrary", "arbitrary")),
        name="adaln",
    )(cvecs, w_ada, b_ada.reshape(depth, 1, nw))


def _lane_lo(shape):
    return (lax.broadcasted_iota(I32, shape, len(shape) - 1) % PAIR_W) < HEAD_DIM


def _pair_rms(x2, g2):
    lo = _lane_lo(x2.shape)
    sq = x2 * x2
    ss_lo = jnp.sum(jnp.where(lo, sq, 0.0), axis=-1, keepdims=True)
    ss_hi = jnp.sum(jnp.where(lo, 0.0, sq), axis=-1, keepdims=True)
    inv = lax.rsqrt(jnp.where(lo, ss_lo, ss_hi) * (1.0 / HEAD_DIM) + RMS_EPS)
    return x2 * inv * g2


def _pair_rope(x2, cos2, sin2):
    q = HEAD_DIM // 4
    first = (lax.broadcasted_iota(I32, x2.shape, 1) % (2 * q)) < q
    swapped = jnp.where(first, pltpu.roll(x2, PAIR_W - q, 1), pltpu.roll(x2, q, 1))
    return x2 * cos2 + swapped * sin2


def _inproj_kernel(x_ref, mod_ref, w_ref, *rest, with_state):
    if with_state:
        kn_ref, att_ref, cp_ref, nak_ref, nav_ref, gk_ref, gv_ref = rest
    else:
        qn_ref, kn_ref, cos_ref, sin_ref, att_ref, cp_ref = rest
    shift = mod_ref[0, 0:1, :]
    scale = mod_ref[0, 1:2, :]
    xm = (x_ref[...] * (1.0 + scale) + shift).astype(BF16)
    z = jnp.dot(xm, w_ref[...], preferred_element_type=F32)
    cp_ref[...] = z[:, ATT_W:]
    qoff = 3 * NA_W
    koff = qoff + GQA_QW
    if with_state:
        att_ref[...] = z[:, :ATT_W].astype(BF16)
        nak_ref[...] = z[:, NA_W:2 * NA_W]
        nav_ref[...] = z[:, 2 * NA_W:3 * NA_W]
        gk_ref[...] = _pair_rms(z[:, koff:koff + GQA_KW], kn_ref[...])
        gv_ref[...] = z[:, koff + GQA_KW:koff + 2 * GQA_KW]
    else:
        att_ref[:, :qoff] = z[:, :qoff].astype(BF16)
        for p in range(GQA_HEADS // 2):
            ps = slice(qoff + p * PAIR_W, qoff + (p + 1) * PAIR_W)
            q2 = _pair_rope(_pair_rms(z[:, ps], qn_ref[...]), cos_ref[...], sin_ref[...]) * Q_SCALE
            att_ref[:, ps] = q2.astype(BF16)
        k2 = _pair_rope(_pair_rms(z[:, koff:koff + GQA_KW], kn_ref[...]), cos_ref[...], sin_ref[...])
        att_ref[:, koff:koff + GQA_KW] = k2.astype(BF16)
        att_ref[:, koff + GQA_KW:] = z[:, koff + GQA_KW:ATT_W].astype(BF16)


def _inproj(x, mod, w_a, seq_len, kn2, qn2=None, cos=None, sin=None):
    n = x.shape[0]
    t = min(TOK_TILE_IN, seq_len)
    per = n // mod.shape[0] // t
    with_state = qn2 is None
    tile = lambda w: pl.BlockSpec((t, w), lambda i: (i, 0))
    vec = pl.BlockSpec((1, PAIR_W), lambda i: (0, 0))
    in_specs = [
        pl.BlockSpec((t, D_MODEL), lambda i: (i, 0)),
        pl.BlockSpec((1, N_MOD, D_MODEL), lambda i: (i // per, 0, 0)),
        pl.BlockSpec((D_MODEL, ATT_W + CP_W), lambda i: (0, 0)),
    ]
    out_specs = [tile(ATT_W), tile(CP_W)]
    out_shape = [jax.ShapeDtypeStruct((n, ATT_W), BF16), jax.ShapeDtypeStruct((n, CP_W), F32)]
    if with_state:
        in_specs.append(vec)
        args = [x, mod, w_a, kn2]
        for w in (NA_W, NA_W, GQA_KW, GQA_KW):
            out_specs.append(tile(w))
            out_shape.append(jax.ShapeDtypeStruct((n, w), F32))
    else:
        nblk = seq_len // t
        pos = pl.BlockSpec((t, PAIR_W), lambda i: (i % nblk, 0))
        in_specs += [vec, vec, pos, pos]
        args = [x, mod, w_a, qn2, kn2, cos, sin]
    return pl.pallas_call(
        functools.partial(_inproj_kernel, with_state=with_state),
        grid=(n // t,),
        in_specs=in_specs,
        out_specs=out_specs,
        out_shape=out_shape,
        compiler_params=_cparams(("arbitrary",)),
        name="inproj",
    )(*args)


def _dup_head(x2, g):
    lo = _lane_lo(x2.shape)
    other = pltpu.roll(x2, HEAD_DIM, 1)
    return jnp.where(lo, x2, other) if g == 0 else jnp.where(lo, other, x2)


def _dot_nt(a, b):
    return lax.dot_general(a, b, (((1,), (1,)), ((), ())), preferred_element_type=F32)


def _softmax_pv(scores, values):
    m = scores[0].max(axis=-1, keepdims=True)
    for s in scores[1:]:
        m = jnp.maximum(m, s.max(axis=-1, keepdims=True))
    l = None
    o = None
    for s, v in zip(scores, values):
        p = jnp.exp2(s - m)
        ls = p.sum(axis=-1, keepdims=True)
        os_ = jnp.dot(p.astype(BF16), v, preferred_element_type=F32)
        l = ls if l is None else l + ls
        o = os_ if o is None else o + os_
    return o / l


def _pair_attn(q2, keys, values, biases=None, stack=False, values_hi=None):
    lo = _lane_lo(q2.shape)
    m_rows = q2.shape[0]
    q_halves = [jnp.where(lo, q2, 0.0).astype(BF16), jnp.where(lo, 0.0, q2).astype(BF16)]

    def run(q, row0):
        scores = []
        for bi, k2 in enumerate(keys):
            s = _dot_nt(q, k2)
            if biases is not None and biases[bi] is not None:
                s = s + biases[bi][row0:row0 + q.shape[0]]
            scores.append(s)
        return _softmax_pv(scores, values)

    if stack:
        o = run(jnp.concatenate(q_halves, axis=0), 0)
        return jnp.where(lo, o[:m_rows], o[m_rows:])
    if values_hi is None:
        return jnp.where(lo, run(q_halves[0], 0), run(q_halves[1], m_rows))

    def run_mxu_sum(q, vals):
        scores = [_dot_nt(q, k2) for k2 in keys]
        m = scores[0].max(axis=-1, keepdims=True)
        for s in scores[1:]:
            m = jnp.maximum(m, s.max(axis=-1, keepdims=True))
        o = None
        for s, v in zip(scores, vals):
            os_ = jnp.dot(jnp.exp2(s - m).astype(BF16), v, preferred_element_type=F32)
            o = os_ if o is None else o + os_
        return o / pltpu.roll(o, HEAD_DIM, 1)

    return jnp.where(lo, run_mxu_sum(q_halves[0], values), run_mxu_sum(q_halves[1], values_hi))


def _ctx_attn_kernel(z_ref, qn_ref, kn_ref, o_ref):
    for p in range(NA_HEADS // 2):
        ps = slice(p * PAIR_W, (p + 1) * PAIR_W)
        q2 = z_ref[:, ps].astype(F32) * Q_SCALE
        k2 = z_ref[:, NA_W + p * PAIR_W:NA_W + (p + 1) * PAIR_W]
        v2 = z_ref[:, 2 * NA_W + p * PAIR_W:2 * NA_W + (p + 1) * PAIR_W]
        o_ref[:, ps] = _pair_attn(q2, [k2], [v2], stack=True).astype(BF16)
    qoff = 3 * NA_W
    koff = qoff + GQA_QW
    voff = koff + GQA_KW
    kn2 = _pair_rms(z_ref[:, koff:koff + GQA_KW].astype(F32), kn_ref[...])
    v2 = z_ref[:, voff:voff + GQA_KW].astype(F32)
    kd = [_dup_head(kn2, g).astype(BF16) for g in range(GQA_KV_HEADS)]
    vd = [_dup_head(v2, g).astype(BF16) for g in range(GQA_KV_HEADS)]
    for p in range(GQA_HEADS // 2):
        g = (2 * p) // GQA_REP
        q2 = _pair_rms(z_ref[:, qoff + p * PAIR_W:qoff + (p + 1) * PAIR_W].astype(F32), qn_ref[...]) * Q_SCALE
        o_ref[:, NA_W + p * PAIR_W:NA_W + (p + 1) * PAIR_W] = _pair_attn(q2, [kd[g]], [vd[g]],
                                                                          stack=True).astype(BF16)


def _ctx_attn(z_att, qn2, kn2, seq_len):
    n = z_att.shape[0]
    return pl.pallas_call(
        _ctx_attn_kernel,
        grid=(n // seq_len,),
        in_specs=[
            pl.BlockSpec((seq_len, ATT_W), lambda b: (b, 0)),
            pl.BlockSpec((1, PAIR_W), lambda b: (0, 0)),
            pl.BlockSpec((1, PAIR_W), lambda b: (0, 0)),
        ],
        out_specs=pl.BlockSpec((seq_len, NA_W + GQA_QW), lambda b: (b, 0)),
        out_shape=jax.ShapeDtypeStruct((n, NA_W + GQA_QW), BF16),
        compiler_params=_cparams(("arbitrary",)),
        name="ctx_attn",
    )(z_att, qn2, kn2)


def _na_row_start(r, rows):
    return jnp.clip(r - NA_WIN_R // 2, 0, rows - NA_WIN_R)


def _lat_na_kernel(q_ref, k_ref, v_ref, kc_ref, vc_ref, tb_ref, o_ref, *, rows, g_rows):
    r0 = pl.program_id(1) * g_rows
    wr = NA_WIN_R
    m2 = 2 * GRID_W
    lo = _lane_lo((GRID_W, PAIR_W))
    for p in range(NA_HEADS // 2):
        ps = slice(p * PAIR_W, (p + 1) * PAIR_W)
        kc2 = kc_ref[0, :, ps]
        vc2 = vc_ref[0, :, ps]
        qs = []
        for a in range(g_rows):
            q2 = q_ref[a * GRID_W:(a + 1) * GRID_W, ps].astype(F32) * Q_SCALE
            qs += [jnp.where(lo, q2, 0.0).astype(BF16), jnp.where(lo, 0.0, q2).astype(BF16)]
        qs_all = jnp.concatenate(qs, axis=0)
        s_ctx_all = _dot_nt(qs_all, kc2)
        o_loc, p_ctx, l_sum = [], [], []
        for a in range(g_rows):
            r = r0 + a
            rs = _na_row_start(r, rows)
            start = pl.multiple_of(rs * GRID_W, GRID_W)
            k2 = k_ref[pl.ds(start, wr * GRID_W), ps]
            v2 = v_ref[pl.ds(start, wr * GRID_W), ps]
            bias = tb_ref[rs - r + (wr - 1), 2 * p:2 * p + 2].reshape(m2, wr * GRID_W)
            s_loc = _dot_nt(qs_all[a * m2:(a + 1) * m2], k2) + bias
            s_ctx = s_ctx_all[a * m2:(a + 1) * m2]
            m = jnp.maximum(s_loc.max(axis=-1, keepdims=True), s_ctx.max(axis=-1, keepdims=True))
            e_loc = jnp.exp2(s_loc - m)
            e_ctx = jnp.exp2(s_ctx - m)
            l_sum.append(e_loc.sum(axis=-1, keepdims=True) + e_ctx.sum(axis=-1, keepdims=True))
            o_loc.append(jnp.dot(e_loc.astype(BF16), v2, preferred_element_type=F32))
            p_ctx.append(e_ctx.astype(BF16))
        o_ctx_all = jnp.dot(jnp.concatenate(p_ctx, axis=0), vc2, preferred_element_type=F32)
        for a in range(g_rows):
            o = (o_loc[a] + o_ctx_all[a * m2:(a + 1) * m2]) / l_sum[a]
            o_ref[a * GRID_W:(a + 1) * GRID_W, ps] = jnp.where(lo, o[:GRID_W], o[GRID_W:]).astype(BF16)


def _na_bias_tables(rpb):
    wr = NA_WIN_R
    nrel = 2 * NA_WIN_C - 1
    cols = np.arange(GRID_W)
    col_start = np.clip(cols - NA_WIN_C // 2, 0, GRID_W - NA_WIN_C)
    ck = np.arange(GRID_W)[None, :]
    valid = (ck >= col_start[:, None]) & (ck < col_start[:, None] + NA_WIN_C)
    rel = ck - cols[:, None] + (NA_WIN_C - 1)
    onehot = (np.arange(nrel)[:, None, None] == rel[None]) & valid[None]
    onehot = jnp.asarray(onehot.reshape(nrel, GRID_W * GRID_W), F32)
    d = np.arange(wr)[:, None] + np.arange(wr)[None, :]
    t = jnp.einsum("lhdir,rq->lhdiq", rpb[:, :, d] * LOG2E, onehot, precision=HIGHEST)
    t = t.reshape(rpb.shape[0], NA_HEADS, wr, wr, GRID_W, GRID_W)
    t = jnp.where(valid[None, None, None, None], t, NEG_BIG)
    t = t.transpose(0, 2, 1, 4, 3, 5)
    return t.reshape(rpb.shape[0], wr, NA_HEADS, GRID_W, wr * GRID_W)


def _lat_na(z_att, kc, vc, tb, seq_len):
    n = z_att.shape[0]
    nb = n // seq_len
    rows = seq_len // GRID_W
    assert rows >= NA_WIN_R
    wr = NA_WIN_R
    p = kc.shape[1]
    g_rows = NA_ROWS_PER_STEP
    blk = rows // g_rows
    return pl.pallas_call(
        functools.partial(_lat_na_kernel, rows=rows, g_rows=g_rows),
        grid=(nb, blk),
        in_specs=[
            pl.BlockSpec((g_rows * GRID_W, NA_W), lambda b, r: (b * blk + r, 0)),
            pl.BlockSpec((seq_len, NA_W), lambda b, r: (b, 1)),
            pl.BlockSpec((seq_len, NA_W), lambda b, r: (b, 2)),
            pl.BlockSpec((1, p, NA_W), lambda b, r: (b, 0, 0)),
            pl.BlockSpec((1, p, NA_W), lambda b, r: (b, 0, 0)),
            pl.BlockSpec((wr, NA_HEADS, GRID_W, wr * GRID_W), lambda b, r: (0, 0, 0, 0),
                         pipeline_mode=pl.Buffered(1)),
        ],
        out_specs=pl.BlockSpec((g_rows * GRID_W, NA_W), lambda b, r: (b * blk + r, 0)),
        out_shape=jax.ShapeDtypeStruct((n, NA_W), BF16),
        compiler_params=_cparams(("arbitrary", "arbitrary")),
        name="lat_na",
    )(z_att, z_att, z_att, kc, vc, tb)


def _lat_gqa_kernel(q_ref, k_ref, v_ref, kc_ref, vc_ref, o_ref):
    kn2 = k_ref[...].astype(F32)
    v2 = v_ref[...].astype(F32)
    kc2 = kc_ref[0].astype(F32)
    vc2 = vc_ref[0].astype(F32)
    def with_ones(x2, g):
        d = _dup_head(x2, g)
        lo = _lane_lo(d.shape)
        return jnp.where(lo, d, 1.0).astype(BF16), jnp.where(lo, 1.0, d).astype(BF16)

    kd, kcd, v_lo, v_hi = [], [], [], []
    for g in range(GQA_KV_HEADS):
        kd.append(_dup_head(kn2, g).astype(BF16))
        kcd.append(_dup_head(kc2, g).astype(BF16))
        (a0, a1), (b0, b1) = with_ones(v2, g), with_ones(vc2, g)
        v_lo.append([a0, b0])
        v_hi.append([a1, b1])
    for p in range(GQA_HEADS // 2):
        g = (2 * p) // GQA_REP
        ps = slice(p * PAIR_W, (p + 1) * PAIR_W)
        q2 = q_ref[:, ps].astype(F32)
        o_ref[:, ps] = _pair_attn(q2, [kd[g], kcd[g]], v_lo[g], values_hi=v_hi[g]).astype(BF16)


def _rope_tables(seq_len):
    t = jnp.arange(seq_len)
    n_freq = HEAD_DIM // 4
    inv = ROPE_THETA ** (-jnp.arange(n_freq, dtype=F32) / n_freq)
    ang_r = (t // GRID_W).astype(F32)[:, None] * inv
    ang_c = (t % GRID_W).astype(F32)[:, None] * inv
    cr, sr, cc, sc = jnp.cos(ang_r), jnp.sin(ang_r), jnp.cos(ang_c), jnp.sin(ang_c)
    cos = jnp.concatenate([cr, cr, cc, cc] * 2, axis=-1)
    sin = jnp.concatenate([-sr, sr, -sc, sc] * 2, axis=-1)
    return cos, sin


def _lat_gqa(z_att, kc, vc, seq_len):
    n = z_att.shape[0]
    nb = n // seq_len
    tq = min(GQA_QBLOCK, seq_len)
    nqb = seq_len // tq
    p = kc.shape[1]
    qcol = (3 * NA_W) // GQA_QW
    kcol = (3 * NA_W + GQA_QW) // GQA_KW
    return pl.pallas_call(
        _lat_gqa_kernel,
        grid=(nb, nqb),
        in_specs=[
            pl.BlockSpec((tq, GQA_QW), lambda b, i: (b * nqb + i, qcol)),
            pl.BlockSpec((seq_len, GQA_KW), lambda b, i: (b, kcol)),
            pl.BlockSpec((seq_len, GQA_KW), lambda b, i: (b, kcol + 1)),
            pl.BlockSpec((1, p, GQA_KW), lambda b, i: (b, 0, 0)),
            pl.BlockSpec((1, p, GQA_KW), lambda b, i: (b, 0, 0)),
        ],
        out_specs=pl.BlockSpec((tq, GQA_QW), lambda b, i: (b * nqb + i, 0)),
        out_shape=jax.ShapeDtypeStruct((n, GQA_QW), BF16),
        compiler_params=_cparams(("arbitrary", "arbitrary")),
        name="lat_gqa",
    )(z_att, z_att, z_att, kc, vc)


def _layer_norm(x, g, b):
    mu = jnp.mean(x, axis=-1, keepdims=True)
    xc = x - mu
    var = jnp.mean(xc * xc, axis=-1, keepdims=True)
    return xc * lax.rsqrt(var + LN_EPS) * g + b


def _merge_kernel(x_ref, mod_ref, cp_ref, cpp_ref, cpn_ref, *rest, seq_len, n_att):
    att_refs = rest[:n_att]
    (wg_ref, wb_ref, wo_ref, pw_ref, cw_ref, cb_ref, ps_ref, lg_ref, lb_ref, wr_ref,
     x1_ref, xm2_ref, aff_ref, ucv_ref, upl_ref) = rest[n_att:]
    t = x_ref.shape[0]
    halo = POOL_HALO
    i = pl.program_id(0)
    pos0 = (i * t) % seq_len
    has_prev = pos0 > 0
    has_next = pos0 + t < seq_len

    shift1, scale1, gate1 = mod_ref[0, 0:1, :], mod_ref[0, 1:2, :], mod_ref[0, 2:3, :]
    shift2, scale2 = mod_ref[0, 3:4, :], mod_ref[0, 4:5, :]

    bw = BRANCH_W
    ucv_ref[halo:halo + t, :] = cp_ref[:, bw:2 * bw] * cp_ref[:, 2 * bw:3 * bw]
    upl_ref[halo:halo + t, :] = cp_ref[:, 3 * bw:4 * bw]
    ucv_ref[0:halo, :] = jnp.where(has_prev, cpp_ref[:, bw:2 * bw] * cpp_ref[:, 2 * bw:3 * bw], 0.0)
    upl_ref[0:halo, :] = jnp.where(has_prev, cpp_ref[:, 3 * bw:4 * bw], 0.0)
    ucv_ref[halo + t:, :] = jnp.where(has_next, cpn_ref[:, bw:2 * bw] * cpn_ref[:, 2 * bw:3 * bw], 0.0)
    upl_ref[halo + t:, :] = jnp.where(has_next, cpn_ref[:, 3 * bw:4 * bw], 0.0)

    wr = wr_ref[...]
    wh = wr.astype(BF16)
    wl = (wr - wh.astype(F32)).astype(BF16)

    th = min(t, MERGE_SUB)
    blocks = [dict(r0=r0, rows=slice(r0, r0 + th)) for r0 in range(0, t, th)]

    def prologue(b):
        r0, rows = b["r0"], b["rows"]
        x = x_ref[rows, :]
        b["x"] = x
        b["xm"] = (x * (1.0 + scale1) + shift1).astype(BF16)
        conv = (cw_ref[0:1, :] * ucv_ref[halo - 1 + r0:halo - 1 + r0 + th, :]
                + cw_ref[1:2, :] * ucv_ref[halo + r0:halo + r0 + th, :]
                + cw_ref[2:3, :] * ucv_ref[halo + 1 + r0:halo + 1 + r0 + th, :]) + cb_ref[...]
        o_conv = (cp_ref[rows, 0:bw] * conv).astype(BF16)
        pos = pos0 + r0 + lax.broadcasted_iota(I32, (th, 1), 0)
        mixed = []
        for g, win in enumerate(POOL_WINDOWS):
            gs = slice(g * POOL_GW, (g + 1) * POOL_GW)
            acc = None
            for dlt in range(-(win // 2), win // 2):
                term = upl_ref[halo + r0 + dlt:halo + r0 + dlt + th, gs]
                acc = term if acc is None else acc + term
            lo = jnp.maximum(pos - win // 2, 0)
            hi = jnp.minimum(pos + win // 2, seq_len)
            cnt = (hi - lo).astype(F32)
            pooled = acc / cnt - upl_ref[halo + r0:halo + r0 + th, gs]
            mixed.append(jnp.dot(pooled.astype(BF16), pw_ref[g], preferred_element_type=F32))
        o_pool = (jnp.concatenate(mixed, axis=-1) * ps_ref[...]).astype(BF16)
        if n_att == 1:
            b["branches"] = [att_refs[0][rows, 0:bw], att_refs[0][rows, bw:2 * bw], o_conv, o_pool]
        else:
            b["branches"] = [att_refs[0][rows, :], att_refs[1][rows, :], o_conv, o_pool]
        b["merged"] = None

    def dots(b, nb):
        b["zg"] = jnp.dot(b["xm"], wg_ref[:, nb * D_MODEL:(nb + 1) * D_MODEL], preferred_element_type=F32)
        b["proj"] = jnp.dot(b["branches"][nb], wb_ref[nb], preferred_element_type=F32)

    def gate(b):
        term = _sigmoid(b["zg"]) * b["proj"]
        b["merged"] = term if b["merged"] is None else b["merged"] + term

    def out_proj(b):
        b["y"] = jnp.dot(b["merged"].astype(BF16), wo_ref[...], preferred_element_type=F32)

    def tail(b):
        rows = b["rows"]
        x1 = _layer_norm(DEEPNORM_ALPHA * b["x"] + gate1 * b["y"], lg_ref[...], lb_ref[...])
        x1_ref[rows, :] = x1
        xm2 = x1 * (1.0 + scale2) + shift2
        xm2_ref[rows, :] = xm2
        xh = xm2.astype(BF16)
        xl = (xm2 - xh.astype(F32)).astype(BF16)
        logits = (jnp.dot(xh, wh, preferred_element_type=F32) + jnp.dot(xl, wh, preferred_element_type=F32)
                  + jnp.dot(xh, wl, preferred_element_type=F32))
        m = logits.max(axis=-1, keepdims=True)
        e = jnp.exp(logits - m)
        aff_ref[rows, :] = e / e.sum(axis=-1, keepdims=True)

    for b in blocks:
        prologue(b)
        for nb in range(N_BRANCH):
            dots(b, nb)
            gate(b)
        out_proj(b)
        tail(b)


def _merge(x, mod, z_cp, atts, w_gates, w_branch, w_out, pool_w, conv_w, conv_b, pool_scale, ln_g, ln_b,
           w_router, seq_len):
    n = x.shape[0]
    t = min(MERGE_TILE, seq_len)
    per = n // mod.shape[0] // t
    hb = t // POOL_HALO
    nhb = n // POOL_HALO
    row = lambda a: a.reshape(1, -1)
    const2 = lambda i: (0, 0)
    const3 = lambda i: (0, 0, 0)
    once = lambda shp, imap: pl.BlockSpec(shp, imap, pipeline_mode=pl.Buffered(1))
    att_specs = [pl.BlockSpec((t, a.shape[1]), lambda i: (i, 0)) for a in atts]
    return pl.pallas_call(
        functools.partial(_merge_kernel, seq_len=seq_len, n_att=len(atts)),
        grid=(n // t,),
        in_specs=[
            pl.BlockSpec((t, D_MODEL), lambda i: (i, 0)),
            pl.BlockSpec((1, N_MOD, D_MODEL), lambda i: (i // per, 0, 0)),
            pl.BlockSpec((t, CP_W), lambda i: (i, 0)),
            pl.BlockSpec((POOL_HALO, CP_W), lambda i: (jnp.maximum(i * hb - 1, 0), 0)),
            pl.BlockSpec((POOL_HALO, CP_W), lambda i: (jnp.minimum((i + 1) * hb, nhb - 1), 0)),
            *att_specs,
            once((D_MODEL, GATE_W), const2),
            once((N_BRANCH, BRANCH_W, D_MODEL), const3),
            once((D_MODEL, D_MODEL), const2),
            pl.BlockSpec((len(POOL_WINDOWS), POOL_GW, POOL_GW), const3),
            pl.BlockSpec((CONV_K, BRANCH_W), const2),
            pl.BlockSpec((1, BRANCH_W), const2),
            pl.BlockSpec((1, BRANCH_W), const2),
            pl.BlockSpec((1, D_MODEL), const2),
            pl.BlockSpec((1, D_MODEL), const2),
            pl.BlockSpec((D_MODEL, N_EXPERTS), const2),
        ],
        out_specs=[
            pl.BlockSpec((t, D_MODEL), lambda i: (i, 0)),
            pl.BlockSpec((t, D_MODEL), lambda i: (i, 0)),
            pl.BlockSpec((t, N_EXPERTS), lambda i: (i, 0)),
        ],
        out_shape=[jax.ShapeDtypeStruct((n, D_MODEL), F32), jax.ShapeDtypeStruct((n, D_MODEL), F32),
                   jax.ShapeDtypeStruct((n, N_EXPERTS), F32)],
        scratch_shapes=[pltpu.VMEM((t + 2 * POOL_HALO, BRANCH_W), F32),
                        pltpu.VMEM((t + 2 * POOL_HALO, BRANCH_W), F32)],
        compiler_params=_cparams(("arbitrary",)),
        name="merge",
    )(x, mod, z_cp, z_cp, z_cp, *atts, w_gates, w_branch, w_out, pool_w, conv_w, row(conv_b), row(pool_scale),
      row(ln_g), row(ln_b), w_router)


def _topk_kernel(aff_ref, idx_ref, dst_ref, gp_ref, cnt_ref, thr_ref, rank_ref, *, cap, plane_stride):
    ne, nc, _ = aff_ref.shape
    capf = float(cap)
    bits_all = pltpu.bitcast(aff_ref[...], I32)

    def search(i, cur):
        cand = cur | (jnp.int32(1) << (30 - i))
        cnt = jnp.sum((bits_all >= cand).astype(F32), axis=(1, 2), keepdims=True)
        return jnp.where(cnt >= capf, cand, cur)

    thr = lax.fori_loop(0, 31, search, jnp.zeros((ne, 1, 1), I32))
    thr_ref[...] = jnp.broadcast_to(thr, thr_ref.shape)

    jj = lax.broadcasted_iota(I32, (CHUNK, CHUNK), 0)
    kk = lax.broadcasted_iota(I32, (CHUNK, CHUNK), 1)
    ut_incl = (jj <= kk).astype(BF16)
    lt_incl = (kk <= jj).astype(BF16)
    cc = lax.broadcasted_iota(I32, (nc, nc), 0)
    dd = lax.broadcasted_iota(I32, (nc, nc), 1)
    cl_excl = (dd < cc).astype(BF16)
    cl_incl = (dd <= cc).astype(BF16)
    eye = (jj == kk).astype(BF16)
    wide = 4 if cap % (4 * CHUNK) == 0 else 1
    lane_w = lax.broadcasted_iota(I32, (1, wide * CHUNK), 1)
    sub_cw = lax.broadcasted_iota(I32, (nc, wide * CHUNK), 0).astype(F32)
    sub_jw = lax.broadcasted_iota(I32, (CHUNK, wide * CHUNK), 0).astype(F32)
    rank_ref[...] = jnp.zeros(rank_ref.shape, F32)
    gp_ref[...] = jnp.full(gp_ref.shape, -1.0, F32)

    def per_expert(e, carry):
        a = aff_ref[e]
        b = pltpu.bitcast(a, I32)
        t = thr_ref[e][0:1, :]
        gt = b > t
        eq = b == t
        need = capf - jnp.sum(gt.astype(F32), keepdims=True)
        eqf = eq.astype(F32)
        incl_eq = jnp.dot(eqf.astype(BF16), ut_incl, preferred_element_type=F32)
        tot_eq = jnp.broadcast_to(incl_eq[:, CHUNK - 1:CHUNK], (nc, CHUNK))
        cum_eq = jnp.dot(cl_excl, tot_eq.astype(BF16), preferred_element_type=F32)
        rank = cum_eq + incl_eq - eqf
        sel = jnp.logical_or(gt, jnp.logical_and(eq, rank < need))
        self_ = sel.astype(F32)

        plane = rank_ref[...]
        rank_ref[...] = plane + self_
        for k in range(ne):
            gp_ref[k] = jnp.where(jnp.logical_and(sel, plane == float(k)), a, gp_ref[k])
        plane_tb = _dot_nt(eye, plane.astype(BF16)).astype(BF16)

        selb = self_.astype(BF16)
        incl = jnp.dot(selb, ut_incl, preferred_element_type=F32)
        tot = jnp.broadcast_to(incl[:, CHUNK - 1:CHUNK], (nc, CHUNK))
        cum_incl = jnp.dot(cl_incl, tot.astype(BF16), preferred_element_type=F32)
        incl_tb = _dot_nt(lt_incl, selb).astype(BF16)

        def per_block(sb, c2):
            s_row = (sb * (wide * CHUNK) + lane_w).astype(F32)
            m = cum_w <= s_row
            c_of_s = jnp.sum(m.astype(F32), axis=0, keepdims=True)
            excl_s = jnp.max(jnp.where(m, cum_w, 0.0), axis=0, keepdims=True)
            onehot_t = (sub_cw == c_of_s).astype(BF16)
            rows_t = jnp.dot(incl_tb, onehot_t, preferred_element_type=F32)
            s_local = s_row - excl_s
            t_local = jnp.sum((rows_t <= s_local).astype(F32), axis=0, keepdims=True)
            tok = (c_of_s * float(CHUNK) + t_local).astype(I32)
            plane_rows = jnp.dot(plane_tb, onehot_t, preferred_element_type=F32)
            plane_s = jnp.sum(jnp.where(sub_jw == t_local, plane_rows, 0.0), axis=0, keepdims=True)
            dst = plane_s.astype(I32) * plane_stride + tok
            for u in range(wide):
                idx_ref[e, pl.ds(sb * wide + u, 1), :] = tok[:, u * CHUNK:(u + 1) * CHUNK]
                dst_ref[e, pl.ds(sb * wide + u, 1), :] = dst[:, u * CHUNK:(u + 1) * CHUNK]
            return c2

        cum_w = jnp.concatenate([cum_incl] * wide, axis=1)
        lax.fori_loop(0, cap // (wide * CHUNK), per_block, 0)
        return carry

    lax.fori_loop(0, ne, per_expert, 0)
    cnt_ref[...] = jnp.broadcast_to(jnp.max(rank_ref[...], axis=1, keepdims=True), cnt_ref.shape).astype(I32)


def _topk(aff3, cap, plane_stride):
    ne, nc, _ = aff3.shape
    full3 = lambda shp: pl.BlockSpec(shp, lambda i: (0, 0, 0))
    return pl.pallas_call(
        functools.partial(_topk_kernel, cap=cap, plane_stride=plane_stride),
        grid=(1,),
        in_specs=[full3((ne, nc, CHUNK))],
        out_specs=[full3((ne, cap // CHUNK, CHUNK)), full3((ne, cap // CHUNK, CHUNK)), full3((ne, nc, CHUNK)),
                   pl.BlockSpec((nc, CHUNK), lambda i: (0, 0))],
        out_shape=[jax.ShapeDtypeStruct((ne, cap // CHUNK, CHUNK), I32),
                   jax.ShapeDtypeStruct((ne, cap // CHUNK, CHUNK), I32),
                   jax.ShapeDtypeStruct((ne, nc, CHUNK), F32),
                   jax.ShapeDtypeStruct((nc, CHUNK), I32)],
        scratch_shapes=[pltpu.VMEM((ne, SUBLANES, CHUNK), I32), pltpu.VMEM((nc, CHUNK), F32)],
        compiler_params=_cparams(("arbitrary",)),
        name="topk",
    )(aff3)


def _ffn_kernel(idx_ref, idxn_ref, dst_ref, x_hbm, wg_ref, wu_ref, wd_ref, y_in_hbm, y_hbm, xs_ref, ys_ref, wgu_ref,
                wdn_ref, gsem_ref, ssem_ref, *, steps):
    tm = xs_ref.shape[1]
    s = pl.program_id(0) * pl.num_programs(1) + pl.program_id(1)

    @pl.when(pl.program_id(1) == 0)
    def _():
        wgu_ref[0] = wg_ref[0].astype(BF16)
        wgu_ref[1] = wu_ref[0].astype(BF16)
        wdn_ref[...] = wd_ref[0].astype(BF16)

    def gather_rows(idx_smem, buf_slot):
        for j in range(tm):
            pltpu.make_async_copy(x_hbm.at[pl.ds(idx_smem[0, 0, j], 1), :], xs_ref.at[buf_slot, pl.ds(j, 1), :],
                                  gsem_ref.at[buf_slot]).start()

    def gathered(buf_slot):
        return pltpu.make_async_copy(x_hbm.at[pl.ds(0, tm), :], xs_ref.at[buf_slot], gsem_ref.at[buf_slot])

    def scattered(buf_slot):
        return pltpu.make_async_copy(ys_ref.at[buf_slot], y_hbm.at[pl.ds(0, tm), :], ssem_ref.at[buf_slot])

    @pl.when(s == 0)
    def _():
        gather_rows(idx_ref, 0)

    def step(slot):
        @pl.when(s + 1 < steps)
        def _():
            gather_rows(idxn_ref, 1 - slot)

        gathered(slot).wait()
        xs = xs_ref[slot].astype(BF16)
        hg = jnp.dot(xs, wgu_ref[0], preferred_element_type=F32)
        hu = jnp.dot(xs, wgu_ref[1], preferred_element_type=F32)
        hdn = (hg * _sigmoid(hg) * hu).astype(BF16)
        ye = jnp.dot(hdn, wdn_ref[...], preferred_element_type=F32)

        @pl.when(s >= 2)
        def _():
            scattered(slot).wait()

        ys_ref[slot] = _pack_bf16_pair(ye[:, :PACK_W], ye[:, PACK_W:])
        for j in range(tm):
            pltpu.make_async_copy(ys_ref.at[slot, pl.ds(j, 1), :], y_hbm.at[pl.ds(dst_ref[0, 0, j], 1), :],
                                  ssem_ref.at[slot]).start(priority=j % 2)

        @pl.when(s == steps - 1)
        def _():
            scattered(slot).wait()
            if steps > 1:
                scattered(1 - slot).wait()

    for parity in range(2):
        pl.when(s % 2 == parity)(functools.partial(step, parity))


def _ffn(idx_tiles, dst_tiles, xm2, w_gate, w_up, w_down, layer, cap, ybuf):
    tm = idx_tiles.shape[2]
    nt = cap // tm
    steps = N_EXPERTS * nt
    wspec = lambda: pl.BlockSpec((None, 1, D_MODEL, EXPERT_FF), lambda e, i: (layer, e, 0, 0))
    return pl.pallas_call(
        functools.partial(_ffn_kernel, steps=steps),
        grid=(N_EXPERTS, nt),
        in_specs=[
            pl.BlockSpec((1, 1, tm), lambda e, i: (e * nt + i, 0, 0), memory_space=pltpu.SMEM),
            pl.BlockSpec((1, 1, tm), lambda e, i: (jnp.minimum(e * nt + i + 1, steps - 1), 0, 0),
                         memory_space=pltpu.SMEM),
            pl.BlockSpec((1, 1, tm), lambda e, i: (e * nt + i, 0, 0), memory_space=pltpu.SMEM),
            pl.BlockSpec(memory_space=pl.ANY),
            wspec(), wspec(),
            pl.BlockSpec((None, 1, EXPERT_FF, D_MODEL), lambda e, i: (layer, e, 0, 0)),
            pl.BlockSpec(memory_space=pl.ANY),
        ],
        out_specs=pl.BlockSpec(memory_space=pl.ANY),
        out_shape=jax.ShapeDtypeStruct(ybuf.shape, U32),
        input_output_aliases={7: 0},
        scratch_shapes=[pltpu.VMEM((2, tm, D_MODEL), F32), pltpu.VMEM((2, tm, PACK_W), U32),
                        pltpu.VMEM((2, D_MODEL, EXPERT_FF), BF16), pltpu.VMEM((EXPERT_FF, D_MODEL), BF16),
                        pltpu.SemaphoreType.DMA((2,)), pltpu.SemaphoreType.DMA((2,))],
        compiler_params=_cparams(("arbitrary", "arbitrary")),
        name="ffn",
    )(idx_tiles, idx_tiles, dst_tiles, xm2, w_gate, w_up, w_down, ybuf)


def _combine_kernel(kmax_ref, y_hbm, x1_ref, mod_ref, gp_ref, lg_ref, lb_ref, o_ref, planes_ref, acc_ref, sem_ref,
                    *, n_planes):
    t = x1_ref.shape[0]
    i = pl.program_id(0)
    ntile = pl.num_programs(0)
    slot = i % 2

    def for_planes(tile, buf_slot, fn):
        for k in range(n_planes):
            @pl.when(k < kmax_ref[tile])
            def _(k=k):
                fn(pltpu.make_async_copy(y_hbm.at[k, pl.ds(tile * t, t), :], planes_ref.at[buf_slot, k],
                                         sem_ref.at[buf_slot]))

    @pl.when(i == 0)
    def _():
        for_planes(0, 0, lambda cp: cp.start())

    @pl.when(i + 1 < ntile)
    def _():
        for_planes(i + 1, 1 - slot, lambda cp: cp.start())

    for_planes(i, slot, lambda cp: cp.wait())

    acc_ref[...] = jnp.zeros(acc_ref.shape, F32)
    for k in range(n_planes):
        @pl.when(k < kmax_ref[i])
        def _(k=k):
            w = gp_ref[:, k:k + 1]
            lo, hi = _unpack_bf16_pair(planes_ref[slot, k])
            acc_ref[:, :PACK_W] += jnp.where(w >= 0.0, w * lo, 0.0)
            acc_ref[:, PACK_W:] += jnp.where(w >= 0.0, w * hi, 0.0)

    gate2 = mod_ref[0, 5:6, :]
    o_ref[...] = _layer_norm(DEEPNORM_ALPHA * x1_ref[...] + gate2 * acc_ref[...], lg_ref[...], lb_ref[...])


def _combine(kmax, y_planes, x1, mod, gp_tok, ln_g, ln_b, seq_len):
    n = x1.shape[0]
    n_planes = y_planes.shape[0]
    t = min(TOK_TILE, n // mod.shape[0])
    per = n // mod.shape[0] // t
    row = lambda a: a.reshape(1, -1)
    grid_spec = pltpu.PrefetchScalarGridSpec(
        num_scalar_prefetch=1,
        grid=(n // t,),
        in_specs=[
            pl.BlockSpec(memory_space=pl.ANY),
            pl.BlockSpec((t, D_MODEL), lambda i, *_: (i, 0)),
            pl.BlockSpec((1, N_MOD, D_MODEL), lambda i, *_: (i // per, 0, 0)),
            pl.BlockSpec((t, n_planes), lambda i, *_: (i, 0)),
            pl.BlockSpec((1, D_MODEL), lambda i, *_: (0, 0)),
            pl.BlockSpec((1, D_MODEL), lambda i, *_: (0, 0)),
        ],
        out_specs=pl.BlockSpec((t, D_MODEL), lambda i, *_: (i, 0)),
        scratch_shapes=[pltpu.VMEM((2, n_planes, t, PACK_W), U32), pltpu.VMEM((t, D_MODEL), F32),
                        pltpu.SemaphoreType.DMA((2,))],
    )
    return pl.pallas_call(
        functools.partial(_combine_kernel, n_planes=n_planes),
        grid_spec=grid_spec,
        out_shape=jax.ShapeDtypeStruct((n, D_MODEL), F32),
        compiler_params=_cparams(("arbitrary",)),
        name="combine",
    )(kmax, y_planes, x1, mod, gp_tok, row(ln_g), row(ln_b))


def _moe(x1, xm2, aff, mod, w_gate, w_up, w_down, layer, ln_g, ln_b, seq_len, ybuf):
    n = x1.shape[0]
    cap = EC_CAPACITY * n // N_EXPERTS
    nc = n // CHUNK
    t = min(TOK_TILE, n // mod.shape[0])
    stride = ybuf.shape[0] // N_EXPERTS
    aff3 = aff.T.reshape(N_EXPERTS, nc, CHUNK)
    idx3, dst3, gp3, cnt = _topk(aff3, cap, stride)
    gp_tok = gp3.reshape(N_EXPERTS, n).T
    kmax = jnp.max(cnt[:, 0].reshape(n // t, t // CHUNK), axis=1)
    tm = min(FFN_TILE, cap)
    idx_tiles = idx3.reshape(N_EXPERTS * cap // tm, 1, tm)
    dst_tiles = dst3.reshape(N_EXPERTS * cap // tm, 1, tm)
    ybuf = _ffn(idx_tiles, dst_tiles, xm2, w_gate, w_up, w_down, layer, cap, ybuf)
    y = _combine(kmax, ybuf.reshape(N_EXPERTS, stride, PACK_W), x1, mod, gp_tok, ln_g, ln_b, seq_len)
    return y, ybuf


def kernel(x_prompt, x_sample, cache_na_k, cache_na_v, cache_gqa_k, cache_gqa_v, c, c_ctx, w_ada, b_ada, w_in,
           na_rpb, qn_g, kn_g, conv_w, conv_b, pool_w, pool_scale, w_branch, w_out, ln1_g, ln1_b, ln2_g, ln2_b,
           w_router, w_gate, w_up, w_down):
    depth = w_in.shape[0]
    bp, lp, _ = x_prompt.shape
    bs, ls, _ = x_sample.shape
    past = cache_na_k.shape[2]

    mods = _adaln(jnp.concatenate([c_ctx[None, :], c], axis=0), w_ada, b_ada)
    mods = mods.reshape(depth, 1 + bs, N_MOD, D_MODEL)
    cos, sin = _rope_tables(ls)
    tbs = _na_bias_tables(na_rpb)

    yp = x_prompt.reshape(bp * lp, D_MODEL)
    ys = x_sample.reshape(bs * ls, D_MODEL)
    ybuf_p = jnp.zeros((N_EXPERTS * bp * lp, PACK_W), U32)
    ybuf_s = jnp.zeros((N_EXPERTS * bs * ls, PACK_W), U32)
    na_k_l, na_v_l, g_k_l, g_v_l = [], [], [], []
    for l in range(depth):
        mod_c = mods[l, 0:1]
        mod_s = mods[l, 1:]
        w_a = w_in[l, :, :ATT_W + CP_W].astype(BF16)
        w_gates = w_in[l, :, ATT_W + CP_W:].astype(BF16)
        w_br = w_branch[l].astype(BF16)
        w_o = w_out[l].astype(BF16)
        p_w = pool_w[l].astype(BF16)
        moe_w = (w_gate, w_up, w_down, l)
        merge_w = (w_gates, w_br, w_o, p_w, conv_w[l], conv_b[l], pool_scale[l], ln1_g[l], ln1_b[l], w_router[l])

        qn2 = jnp.tile(qn_g[l], 2).reshape(1, PAIR_W)
        kn2 = jnp.tile(kn_g[l], 2).reshape(1, PAIR_W)
        cache = lambda a, w: a[:, l].reshape(bs, past, w).astype(BF16)

        z_att, z_cp, na_k, na_v, g_k, g_v = _inproj(yp, mod_c, w_a, bp * lp, kn2)
        o_att = _ctx_attn(z_att, qn2, kn2, lp)
        na_k_l.append(na_k.reshape(bp, lp, NA_HEADS, HEAD_DIM))
        na_v_l.append(na_v.reshape(bp, lp, NA_HEADS, HEAD_DIM))
        g_k_l.append(g_k.reshape(bp, lp, GQA_KV_HEADS, HEAD_DIM))
        g_v_l.append(g_v.reshape(bp, lp, GQA_KV_HEADS, HEAD_DIM))
        x1, xm2, aff = _merge(yp, mod_c, z_cp, [o_att], *merge_w, lp)
        yp, ybuf_p = _moe(x1, xm2, aff, mod_c, *moe_w, ln2_g[l], ln2_b[l], lp, ybuf_p)

        z_att, z_cp = _inproj(ys, mod_s, w_a, ls, kn2, qn2, cos, sin)
        o_na = _lat_na(z_att, cache(cache_na_k, NA_W), cache(cache_na_v, NA_W), tbs[l], ls)
        o_gqa = _lat_gqa(z_att, cache(cache_gqa_k, GQA_KW), cache(cache_gqa_v, GQA_KW), ls)
        x1, xm2, aff = _merge(ys, mod_s, z_cp, [o_na, o_gqa], *merge_w, ls)
        ys, ybuf_s = _moe(x1, xm2, aff, mod_s, *moe_w, ln2_g[l], ln2_b[l], ls, ybuf_s)

    return (yp.reshape(bp, lp, D_MODEL), ys.reshape(bs, ls, D_MODEL),
            jnp.stack(na_k_l, axis=1), jnp.stack(na_v_l, axis=1),
            jnp.stack(g_k_l, axis=1), jnp.stack(g_v_l, axis=1))
```

```python
import functools

import numpy as np
import jax
import jax.numpy as jnp
from jax import lax
from jax.experimental import pallas as pl
from jax.experimental.pallas import tpu as pltpu

F32 = jnp.float32
BF16 = jnp.bfloat16
I32 = jnp.int32
U32 = jnp.uint32
HIGHEST = lax.Precision.HIGHEST

D_MODEL = 1024
NOMINAL_DEPTH = 4
GRID_W = 64
HEAD_DIM = 64
NA_HEADS = 8
NA_WIN_R = 8
NA_WIN_C = 16
GQA_HEADS = 8
GQA_KV_HEADS = 2
GQA_REP = GQA_HEADS // GQA_KV_HEADS
BRANCH_W = 512
N_BRANCH = 4
CONV_K = 3
POOL_WINDOWS = (2, 4, 8, 16)
POOL_GW = BRANCH_W // len(POOL_WINDOWS)
POOL_HALO = max(POOL_WINDOWS) // 2
N_EXPERTS = 16
EXPERT_FF = 1024
EC_CAPACITY = 2
ROPE_THETA = 10000.0
LN_EPS = 1e-6
RMS_EPS = 1e-6
N_MOD = 6
DEEPNORM_ALPHA = (2 * NOMINAL_DEPTH) ** 0.25
NA_W = NA_HEADS * HEAD_DIM
GQA_QW = GQA_HEADS * HEAD_DIM
GQA_KW = GQA_KV_HEADS * HEAD_DIM
ATT_W = 3 * NA_W + GQA_QW + 2 * GQA_KW
CP_W = 4 * BRANCH_W
GATE_W = N_BRANCH * D_MODEL
LOG2E = 1.4426950408889634
Q_SCALE = HEAD_DIM ** -0.5 * LOG2E
NEG_BIG = -1e30
PAIR_W = 2 * HEAD_DIM
PACK_W = D_MODEL // 2

LANES = 128
SUBLANES = 8
VMEM_LIMIT = 56 * 1024 * 1024

TOK_TILE_IN = 512
MERGE_TILE = 512
MERGE_SUB = 256
TOK_TILE = 512
CHUNK = LANES
FFN_TILE = 512
GQA_QBLOCK = 512
NA_ROWS_PER_STEP = 8


def _cparams(sem):
    return pltpu.CompilerParams(dimension_semantics=sem, vmem_limit_bytes=VMEM_LIMIT)


def _pack_bf16_pair(a, b):
    lo = pltpu.bitcast(a.astype(BF16).astype(F32), U32)
    hi = pltpu.bitcast(b.astype(BF16).astype(F32), U32)
    return (lo >> 16) | (hi & jnp.uint32(0xFFFF0000))


def _unpack_bf16_pair(w):
    return pltpu.bitcast(w << 16, F32), pltpu.bitcast(w & jnp.uint32(0xFFFF0000), F32)


def _sigmoid(x):
    return 0.5 * jnp.tanh(0.5 * x) + 0.5


def _adaln_kernel(cv_ref, w_ref, b_ref, o_ref):
    cv = cv_ref[...]
    s = cv * jax.nn.sigmoid(cv)
    o_ref[0] = jnp.dot(s, w_ref[0], precision=HIGHEST, preferred_element_type=F32) + b_ref[0]


def _adaln(cvecs, w_ada, b_ada):
    depth = w_ada.shape[0]
    r = cvecs.shape[0]
    tn = 1536
    nw = N_MOD * D_MODEL
    return pl.pallas_call(
        _adaln_kernel,
        grid=(depth, nw // tn),
        in_specs=[
            pl.BlockSpec((r, D_MODEL), lambda l, j: (0, 0)),
            pl.BlockSpec((1, D_MODEL, tn), lambda l, j: (l, 0, j)),
            pl.BlockSpec((1, 1, tn), lambda l, j: (l, 0, j)),
        ],
        out_specs=pl.BlockSpec((1, r, tn), lambda l, j: (l, 0, j)),
        out_shape=jax.ShapeDtypeStruct((depth, r, nw), F32),
        compiler_params=_cparams(("arbitrary", "arbitrary")),
        name="adaln",
    )(cvecs, w_ada, b_ada.reshape(depth, 1, nw))


def _lane_lo(shape):
    return (lax.broadcasted_iota(I32, shape, len(shape) - 1) % PAIR_W) < HEAD_DIM


def _pair_rms(x2, g2):
    lo = _lane_lo(x2.shape)
    sq = x2 * x2
    ss_lo = jnp.sum(jnp.where(lo, sq, 0.0), axis=-1, keepdims=True)
    ss_hi = jnp.sum(jnp.where(lo, 0.0, sq), axis=-1, keepdims=True)
    inv = lax.rsqrt(jnp.where(lo, ss_lo, ss_hi) * (1.0 / HEAD_DIM) + RMS_EPS)
    return x2 * inv * g2


def _pair_rope(x2, cos2, sin2):
    q = HEAD_DIM // 4
    first = (lax.broadcasted_iota(I32, x2.shape, 1) % (2 * q)) < q
    swapped = jnp.where(first, pltpu.roll(x2, PAIR_W - q, 1), pltpu.roll(x2, q, 1))
    return x2 * cos2 + swapped * sin2


def _inproj_kernel(x_ref, mod_ref, w_ref, *rest, with_state):
    if with_state:
        kn_ref, att_ref, cp_ref, nak_ref, nav_ref, gk_ref, gv_ref = rest
    else:
        qn_ref, kn_ref, cos_ref, sin_ref, att_ref, cp_ref = rest
    shift = mod_ref[0, 0:1, :]
    scale = mod_ref[0, 1:2, :]
    xm = (x_ref[...] * (1.0 + scale) + shift).astype(BF16)
    z = jnp.dot(xm, w_ref[...], preferred_element_type=F32)
    cp_ref[...] = z[:, ATT_W:]
    qoff = 3 * NA_W
    koff = qoff + GQA_QW
    if with_state:
        att_ref[...] = z[:, :ATT_W].astype(BF16)
        nak_ref[...] = z[:, NA_W:2 * NA_W]
        nav_ref[...] = z[:, 2 * NA_W:3 * NA_W]
        gk_ref[...] = _pair_rms(z[:, koff:koff + GQA_KW], kn_ref[...])
        gv_ref[...] = z[:, koff + GQA_KW:koff + 2 * GQA_KW]
    else:
        att_ref[:, :qoff] = z[:, :qoff].astype(BF16)
        for p in range(GQA_HEADS // 2):
            ps = slice(qoff + p * PAIR_W, qoff + (p + 1) * PAIR_W)
            q2 = _pair_rope(_pair_rms(z[:, ps], qn_ref[...]), cos_ref[...], sin_ref[...]) * Q_SCALE
            att_ref[:, ps] = q2.astype(BF16)
        k2 = _pair_rope(_pair_rms(z[:, koff:koff + GQA_KW], kn_ref[...]), cos_ref[...], sin_ref[...])
        att_ref[:, koff:koff + GQA_KW] = k2.astype(BF16)
        att_ref[:, koff + GQA_KW:] = z[:, koff + GQA_KW:ATT_W].astype(BF16)


def _inproj(x, mod, w_a, seq_len, kn2, qn2=None, cos=None, sin=None):
    n = x.shape[0]
    t = min(TOK_TILE_IN, seq_len)
    per = n // mod.shape[0] // t
    with_state = qn2 is None
    tile = lambda w: pl.BlockSpec((t, w), lambda i: (i, 0))
    vec = pl.BlockSpec((1, PAIR_W), lambda i: (0, 0))
    in_specs = [
        pl.BlockSpec((t, D_MODEL), lambda i: (i, 0)),
        pl.BlockSpec((1, N_MOD, D_MODEL), lambda i: (i // per, 0, 0)),
        pl.BlockSpec((D_MODEL, ATT_W + CP_W), lambda i: (0, 0)),
    ]
    out_specs = [tile(ATT_W), tile(CP_W)]
    out_shape = [jax.ShapeDtypeStruct((n, ATT_W), BF16), jax.ShapeDtypeStruct((n, CP_W), F32)]
    if with_state:
        in_specs.append(vec)
        args = [x, mod, w_a, kn2]
        for w in (NA_W, NA_W, GQA_KW, GQA_KW):
            out_specs.append(tile(w))
            out_shape.append(jax.ShapeDtypeStruct((n, w), F32))
    else:
        nblk = seq_len // t
        pos = pl.BlockSpec((t, PAIR_W), lambda i: (i % nblk, 0))
        in_specs += [vec, vec, pos, pos]
        args = [x, mod, w_a, qn2, kn2, cos, sin]
    return pl.pallas_call(
        functools.partial(_inproj_kernel, with_state=with_state),
        grid=(n // t,),
        in_specs=in_specs,
        out_specs=out_specs,
        out_shape=out_shape,
        compiler_params=_cparams(("arbitrary",)),
        name="inproj",
    )(*args)


def _dup_head(x2, g):
    lo = _lane_lo(x2.shape)
    other = pltpu.roll(x2, HEAD_DIM, 1)
    return jnp.where(lo, x2, other) if g == 0 else jnp.where(lo, other, x2)


def _dot_nt(a, b):
    return lax.dot_general(a, b, (((1,), (1,)), ((), ())), preferred_element_type=F32)


def _softmax_pv(scores, values):
    m = scores[0].max(axis=-1, keepdims=True)
    for s in scores[1:]:
        m = jnp.maximum(m, s.max(axis=-1, keepdims=True))
    l = None
    o = None
    for s, v in zip(scores, values):
        p = jnp.exp2(s - m)
        ls = p.sum(axis=-1, keepdims=True)
        os_ = jnp.dot(p.astype(BF16), v, preferred_element_type=F32)
        l = ls if l is None else l + ls
        o = os_ if o is None else o + os_
    return o / l


def _pair_attn(q2, keys, values, biases=None, stack=False, values_hi=None):
    lo = _lane_lo(q2.shape)
    m_rows = q2.shape[0]
    q_halves = [jnp.where(lo, q2, 0.0).astype(BF16), jnp.where(lo, 0.0, q2).astype(BF16)]

    def run(q, row0):
        scores = []
        for bi, k2 in enumerate(keys):
            s = _dot_nt(q, k2)
            if biases is not None and biases[bi] is not None:
                s = s + biases[bi][row0:row0 + q.shape[0]]
            scores.append(s)
        return _softmax_pv(scores, values)

    if stack:
        o = run(jnp.concatenate(q_halves, axis=0), 0)
        return jnp.where(lo, o[:m_rows], o[m_rows:])
    if values_hi is None:
        return jnp.where(lo, run(q_halves[0], 0), run(q_halves[1], m_rows))

    def run_mxu_sum(q, vals):
        scores = [_dot_nt(q, k2) for k2 in keys]
        m = scores[0].max(axis=-1, keepdims=True)
        for s in scores[1:]:
            m = jnp.maximum(m, s.max(axis=-1, keepdims=True))
        o = None
        for s, v in zip(scores, vals):
            os_ = jnp.dot(jnp.exp2(s - m).astype(BF16), v, preferred_element_type=F32)
            o = os_ if o is None else o + os_
        return o / pltpu.roll(o, HEAD_DIM, 1)

    return jnp.where(lo, run_mxu_sum(q_halves[0], values), run_mxu_sum(q_halves[1], values_hi))


def _ctx_attn_kernel(z_ref, qn_ref, kn_ref, o_ref):
    for p in range(NA_HEADS // 2):
        ps = slice(p * PAIR_W, (p + 1) * PAIR_W)
        q2 = z_ref[:, ps].astype(F32) * Q_SCALE
        k2 = z_ref[:, NA_W + p * PAIR_W:NA_W + (p + 1) * PAIR_W]
        v2 = z_ref[:, 2 * NA_W + p * PAIR_W:2 * NA_W + (p + 1) * PAIR_W]
        o_ref[:, ps] = _pair_attn(q2, [k2], [v2], stack=True).astype(BF16)
    qoff = 3 * NA_W
    koff = qoff + GQA_QW
    voff = koff + GQA_KW
    kn2 = _pair_rms(z_ref[:, koff:koff + GQA_KW].astype(F32), kn_ref[...])
    v2 = z_ref[:, voff:voff + GQA_KW].astype(F32)
    kd = [_dup_head(kn2, g).astype(BF16) for g in range(GQA_KV_HEADS)]
    vd = [_dup_head(v2, g).astype(BF16) for g in range(GQA_KV_HEADS)]
    for p in range(GQA_HEADS // 2):
        g = (2 * p) // GQA_REP
        q2 = _pair_rms(z_ref[:, qoff + p * PAIR_W:qoff + (p + 1) * PAIR_W].astype(F32), qn_ref[...]) * Q_SCALE
        o_ref[:, NA_W + p * PAIR_W:NA_W + (p + 1) * PAIR_W] = _pair_attn(q2, [kd[g]], [vd[g]],
                                                                          stack=True).astype(BF16)


def _ctx_attn(z_att, qn2, kn2, seq_len):
    n = z_att.shape[0]
    return pl.pallas_call(
        _ctx_attn_kernel,
        grid=(n // seq_len,),
        in_specs=[
            pl.BlockSpec((seq_len, ATT_W), lambda b: (b, 0)),
            pl.BlockSpec((1, PAIR_W), lambda b: (0, 0)),
            pl.BlockSpec((1, PAIR_W), lambda b: (0, 0)),
        ],
        out_specs=pl.BlockSpec((seq_len, NA_W + GQA_QW), lambda b: (b, 0)),
        out_shape=jax.ShapeDtypeStruct((n, NA_W + GQA_QW), BF16),
        compiler_params=_cparams(("arbitrary",)),
        name="ctx_attn",
    )(z_att, qn2, kn2)


def _na_row_start(r, rows):
    return jnp.clip(r - NA_WIN_R // 2, 0, rows - NA_WIN_R)


def _lat_na_kernel(q_ref, k_ref, v_ref, kc_ref, vc_ref, tb_ref, o_ref, *, rows, g_rows):
    r0 = pl.program_id(1) * g_rows
    wr = NA_WIN_R
    m2 = 2 * GRID_W
    lo = _lane_lo((GRID_W, PAIR_W))
    for p in range(NA_HEADS // 2):
        ps = slice(p * PAIR_W, (p + 1) * PAIR_W)
        kc2 = kc_ref[0, :, ps]
        vc2 = vc_ref[0, :, ps]
        qs = []
        for a in range(g_rows):
            q2 = q_ref[a * GRID_W:(a + 1) * GRID_W, ps].astype(F32) * Q_SCALE
            qs += [jnp.where(lo, q2, 0.0).astype(BF16), jnp.where(lo, 0.0, q2).astype(BF16)]
        qs_all = jnp.concatenate(qs, axis=0)
        s_ctx_all = _dot_nt(qs_all, kc2)
        o_loc, p_ctx, l_sum = [], [], []
        for a in range(g_rows):
            r = r0 + a
            rs = _na_row_start(r, rows)
            start = pl.multiple_of(rs * GRID_W, GRID_W)
            k2 = k_ref[pl.ds(start, wr * GRID_W), ps]
            v2 = v_ref[pl.ds(start, wr * GRID_W), ps]
            bias = tb_ref[rs - r + (wr - 1), 2 * p:2 * p + 2].reshape(m2, wr * GRID_W)
            s_loc = _dot_nt(qs_all[a * m2:(a + 1) * m2], k2) + bias
            s_ctx = s_ctx_all[a * m2:(a + 1) * m2]
            m = jnp.maximum(s_loc.max(axis=-1, keepdims=True), s_ctx.max(axis=-1, keepdims=True))
            e_loc = jnp.exp2(s_loc - m)
            e_ctx = jnp.exp2(s_ctx - m)
            l_sum.append(e_loc.sum(axis=-1, keepdims=True) + e_ctx.sum(axis=-1, keepdims=True))
            o_loc.append(jnp.dot(e_loc.astype(BF16), v2, preferred_element_type=F32))
            p_ctx.append(e_ctx.astype(BF16))
        o_ctx_all = jnp.dot(jnp.concatenate(p_ctx, axis=0), vc2, preferred_element_type=F32)
        for a in range(g_rows):
            o = (o_loc[a] + o_ctx_all[a * m2:(a + 1) * m2]) / l_sum[a]
            o_ref[a * GRID_W:(a + 1) * GRID_W, ps] = jnp.where(lo, o[:GRID_W], o[GRID_W:]).astype(BF16)


def _na_bias_tables(rpb):
    wr = NA_WIN_R
    nrel = 2 * NA_WIN_C - 1
    cols = np.arange(GRID_W)
    col_start = np.clip(cols - NA_WIN_C // 2, 0, GRID_W - NA_WIN_C)
    ck = np.arange(GRID_W)[None, :]
    valid = (ck >= col_start[:, None]) & (ck < col_start[:, None] + NA_WIN_C)
    rel = ck - cols[:, None] + (NA_WIN_C - 1)
    onehot = (np.arange(nrel)[:, None, None] == rel[None]) & valid[None]
    onehot = jnp.asarray(onehot.reshape(nrel, GRID_W * GRID_W), F32)
    d = np.arange(wr)[:, None] + np.arange(wr)[None, :]
    t = jnp.einsum("lhdir,rq->lhdiq", rpb[:, :, d] * LOG2E, onehot, precision=HIGHEST)
    t = t.reshape(rpb.shape[0], NA_HEADS, wr, wr, GRID_W, GRID_W)
    t = jnp.where(valid[None, None, None, None], t, NEG_BIG)
    t = t.transpose(0, 2, 1, 4, 3, 5)
    return t.reshape(rpb.shape[0], wr, NA_HEADS, GRID_W, wr * GRID_W)


def _lat_na(z_att, kc, vc, tb, seq_len):
    n = z_att.shape[0]
    nb = n // seq_len
    rows = seq_len // GRID_W
    assert rows >= NA_WIN_R
    wr = NA_WIN_R
    p = kc.shape[1]
    g_rows = NA_ROWS_PER_STEP
    blk = rows // g_rows
    return pl.pallas_call(
        functools.partial(_lat_na_kernel, rows=rows, g_rows=g_rows),
        grid=(nb, blk),
        in_specs=[
            pl.BlockSpec((g_rows * GRID_W, NA_W), lambda b, r: (b * blk + r, 0)),
            pl.BlockSpec((seq_len, NA_W), lambda b, r: (b, 1)),
            pl.BlockSpec((seq_len, NA_W), lambda b, r: (b, 2)),
            pl.BlockSpec((1, p, NA_W), lambda b, r: (b, 0, 0)),
            pl.BlockSpec((1, p, NA_W), lambda b, r: (b, 0, 0)),
            pl.BlockSpec((wr, NA_HEADS, GRID_W, wr * GRID_W), lambda b, r: (0, 0, 0, 0),
                         pipeline_mode=pl.Buffered(1)),
        ],
        out_specs=pl.BlockSpec((g_rows * GRID_W, NA_W), lambda b, r: (b * blk + r, 0)),
        out_shape=jax.ShapeDtypeStruct((n, NA_W), BF16),
        compiler_params=_cparams(("arbitrary", "arbitrary")),
        name="lat_na",
    )(z_att, z_att, z_att, kc, vc, tb)


def _lat_gqa_kernel(q_ref, k_ref, v_ref, kc_ref, vc_ref, o_ref):
    kn2 = k_ref[...].astype(F32)
    v2 = v_ref[...].astype(F32)
    kc2 = kc_ref[0].astype(F32)
    vc2 = vc_ref[0].astype(F32)
    def with_ones(x2, g):
        d = _dup_head(x2, g)
        lo = _lane_lo(d.shape)
        return jnp.where(lo, d, 1.0).astype(BF16), jnp.where(lo, 1.0, d).astype(BF16)

    kd, kcd, v_lo, v_hi = [], [], [], []
    for g in range(GQA_KV_HEADS):
        kd.append(_dup_head(kn2, g).astype(BF16))
        kcd.append(_dup_head(kc2, g).astype(BF16))
        (a0, a1), (b0, b1) = with_ones(v2, g), with_ones(vc2, g)
        v_lo.append([a0, b0])
        v_hi.append([a1, b1])
    for p in range(GQA_HEADS // 2):
        g = (2 * p) // GQA_REP
        ps = slice(p * PAIR_W, (p + 1) * PAIR_W)
        q2 = q_ref[:, ps].astype(F32)
        o_ref[:, ps] = _pair_attn(q2, [kd[g], kcd[g]], v_lo[g], values_hi=v_hi[g]).astype(BF16)


def _rope_tables(seq_len):
    t = jnp.arange(seq_len)
    n_freq = HEAD_DIM // 4
    inv = ROPE_THETA ** (-jnp.arange(n_freq, dtype=F32) / n_freq)
    ang_r = (t // GRID_W).astype(F32)[:, None] * inv
    ang_c = (t % GRID_W).astype(F32)[:, None] * inv
    cr, sr, cc, sc = jnp.cos(ang_r), jnp.sin(ang_r), jnp.cos(ang_c), jnp.sin(ang_c)
    cos = jnp.concatenate([cr, cr, cc, cc] * 2, axis=-1)
    sin = jnp.concatenate([-sr, sr, -sc, sc] * 2, axis=-1)
    return cos, sin


def _lat_gqa(z_att, kc, vc, seq_len):
    n = z_att.shape[0]
    nb = n // seq_len
    tq = min(GQA_QBLOCK, seq_len)
    nqb = seq_len // tq
    p = kc.shape[1]
    qcol = (3 * NA_W) // GQA_QW
    kcol = (3 * NA_W + GQA_QW) // GQA_KW
    return pl.pallas_call(
        _lat_gqa_kernel,
        grid=(nb, nqb),
        in_specs=[
            pl.BlockSpec((tq, GQA_QW), lambda b, i: (b * nqb + i, qcol)),
            pl.BlockSpec((seq_len, GQA_KW), lambda b, i: (b, kcol)),
            pl.BlockSpec((seq_len, GQA_KW), lambda b, i: (b, kcol + 1)),
            pl.BlockSpec((1, p, GQA_KW), lambda b, i: (b, 0, 0)),
            pl.BlockSpec((1, p, GQA_KW), lambda b, i: (b, 0, 0)),
        ],
        out_specs=pl.BlockSpec((tq, GQA_QW), lambda b, i: (b * nqb + i, 0)),
        out_shape=jax.ShapeDtypeStruct((n, GQA_QW), BF16),
        compiler_params=_cparams(("arbitrary", "arbitrary")),
        name="lat_gqa",
    )(z_att, z_att, z_att, kc, vc)


def _layer_norm(x, g, b):
    mu = jnp.mean(x, axis=-1, keepdims=True)
    xc = x - mu
    var = jnp.mean(xc * xc, axis=-1, keepdims=True)
    return xc * lax.rsqrt(var + LN_EPS) * g + b


def _merge_kernel(x_ref, mod_ref, cp_ref, cpp_ref, cpn_ref, *rest, seq_len, n_att):
    att_refs = rest[:n_att]
    (wg_ref, wb_ref, wo_ref, pw_ref, cw_ref, cb_ref, ps_ref, lg_ref, lb_ref, wr_ref,
     x1_ref, xm2_ref, aff_ref, ucv_ref, upl_ref) = rest[n_att:]
    t = x_ref.shape[0]
    halo = POOL_HALO
    i = pl.program_id(0)
    pos0 = (i * t) % seq_len
    has_prev = pos0 > 0
    has_next = pos0 + t < seq_len

    shift1, scale1, gate1 = mod_ref[0, 0:1, :], mod_ref[0, 1:2, :], mod_ref[0, 2:3, :]
    shift2, scale2 = mod_ref[0, 3:4, :], mod_ref[0, 4:5, :]

    bw = BRANCH_W
    ucv_ref[halo:halo + t, :] = cp_ref[:, bw:2 * bw] * cp_ref[:, 2 * bw:3 * bw]
    upl_ref[halo:halo + t, :] = cp_ref[:, 3 * bw:4 * bw]
    ucv_ref[0:halo, :] = jnp.where(has_prev, cpp_ref[:, bw:2 * bw] * cpp_ref[:, 2 * bw:3 * bw], 0.0)
    upl_ref[0:halo, :] = jnp.where(has_prev, cpp_ref[:, 3 * bw:4 * bw], 0.0)
    ucv_ref[halo + t:, :] = jnp.where(has_next, cpn_ref[:, bw:2 * bw] * cpn_ref[:, 2 * bw:3 * bw], 0.0)
    upl_ref[halo + t:, :] = jnp.where(has_next, cpn_ref[:, 3 * bw:4 * bw], 0.0)

    wr = wr_ref[...]
    wh = wr.astype(BF16)
    wl = (wr - wh.astype(F32)).astype(BF16)

    th = min(t, MERGE_SUB)
    blocks = [dict(r0=r0, rows=slice(r0, r0 + th)) for r0 in range(0, t, th)]

    def prologue(b):
        r0, rows = b["r0"], b["rows"]
        x = x_ref[rows, :]
        b["x"] = x
        b["xm"] = (x * (1.0 + scale1) + shift1).astype(BF16)
        conv = (cw_ref[0:1, :] * ucv_ref[halo - 1 + r0:halo - 1 + r0 + th, :]
                + cw_ref[1:2, :] * ucv_ref[halo + r0:halo + r0 + th, :]
                + cw_ref[2:3, :] * ucv_ref[halo + 1 + r0:halo + 1 + r0 + th, :]) + cb_ref[...]
        o_conv = (cp_ref[rows, 0:bw] * conv).astype(BF16)
        pos = pos0 + r0 + lax.broadcasted_iota(I32, (th, 1), 0)
        mixed = []
        for g, win in enumerate(POOL_WINDOWS):
            gs = slice(g * POOL_GW, (g + 1) * POOL_GW)
            acc = None
            for dlt in range(-(win // 2), win // 2):
                term = upl_ref[halo + r0 + dlt:halo + r0 + dlt + th, gs]
                acc = term if acc is None else acc + term
            lo = jnp.maximum(pos - win // 2, 0)
            hi = jnp.minimum(pos + win // 2, seq_len)
            cnt = (hi - lo).astype(F32)
            pooled = acc / cnt - upl_ref[halo + r0:halo + r0 + th, gs]
            mixed.append(jnp.dot(pooled.astype(BF16), pw_ref[g], preferred_element_type=F32))
        o_pool = (jnp.concatenate(mixed, axis=-1) * ps_ref[...]).astype(BF16)
        if n_att == 1:
            b["branches"] = [att_refs[0][rows, 0:bw], att_refs[0][rows, bw:2 * bw], o_conv, o_pool]
        else:
            b["branches"] = [att_refs[0][rows, :], att_refs[1][rows, :], o_conv, o_pool]
        b["merged"] = None

    def dots(b, nb):
        b["zg"] = jnp.dot(b["xm"], wg_ref[:, nb * D_MODEL:(nb + 1) * D_MODEL], preferred_element_type=F32)
        b["proj"] = jnp.dot(b["branches"][nb], wb_ref[nb], preferred_element_type=F32)

    def gate(b):
        term = _sigmoid(b["zg"]) * b["proj"]
        b["merged"] = term if b["merged"] is None else b["merged"] + term

    def out_proj(b):
        b["y"] = jnp.dot(b["merged"].astype(BF16), wo_ref[...], preferred_element_type=F32)

    def tail(b):
        rows = b["rows"]
        x1 = _layer_norm(DEEPNORM_ALPHA * b["x"] + gate1 * b["y"], lg_ref[...], lb_ref[...])
        x1_ref[rows, :] = x1
        xm2 = x1 * (1.0 + scale2) + shift2
        xm2_ref[rows, :] = xm2
        xh = xm2.astype(BF16)
        xl = (xm2 - xh.astype(F32)).astype(BF16)
        logits = (jnp.dot(xh, wh, preferred_element_type=F32) + jnp.dot(xl, wh, preferred_element_type=F32)
                  + jnp.dot(xh, wl, preferred_element_type=F32))
        m = logits.max(axis=-1, keepdims=True)
        e = jnp.exp(logits - m)
        aff_ref[rows, :] = e / e.sum(axis=-1, keepdims=True)

    for b in blocks:
        prologue(b)
        for nb in range(N_BRANCH):
            dots(b, nb)
            gate(b)
        out_proj(b)
        tail(b)


def _merge(x, mod, z_cp, atts, w_gates, w_branch, w_out, pool_w, conv_w, conv_b, pool_scale, ln_g, ln_b,
           w_router, seq_len):
    n = x.shape[0]
    t = min(MERGE_TILE, seq_len)
    per = n // mod.shape[0] // t
    hb = t // POOL_HALO
    nhb = n // POOL_HALO
    row = lambda a: a.reshape(1, -1)
    const2 = lambda i: (0, 0)
    const3 = lambda i: (0, 0, 0)
    once = lambda shp, imap: pl.BlockSpec(shp, imap, pipeline_mode=pl.Buffered(1))
    att_specs = [pl.BlockSpec((t, a.shape[1]), lambda i: (i, 0)) for a in atts]
    return pl.pallas_call(
        functools.partial(_merge_kernel, seq_len=seq_len, n_att=len(atts)),
        grid=(n // t,),
        in_specs=[
            pl.BlockSpec((t, D_MODEL), lambda i: (i, 0)),
            pl.BlockSpec((1, N_MOD, D_MODEL), lambda i: (i // per, 0, 0)),
            pl.BlockSpec((t, CP_W), lambda i: (i, 0)),
            pl.BlockSpec((POOL_HALO, CP_W), lambda i: (jnp.maximum(i * hb - 1, 0), 0)),
            pl.BlockSpec((POOL_HALO, CP_W), lambda i: (jnp.minimum((i + 1) * hb, nhb - 1), 0)),
            *att_specs,
            once((D_MODEL, GATE_W), const2),
            once((N_BRANCH, BRANCH_W, D_MODEL), const3),
            once((D_MODEL, D_MODEL), const2),
            pl.BlockSpec((len(POOL_WINDOWS), POOL_GW, POOL_GW), const3),
            pl.BlockSpec((CONV_K, BRANCH_W), const2),
            pl.BlockSpec((1, BRANCH_W), const2),
            pl.BlockSpec((1, BRANCH_W), const2),
            pl.BlockSpec((1, D_MODEL), const2),
            pl.BlockSpec((1, D_MODEL), const2),
            pl.BlockSpec((D_MODEL, N_EXPERTS), const2),
        ],
        out_specs=[
            pl.BlockSpec((t, D_MODEL), lambda i: (i, 0)),
            pl.BlockSpec((t, D_MODEL), lambda i: (i, 0)),
            pl.BlockSpec((t, N_EXPERTS), lambda i: (i, 0)),
        ],
        out_shape=[jax.ShapeDtypeStruct((n, D_MODEL), F32), jax.ShapeDtypeStruct((n, D_MODEL), F32),
                   jax.ShapeDtypeStruct((n, N_EXPERTS), F32)],
        scratch_shapes=[pltpu.VMEM((t + 2 * POOL_HALO, BRANCH_W), F32),
                        pltpu.VMEM((t + 2 * POOL_HALO, BRANCH_W), F32)],
        compiler_params=_cparams(("arbitrary",)),
        name="merge",
    )(x, mod, z_cp, z_cp, z_cp, *atts, w_gates, w_branch, w_out, pool_w, conv_w, row(conv_b), row(pool_scale),
      row(ln_g), row(ln_b), w_router)


def _topk_kernel(aff_ref, idx_ref, dst_ref, gp_ref, cnt_ref, thr_ref, rank_ref, *, cap, plane_stride):
    ne, nc, _ = aff_ref.shape
    capf = float(cap)
    bits_all = pltpu.bitcast(aff_ref[...], I32)

    def search(i, cur):
        cand = cur | (jnp.int32(1) << (30 - i))
        cnt = jnp.sum((bits_all >= cand).astype(F32), axis=(1, 2), keepdims=True)
        return jnp.where(cnt >= capf, cand, cur)

    thr = lax.fori_loop(0, 31, search, jnp.zeros((ne, 1, 1), I32))
    thr_ref[...] = jnp.broadcast_to(thr, thr_ref.shape)

    jj = lax.broadcasted_iota(I32, (CHUNK, CHUNK), 0)
    kk = lax.broadcasted_iota(I32, (CHUNK, CHUNK), 1)
    ut_incl = (jj <= kk).astype(BF16)
    lt_incl = (kk <= jj).astype(BF16)
    cc = lax.broadcasted_iota(I32, (nc, nc), 0)
    dd = lax.broadcasted_iota(I32, (nc, nc), 1)
    cl_excl = (dd < cc).astype(BF16)
    cl_incl = (dd <= cc).astype(BF16)
    eye = (jj == kk).astype(BF16)
    wide = 4 if cap % (4 * CHUNK) == 0 else 1
    lane_w = lax.broadcasted_iota(I32, (1, wide * CHUNK), 1)
    sub_cw = lax.broadcasted_iota(I32, (nc, wide * CHUNK), 0).astype(F32)
    sub_jw = lax.broadcasted_iota(I32, (CHUNK, wide * CHUNK), 0).astype(F32)
    rank_ref[...] = jnp.zeros(rank_ref.shape, F32)
    gp_ref[...] = jnp.full(gp_ref.shape, -1.0, F32)

    def per_expert(e, carry):
        a = aff_ref[e]
        b = pltpu.bitcast(a, I32)
        t = thr_ref[e][0:1, :]
        gt = b > t
        eq = b == t
        need = capf - jnp.sum(gt.astype(F32), keepdims=True)
        eqf = eq.astype(F32)
        incl_eq = jnp.dot(eqf.astype(BF16), ut_incl, preferred_element_type=F32)
        tot_eq = jnp.broadcast_to(incl_eq[:, CHUNK - 1:CHUNK], (nc, CHUNK))
        cum_eq = jnp.dot(cl_excl, tot_eq.astype(BF16), preferred_element_type=F32)
        rank = cum_eq + incl_eq - eqf
        sel = jnp.logical_or(gt, jnp.logical_and(eq, rank < need))
        self_ = sel.astype(F32)

        plane = rank_ref[...]
        rank_ref[...] = plane + self_
        for k in range(ne):
            gp_ref[k] = jnp.where(jnp.logical_and(sel, plane == float(k)), a, gp_ref[k])
        plane_tb = _dot_nt(eye, plane.astype(BF16)).astype(BF16)

        selb = self_.astype(BF16)
        incl = jnp.dot(selb, ut_incl, preferred_element_type=F32)
        tot = jnp.broadcast_to(incl[:, CHUNK - 1:CHUNK], (nc, CHUNK))
        cum_incl = jnp.dot(cl_incl, tot.astype(BF16), preferred_element_type=F32)
        incl_tb = _dot_nt(lt_incl, selb).astype(BF16)

        def per_block(sb, c2):
            s_row = (sb * (wide * CHUNK) + lane_w).astype(F32)
            m = cum_w <= s_row
            c_of_s = jnp.sum(m.astype(F32), axis=0, keepdims=True)
            excl_s = jnp.max(jnp.where(m, cum_w, 0.0), axis=0, keepdims=True)
            onehot_t = (sub_cw == c_of_s).astype(BF16)
            rows_t = jnp.dot(incl_tb, onehot_t, preferred_element_type=F32)
            s_local = s_row - excl_s
            t_local = jnp.sum((rows_t <= s_local).astype(F32), axis=0, keepdims=True)
            tok = (c_of_s * float(CHUNK) + t_local).astype(I32)
            plane_rows = jnp.dot(plane_tb, onehot_t, preferred_element_type=F32)
            plane_s = jnp.sum(jnp.where(sub_jw == t_local, plane_rows, 0.0), axis=0, keepdims=True)
            dst = plane_s.astype(I32) * plane_stride + tok
            for u in range(wide):
                idx_ref[e, pl.ds(sb * wide + u, 1), :] = tok[:, u * CHUNK:(u + 1) * CHUNK]
                dst_ref[e, pl.ds(sb * wide + u, 1), :] = dst[:, u * CHUNK:(u + 1) * CHUNK]
            return c2

        cum_w = jnp.concatenate([cum_incl] * wide, axis=1)
        lax.fori_loop(0, cap // (wide * CHUNK), per_block, 0)
        return carry

    lax.fori_loop(0, ne, per_expert, 0)
    cnt_ref[...] = jnp.broadcast_to(jnp.max(rank_ref[...], axis=1, keepdims=True), cnt_ref.shape).astype(I32)


def _topk(aff3, cap, plane_stride):
    ne, nc, _ = aff3.shape
    full3 = lambda shp: pl.BlockSpec(shp, lambda i: (0, 0, 0))
    return pl.pallas_call(
        functools.partial(_topk_kernel, cap=cap, plane_stride=plane_stride),
        grid=(1,),
        in_specs=[full3((ne, nc, CHUNK))],
        out_specs=[full3((ne, cap // CHUNK, CHUNK)), full3((ne, cap // CHUNK, CHUNK)), full3((ne, nc, CHUNK)),
                   pl.BlockSpec((nc, CHUNK), lambda i: (0, 0))],
        out_shape=[jax.ShapeDtypeStruct((ne, cap // CHUNK, CHUNK), I32),
                   jax.ShapeDtypeStruct((ne, cap // CHUNK, CHUNK), I32),
                   jax.ShapeDtypeStruct((ne, nc, CHUNK), F32),
                   jax.ShapeDtypeStruct((nc, CHUNK), I32)],
        scratch_shapes=[pltpu.VMEM((ne, SUBLANES, CHUNK), I32), pltpu.VMEM((nc, CHUNK), F32)],
        compiler_params=_cparams(("arbitrary",)),
        name="topk",
    )(aff3)


def _ffn_kernel(idx_ref, idxn_ref, dst_ref, x_hbm, wg_ref, wu_ref, wd_ref, y_in_hbm, y_hbm, xs_ref, ys_ref, wgu_ref,
                wdn_ref, gsem_ref, ssem_ref, *, steps):
    tm = xs_ref.shape[1]
    s = pl.program_id(0) * pl.num_programs(1) + pl.program_id(1)

    @pl.when(pl.program_id(1) == 0)
    def _():
        wgu_ref[0] = wg_ref[0].astype(BF16)
        wgu_ref[1] = wu_ref[0].astype(BF16)
        wdn_ref[...] = wd_ref[0].astype(BF16)

    def gather_rows(idx_smem, buf_slot):
        for j in range(tm):
            pltpu.make_async_copy(x_hbm.at[pl.ds(idx_smem[0, 0, j], 1), :], xs_ref.at[buf_slot, pl.ds(j, 1), :],
                                  gsem_ref.at[buf_slot]).start(priority=j % 2)

    def gathered(buf_slot):
        return pltpu.make_async_copy(x_hbm.at[pl.ds(0, tm), :], xs_ref.at[buf_slot], gsem_ref.at[buf_slot])

    def scattered(buf_slot):
        return pltpu.make_async_copy(ys_ref.at[buf_slot], y_hbm.at[pl.ds(0, tm), :], ssem_ref.at[buf_slot])

    @pl.when(s == 0)
    def _():
        gather_rows(idx_ref, 0)

    def step(slot):
        @pl.when(s + 1 < steps)
        def _():
            gather_rows(idxn_ref, 1 - slot)

        gathered(slot).wait()
        xs = xs_ref[slot].astype(BF16)
        hg = jnp.dot(xs, wgu_ref[0], preferred_element_type=F32)
        hu = jnp.dot(xs, wgu_ref[1], preferred_element_type=F32)
        hdn = (hg * _sigmoid(hg) * hu).astype(BF16)
        ye = jnp.dot(hdn, wdn_ref[...], preferred_element_type=F32)

        @pl.when(s >= 2)
        def _():
            scattered(slot).wait()

        ys_ref[slot] = _pack_bf16_pair(ye[:, :PACK_W], ye[:, PACK_W:])
        for j in range(tm):
            pltpu.make_async_copy(ys_ref.at[slot, pl.ds(j, 1), :], y_hbm.at[pl.ds(dst_ref[0, 0, j], 1), :],
                                  ssem_ref.at[slot]).start(priority=j % 2)

        @pl.when(s == steps - 1)
        def _():
            scattered(slot).wait()
            if steps > 1:
                scattered(1 - slot).wait()

    for parity in range(2):
        pl.when(s % 2 == parity)(functools.partial(step, parity))


def _ffn(idx_tiles, dst_tiles, xm2, w_gate, w_up, w_down, layer, cap, ybuf):
    tm = idx_tiles.shape[2]
    nt = cap // tm
    steps = N_EXPERTS * nt
    wspec = lambda: pl.BlockSpec((None, 1, D_MODEL, EXPERT_FF), lambda e, i: (layer, e, 0, 0))
    return pl.pallas_call(
        functools.partial(_ffn_kernel, steps=steps),
        grid=(N_EXPERTS, nt),
        in_specs=[
            pl.BlockSpec((1, 1, tm), lambda e, i: (e * nt + i, 0, 0), memory_space=pltpu.SMEM),
            pl.BlockSpec((1, 1, tm), lambda e, i: (jnp.minimum(e * nt + i + 1, steps - 1), 0, 0),
                         memory_space=pltpu.SMEM),
            pl.BlockSpec((1, 1, tm), lambda e, i: (e * nt + i, 0, 0), memory_space=pltpu.SMEM),
            pl.BlockSpec(memory_space=pl.ANY),
            wspec(), wspec(),
            pl.BlockSpec((None, 1, EXPERT_FF, D_MODEL), lambda e, i: (layer, e, 0, 0)),
            pl.BlockSpec(memory_space=pl.ANY),
        ],
        out_specs=pl.BlockSpec(memory_space=pl.ANY),
        out_shape=jax.ShapeDtypeStruct(ybuf.shape, U32),
        input_output_aliases={7: 0},
        scratch_shapes=[pltpu.VMEM((2, tm, D_MODEL), F32), pltpu.VMEM((2, tm, PACK_W), U32),
                        pltpu.VMEM((2, D_MODEL, EXPERT_FF), BF16), pltpu.VMEM((EXPERT_FF, D_MODEL), BF16),
                        pltpu.SemaphoreType.DMA((2,)), pltpu.SemaphoreType.DMA((2,))],
        compiler_params=_cparams(("arbitrary", "arbitrary")),
        name="ffn",
    )(idx_tiles, idx_tiles, dst_tiles, xm2, w_gate, w_up, w_down, ybuf)


def _combine_kernel(kmax_ref, y_hbm, x1_ref, mod_ref, gp_ref, lg_ref, lb_ref, o_ref, planes_ref, acc_ref, sem_ref,
                    *, n_planes):
    t = x1_ref.shape[0]
    i = pl.program_id(0)
    ntile = pl.num_programs(0)
    slot = i % 2

    def for_planes(tile, buf_slot, fn):
        for k in range(n_planes):
            @pl.when(k < kmax_ref[tile])
            def _(k=k):
                fn(pltpu.make_async_copy(y_hbm.at[k, pl.ds(tile * t, t), :], planes_ref.at[buf_slot, k],
                                         sem_ref.at[buf_slot]))

    @pl.when(i == 0)
    def _():
        for_planes(0, 0, lambda cp: cp.start())

    @pl.when(i + 1 < ntile)
    def _():
        for_planes(i + 1, 1 - slot, lambda cp: cp.start())

    for_planes(i, slot, lambda cp: cp.wait())

    acc_ref[...] = jnp.zeros(acc_ref.shape, F32)
    for k in range(n_planes):
        @pl.when(k < kmax_ref[i])
        def _(k=k):
            w = gp_ref[:, k:k + 1]
            lo, hi = _unpack_bf16_pair(planes_ref[slot, k])
            acc_ref[:, :PACK_W] += jnp.where(w >= 0.0, w * lo, 0.0)
            acc_ref[:, PACK_W:] += jnp.where(w >= 0.0, w * hi, 0.0)

    gate2 = mod_ref[0, 5:6, :]
    o_ref[...] = _layer_norm(DEEPNORM_ALPHA * x1_ref[...] + gate2 * acc_ref[...], lg_ref[...], lb_ref[...])


def _combine(kmax, y_planes, x1, mod, gp_tok, ln_g, ln_b, seq_len):
    n = x1.shape[0]
    n_planes = y_planes.shape[0]
    t = min(TOK_TILE, n // mod.shape[0])
    per = n // mod.shape[0] // t
    row = lambda a: a.reshape(1, -1)
    grid_spec = pltpu.PrefetchScalarGridSpec(
        num_scalar_prefetch=1,
        grid=(n // t,),
        in_specs=[
            pl.BlockSpec(memory_space=pl.ANY),
            pl.BlockSpec((t, D_MODEL), lambda i, *_: (i, 0)),
            pl.BlockSpec((1, N_MOD, D_MODEL), lambda i, *_: (i // per, 0, 0)),
            pl.BlockSpec((t, n_planes), lambda i, *_: (i, 0)),
            pl.BlockSpec((1, D_MODEL), lambda i, *_: (0, 0)),
            pl.BlockSpec((1, D_MODEL), lambda i, *_: (0, 0)),
        ],
        out_specs=pl.BlockSpec((t, D_MODEL), lambda i, *_: (i, 0)),
        scratch_shapes=[pltpu.VMEM((2, n_planes, t, PACK_W), U32), pltpu.VMEM((t, D_MODEL), F32),
                        pltpu.SemaphoreType.DMA((2,))],
    )
    return pl.pallas_call(
        functools.partial(_combine_kernel, n_planes=n_planes),
        grid_spec=grid_spec,
        out_shape=jax.ShapeDtypeStruct((n, D_MODEL), F32),
        compiler_params=_cparams(("arbitrary",)),
        name="combine",
    )(kmax, y_planes, x1, mod, gp_tok, row(ln_g), row(ln_b))


def _moe(x1, xm2, aff, mod, w_gate, w_up, w_down, layer, ln_g, ln_b, seq_len, ybuf):
    n = x1.shape[0]
    cap = EC_CAPACITY * n // N_EXPERTS
    nc = n // CHUNK
    t = min(TOK_TILE, n // mod.shape[0])
    stride = ybuf.shape[0] // N_EXPERTS
    aff3 = aff.T.reshape(N_EXPERTS, nc, CHUNK)
    idx3, dst3, gp3, cnt = _topk(aff3, cap, stride)
    gp_tok = gp3.reshape(N_EXPERTS, n).T
    kmax = jnp.max(cnt[:, 0].reshape(n // t, t // CHUNK), axis=1)
    tm = min(FFN_TILE, cap)
    idx_tiles = idx3.reshape(N_EXPERTS * cap // tm, 1, tm)
    dst_tiles = dst3.reshape(N_EXPERTS * cap // tm, 1, tm)
    ybuf = _ffn(idx_tiles, dst_tiles, xm2, w_gate, w_up, w_down, layer, cap, ybuf)
    y = _combine(kmax, ybuf.reshape(N_EXPERTS, stride, PACK_W), x1, mod, gp_tok, ln_g, ln_b, seq_len)
    return y, ybuf


def kernel(x_prompt, x_sample, cache_na_k, cache_na_v, cache_gqa_k, cache_gqa_v, c, c_ctx, w_ada, b_ada, w_in,
           na_rpb, qn_g, kn_g, conv_w, conv_b, pool_w, pool_scale, w_branch, w_out, ln1_g, ln1_b, ln2_g, ln2_b,
           w_router, w_gate, w_up, w_down):
    depth = w_in.shape[0]
    bp, lp, _ = x_prompt.shape
    bs, ls, _ = x_sample.shape
    past = cache_na_k.shape[2]

    mods = _adaln(jnp.concatenate([c_ctx[None, :], c], axis=0), w_ada, b_ada)
    mods = mods.reshape(depth, 1 + bs, N_MOD, D_MODEL)
    cos, sin = _rope_tables(ls)
    tbs = _na_bias_tables(na_rpb)

    yp = x_prompt.reshape(bp * lp, D_MODEL)
    ys = x_sample.reshape(bs * ls, D_MODEL)
    ybuf_p = jnp.zeros((N_EXPERTS * bp * lp, PACK_W), U32)
    ybuf_s = jnp.zeros((N_EXPERTS * bs * ls, PACK_W), U32)
    na_k_l, na_v_l, g_k_l, g_v_l = [], [], [], []
    for l in range(depth):
        mod_c = mods[l, 0:1]
        mod_s = mods[l, 1:]
        w_a = w_in[l, :, :ATT_W + CP_W].astype(BF16)
        w_gates = w_in[l, :, ATT_W + CP_W:].astype(BF16)
        w_br = w_branch[l].astype(BF16)
        w_o = w_out[l].astype(BF16)
        p_w = pool_w[l].astype(BF16)
        moe_w = (w_gate, w_up, w_down, l)
        merge_w = (w_gates, w_br, w_o, p_w, conv_w[l], conv_b[l], pool_scale[l], ln1_g[l], ln1_b[l], w_router[l])

        qn2 = jnp.tile(qn_g[l], 2).reshape(1, PAIR_W)
        kn2 = jnp.tile(kn_g[l], 2).reshape(1, PAIR_W)
        cache = lambda a, w: a[:, l].reshape(bs, past, w).astype(BF16)

        z_att, z_cp, na_k, na_v, g_k, g_v = _inproj(yp, mod_c, w_a, bp * lp, kn2)
        o_att = _ctx_attn(z_att, qn2, kn2, lp)
        na_k_l.append(na_k.reshape(bp, lp, NA_HEADS, HEAD_DIM))
        na_v_l.append(na_v.reshape(bp, lp, NA_HEADS, HEAD_DIM))
        g_k_l.append(g_k.reshape(bp, lp, GQA_KV_HEADS, HEAD_DIM))
        g_v_l.append(g_v.reshape(bp, lp, GQA_KV_HEADS, HEAD_DIM))
        x1, xm2, aff = _merge(yp, mod_c, z_cp, [o_att], *merge_w, lp)
        yp, ybuf_p = _moe(x1, xm2, aff, mod_c, *moe_w, ln2_g[l], ln2_b[l], lp, ybuf_p)

        z_att, z_cp = _inproj(ys, mod_s, w_a, ls, kn2, qn2, cos, sin)
        o_na = _lat_na(z_att, cache(cache_na_k, NA_W), cache(cache_na_v, NA_W), tbs[l], ls)
        o_gqa = _lat_gqa(z_att, cache(cache_gqa_k, GQA_KW), cache(cache_gqa_v, GQA_KW), ls)
        x1, xm2, aff = _merge(ys, mod_s, z_cp, [o_na, o_gqa], *merge_w, ls)
        ys, ybuf_s = _moe(x1, xm2, aff, mod_s, *moe_w, ln2_g[l], ln2_b[l], ls, ybuf_s)

    return (yp.reshape(bp, lp, D_MODEL), ys.reshape(bs, ls, D_MODEL),
            jnp.stack(na_k_l, axis=1), jnp.stack(na_v_l, axis=1),
            jnp.stack(g_k_l, axis=1), jnp.stack(g_v_l, axis=1))
```

```python
import functools

import numpy as np
import jax
import jax.numpy as jnp
from jax import lax
from jax.experimental import pallas as pl
from jax.experimental.pallas import tpu as pltpu

F32 = jnp.float32
BF16 = jnp.bfloat16
I32 = jnp.int32
U32 = jnp.uint32
HIGHEST = lax.Precision.HIGHEST

D_MODEL = 1024
NOMINAL_DEPTH = 4
GRID_W = 64
HEAD_DIM = 64
NA_HEADS = 8
NA_WIN_R = 8
NA_WIN_C = 16
GQA_HEADS = 8
GQA_KV_HEADS = 2
GQA_REP = GQA_HEADS // GQA_KV_HEADS
BRANCH_W = 512
N_BRANCH = 4
CONV_K = 3
POOL_WINDOWS = (2, 4, 8, 16)
POOL_GW = BRANCH_W // len(POOL_WINDOWS)
POOL_HALO = max(POOL_WINDOWS) // 2
N_EXPERTS = 16
EXPERT_FF = 1024
EC_CAPACITY = 2
ROPE_THETA = 10000.0
LN_EPS = 1e-6
RMS_EPS = 1e-6
N_MOD = 6
DEEPNORM_ALPHA = (2 * NOMINAL_DEPTH) ** 0.25
NA_W = NA_HEADS * HEAD_DIM
GQA_QW = GQA_HEADS * HEAD_DIM
GQA_KW = GQA_KV_HEADS * HEAD_DIM
ATT_W = 3 * NA_W + GQA_QW + 2 * GQA_KW
CP_W = 4 * BRANCH_W
GATE_W = N_BRANCH * D_MODEL
LOG2E = 1.4426950408889634
Q_SCALE = HEAD_DIM ** -0.5 * LOG2E
NEG_BIG = -1e30
PAIR_W = 2 * HEAD_DIM
PACK_W = D_MODEL // 2

LANES = 128
SUBLANES = 8
VMEM_LIMIT = 56 * 1024 * 1024

TOK_TILE_IN = 512
MERGE_TILE = 512
MERGE_SUB = 256
TOK_TILE = 512
CHUNK = LANES
FFN_TILE = 512
GQA_QBLOCK = 1024
NA_ROWS_PER_STEP = 8


def _cparams(sem):
    return pltpu.CompilerParams(dimension_semantics=sem, vmem_limit_bytes=VMEM_LIMIT)


def _pack_bf16_pair(a, b):
    lo = pltpu.bitcast(a.astype(BF16).astype(F32), U32)
    hi = pltpu.bitcast(b.astype(BF16).astype(F32), U32)
    return (lo >> 16) | (hi & jnp.uint32(0xFFFF0000))


def _unpack_bf16_pair(w):
    return pltpu.bitcast(w << 16, F32), pltpu.bitcast(w & jnp.uint32(0xFFFF0000), F32)


def _sigmoid(x):
    return 0.5 * jnp.tanh(0.5 * x) + 0.5


def _adaln_kernel(cv_ref, w_ref, b_ref, o_ref):
    cv = cv_ref[...]
    s = cv * jax.nn.sigmoid(cv)
    o_ref[0] = jnp.dot(s, w_ref[0], precision=HIGHEST, preferred_element_type=F32) + b_ref[0]


def _adaln(cvecs, w_ada, b_ada):
    depth = w_ada.shape[0]
    r = cvecs.shape[0]
    tn = 1536
    nw = N_MOD * D_MODEL
    return pl.pallas_call(
        _adaln_kernel,
        grid=(depth, nw // tn),
        in_specs=[
            pl.BlockSpec((r, D_MODEL), lambda l, j: (0, 0)),
            pl.BlockSpec((1, D_MODEL, tn), lambda l, j: (l, 0, j)),
            pl.BlockSpec((1, 1, tn), lambda l, j: (l, 0, j)),
        ],
        out_specs=pl.BlockSpec((1, r, tn), lambda l, j: (l, 0, j)),
        out_shape=jax.ShapeDtypeStruct((depth, r, nw), F32),
        compiler_params=_cparams(("arbitrary", "arbitrary")),
        name="adaln",
    )(cvecs, w_ada, b_ada.reshape(depth, 1, nw))


def _lane_lo(shape):
    return (lax.broadcasted_iota(I32, shape, len(shape) - 1) % PAIR_W) < HEAD_DIM


def _pair_rms(x2, g2):
    lo = _lane_lo(x2.shape)
    sq = x2 * x2
    ss_lo = jnp.sum(jnp.where(lo, sq, 0.0), axis=-1, keepdims=True)
    ss_hi = jnp.sum(jnp.where(lo, 0.0, sq), axis=-1, keepdims=True)
    inv = lax.rsqrt(jnp.where(lo, ss_lo, ss_hi) * (1.0 / HEAD_DIM) + RMS_EPS)
    return x2 * inv * g2


def _pair_rope(x2, cos2, sin2):
    q = HEAD_DIM // 4
    first = (lax.broadcasted_iota(I32, x2.shape, 1) % (2 * q)) < q
    swapped = jnp.where(first, pltpu.roll(x2, PAIR_W - q, 1), pltpu.roll(x2, q, 1))
    return x2 * cos2 + swapped * sin2


def _inproj_kernel(x_ref, mod_ref, w_ref, *rest, with_state):
    if with_state:
        kn_ref, att_ref, cp_ref, nak_ref, nav_ref, gk_ref, gv_ref = rest
    else:
        qn_ref, kn_ref, cos_ref, sin_ref, att_ref, cp_ref = rest
    shift = mod_ref[0, 0:1, :]
    scale = mod_ref[0, 1:2, :]
    xm = (x_ref[...] * (1.0 + scale) + shift).astype(BF16)
    z = jnp.dot(xm, w_ref[...], preferred_element_type=F32)
    cp_ref[...] = z[:, ATT_W:]
    qoff = 3 * NA_W
    koff = qoff + GQA_QW
    if with_state:
        att_ref[...] = z[:, :ATT_W].astype(BF16)
        nak_ref[...] = z[:, NA_W:2 * NA_W]
        nav_ref[...] = z[:, 2 * NA_W:3 * NA_W]
        gk_ref[...] = _pair_rms(z[:, koff:koff + GQA_KW], kn_ref[...])
        gv_ref[...] = z[:, koff + GQA_KW:koff + 2 * GQA_KW]
    else:
        att_ref[:, :qoff] = z[:, :qoff].astype(BF16)
        for p in range(GQA_HEADS // 2):
            ps = slice(qoff + p * PAIR_W, qoff + (p + 1) * PAIR_W)
            q2 = _pair_rope(_pair_rms(z[:, ps], qn_ref[...]), cos_ref[...], sin_ref[...]) * Q_SCALE
            att_ref[:, ps] = q2.astype(BF16)
        k2 = _pair_rope(_pair_rms(z[:, koff:koff + GQA_KW], kn_ref[...]), cos_ref[...], sin_ref[...])
        att_ref[:, koff:koff + GQA_KW] = k2.astype(BF16)
        att_ref[:, koff + GQA_KW:] = z[:, koff + GQA_KW:ATT_W].astype(BF16)


def _inproj(x, mod, w_a, seq_len, kn2, qn2=None, cos=None, sin=None):
    n = x.shape[0]
    t = min(TOK_TILE_IN, seq_len)
    per = n // mod.shape[0] // t
    with_state = qn2 is None
    tile = lambda w: pl.BlockSpec((t, w), lambda i: (i, 0))
    vec = pl.BlockSpec((1, PAIR_W), lambda i: (0, 0))
    in_specs = [
        pl.BlockSpec((t, D_MODEL), lambda i: (i, 0)),
        pl.BlockSpec((1, N_MOD, D_MODEL), lambda i: (i // per, 0, 0)),
        pl.BlockSpec((D_MODEL, ATT_W + CP_W), lambda i: (0, 0)),
    ]
    out_specs = [tile(ATT_W), tile(CP_W)]
    out_shape = [jax.ShapeDtypeStruct((n, ATT_W), BF16), jax.ShapeDtypeStruct((n, CP_W), F32)]
    if with_state:
        in_specs.append(vec)
        args = [x, mod, w_a, kn2]
        for w in (NA_W, NA_W, GQA_KW, GQA_KW):
            out_specs.append(tile(w))
            out_shape.append(jax.ShapeDtypeStruct((n, w), F32))
    else:
        nblk = seq_len // t
        pos = pl.BlockSpec((t, PAIR_W), lambda i: (i % nblk, 0))
        in_specs += [vec, vec, pos, pos]
        args = [x, mod, w_a, qn2, kn2, cos, sin]
    return pl.pallas_call(
        functools.partial(_inproj_kernel, with_state=with_state),
        grid=(n // t,),
        in_specs=in_specs,
        out_specs=out_specs,
        out_shape=out_shape,
        compiler_params=_cparams(("arbitrary",)),
        name="inproj",
    )(*args)


def _dup_head(x2, g):
    lo = _lane_lo(x2.shape)
    other = pltpu.roll(x2, HEAD_DIM, 1)
    return jnp.where(lo, x2, other) if g == 0 else jnp.where(lo, other, x2)


def _dot_nt(a, b):
    return lax.dot_general(a, b, (((1,), (1,)), ((), ())), preferred_element_type=F32)


def _softmax_pv(scores, values):
    m = scores[0].max(axis=-1, keepdims=True)
    for s in scores[1:]:
        m = jnp.maximum(m, s.max(axis=-1, keepdims=True))
    l = None
    o = None
    for s, v in zip(scores, values):
        p = jnp.exp2(s - m)
        ls = p.sum(axis=-1, keepdims=True)
        os_ = jnp.dot(p.astype(BF16), v, preferred_element_type=F32)
        l = ls if l is None else l + ls
        o = os_ if o is None else o + os_
    return o / l


def _pair_attn(q2, keys, values, biases=None, stack=False, values_hi=None):
    lo = _lane_lo(q2.shape)
    m_rows = q2.shape[0]
    q_halves = [jnp.where(lo, q2, 0.0).astype(BF16), jnp.where(lo, 0.0, q2).astype(BF16)]

    def run(q, row0):
        scores = []
        for bi, k2 in enumerate(keys):
            s = _dot_nt(q, k2)
            if biases is not None and biases[bi] is not None:
                s = s + biases[bi][row0:row0 + q.shape[0]]
            scores.append(s)
        return _softmax_pv(scores, values)

    if stack:
        o = run(jnp.concatenate(q_halves, axis=0), 0)
        return jnp.where(lo, o[:m_rows], o[m_rows:])
    if values_hi is None:
        return jnp.where(lo, run(q_halves[0], 0), run(q_halves[1], m_rows))

    def run_mxu_sum(q, vals):
        scores = [_dot_nt(q, k2) for k2 in keys]
        m = scores[0].max(axis=-1, keepdims=True)
        for s in scores[1:]:
            m = jnp.maximum(m, s.max(axis=-1, keepdims=True))
        o = None
        for s, v in zip(scores, vals):
            os_ = jnp.dot(jnp.exp2(s - m).astype(BF16), v, preferred_element_type=F32)
            o = os_ if o is None else o + os_
        return o / pltpu.roll(o, HEAD_DIM, 1)

    return jnp.where(lo, run_mxu_sum(q_halves[0], values), run_mxu_sum(q_halves[1], values_hi))


def _ctx_attn_kernel(z_ref, qn_ref, kn_ref, o_ref):
    for p in range(NA_HEADS // 2):
        ps = slice(p * PAIR_W, (p + 1) * PAIR_W)
        q2 = z_ref[:, ps].astype(F32) * Q_SCALE
        k2 = z_ref[:, NA_W + p * PAIR_W:NA_W + (p + 1) * PAIR_W]
        v2 = z_ref[:, 2 * NA_W + p * PAIR_W:2 * NA_W + (p + 1) * PAIR_W]
        o_ref[:, ps] = _pair_attn(q2, [k2], [v2], stack=True).astype(BF16)
    qoff = 3 * NA_W
    koff = qoff + GQA_QW
    voff = koff + GQA_KW
    kn2 = _pair_rms(z_ref[:, koff:koff + GQA_KW].astype(F32), kn_ref[...])
    v2 = z_ref[:, voff:voff + GQA_KW].astype(F32)
    kd = [_dup_head(kn2, g).astype(BF16) for g in range(GQA_KV_HEADS)]
    vd = [_dup_head(v2, g).astype(BF16) for g in range(GQA_KV_HEADS)]
    for p in range(GQA_HEADS // 2):
        g = (2 * p) // GQA_REP
        q2 = _pair_rms(z_ref[:, qoff + p * PAIR_W:qoff + (p + 1) * PAIR_W].astype(F32), qn_ref[...]) * Q_SCALE
        o_ref[:, NA_W + p * PAIR_W:NA_W + (p + 1) * PAIR_W] = _pair_attn(q2, [kd[g]], [vd[g]],
                                                                          stack=True).astype(BF16)


def _ctx_attn(z_att, qn2, kn2, seq_len):
    n = z_att.shape[0]
    return pl.pallas_call(
        _ctx_attn_kernel,
        grid=(n // seq_len,),
        in_specs=[
            pl.BlockSpec((seq_len, ATT_W), lambda b: (b, 0)),
            pl.BlockSpec((1, PAIR_W), lambda b: (0, 0)),
            pl.BlockSpec((1, PAIR_W), lambda b: (0, 0)),
        ],
        out_specs=pl.BlockSpec((seq_len, NA_W + GQA_QW), lambda b: (b, 0)),
        out_shape=jax.ShapeDtypeStruct((n, NA_W + GQA_QW), BF16),
        compiler_params=_cparams(("arbitrary",)),
        name="ctx_attn",
    )(z_att, qn2, kn2)


def _na_row_start(r, rows):
    return jnp.clip(r - NA_WIN_R // 2, 0, rows - NA_WIN_R)


def _lat_na_kernel(q_ref, k_ref, v_ref, kc_ref, vc_ref, tb_ref, o_ref, *, rows, g_rows):
    r0 = pl.program_id(1) * g_rows
    wr = NA_WIN_R
    m2 = 2 * GRID_W
    lo = _lane_lo((GRID_W, PAIR_W))
    for p in range(NA_HEADS // 2):
        ps = slice(p * PAIR_W, (p + 1) * PAIR_W)
        kc2 = kc_ref[0, :, ps]
        vc2 = vc_ref[0, :, ps]
        qs = []
        for a in range(g_rows):
            q2 = q_ref[a * GRID_W:(a + 1) * GRID_W, ps].astype(F32) * Q_SCALE
            qs += [jnp.where(lo, q2, 0.0).astype(BF16), jnp.where(lo, 0.0, q2).astype(BF16)]
        qs_all = jnp.concatenate(qs, axis=0)
        s_ctx_all = _dot_nt(qs_all, kc2)
        o_loc, p_ctx, l_sum = [], [], []
        for a in range(g_rows):
            r = r0 + a
            rs = _na_row_start(r, rows)
            start = pl.multiple_of(rs * GRID_W, GRID_W)
            k2 = k_ref[pl.ds(start, wr * GRID_W), ps]
            v2 = v_ref[pl.ds(start, wr * GRID_W), ps]
            bias = tb_ref[rs - r + (wr - 1), 2 * p:2 * p + 2].reshape(m2, wr * GRID_W)
            s_loc = _dot_nt(qs_all[a * m2:(a + 1) * m2], k2) + bias
            s_ctx = s_ctx_all[a * m2:(a + 1) * m2]
            m = jnp.maximum(s_loc.max(axis=-1, keepdims=True), s_ctx.max(axis=-1, keepdims=True))
            e_loc = jnp.exp2(s_loc - m)
            e_ctx = jnp.exp2(s_ctx - m)
            l_sum.append(e_loc.sum(axis=-1, keepdims=True) + e_ctx.sum(axis=-1, keepdims=True))
            o_loc.append(jnp.dot(e_loc.astype(BF16), v2, preferred_element_type=F32))
            p_ctx.append(e_ctx.astype(BF16))
        o_ctx_all = jnp.dot(jnp.concatenate(p_ctx, axis=0), vc2, preferred_element_type=F32)
        for a in range(g_rows):
            o = (o_loc[a] + o_ctx_all[a * m2:(a + 1) * m2]) / l_sum[a]
            o_ref[a * GRID_W:(a + 1) * GRID_W, ps] = jnp.where(lo, o[:GRID_W], o[GRID_W:]).astype(BF16)


def _na_bias_tables(rpb):
    wr = NA_WIN_R
    nrel = 2 * NA_WIN_C - 1
    cols = np.arange(GRID_W)
    col_start = np.clip(cols - NA_WIN_C // 2, 0, GRID_W - NA_WIN_C)
    ck = np.arange(GRID_W)[None, :]
    valid = (ck >= col_start[:, None]) & (ck < col_start[:, None] + NA_WIN_C)
    rel = ck - cols[:, None] + (NA_WIN_C - 1)
    onehot = (np.arange(nrel)[:, None, None] == rel[None]) & valid[None]
    onehot = jnp.asarray(onehot.reshape(nrel, GRID_W * GRID_W), F32)
    d = np.arange(wr)[:, None] + np.arange(wr)[None, :]
    t = jnp.einsum("lhdir,rq->lhdiq", rpb[:, :, d] * LOG2E, onehot, precision=HIGHEST)
    t = t.reshape(rpb.shape[0], NA_HEADS, wr, wr, GRID_W, GRID_W)
    t = jnp.where(valid[None, None, None, None], t, NEG_BIG)
    t = t.transpose(0, 2, 1, 4, 3, 5)
    return t.reshape(rpb.shape[0], wr, NA_HEADS, GRID_W, wr * GRID_W)


def _lat_na(z_att, kc, vc, tb, seq_len):
    n = z_att.shape[0]
    nb = n // seq_len
    rows = seq_len // GRID_W
    assert rows >= NA_WIN_R
    wr = NA_WIN_R
    p = kc.shape[1]
    g_rows = NA_ROWS_PER_STEP
    blk = rows // g_rows
    return pl.pallas_call(
        functools.partial(_lat_na_kernel, rows=rows, g_rows=g_rows),
        grid=(nb, blk),
        in_specs=[
            pl.BlockSpec((g_rows * GRID_W, NA_W), lambda b, r: (b * blk + r, 0)),
            pl.BlockSpec((seq_len, NA_W), lambda b, r: (b, 1)),
            pl.BlockSpec((seq_len, NA_W), lambda b, r: (b, 2)),
            pl.BlockSpec((1, p, NA_W), lambda b, r: (b, 0, 0)),
            pl.BlockSpec((1, p, NA_W), lambda b, r: (b, 0, 0)),
            pl.BlockSpec((wr, NA_HEADS, GRID_W, wr * GRID_W), lambda b, r: (0, 0, 0, 0),
                         pipeline_mode=pl.Buffered(1)),
        ],
        out_specs=pl.BlockSpec((g_rows * GRID_W, NA_W), lambda b, r: (b * blk + r, 0)),
        out_shape=jax.ShapeDtypeStruct((n, NA_W), BF16),
        compiler_params=_cparams(("arbitrary", "arbitrary")),
        name="lat_na",
    )(z_att, z_att, z_att, kc, vc, tb)


def _lat_gqa_kernel(q_ref, k_ref, v_ref, kc_ref, vc_ref, o_ref):
    kn2 = k_ref[...].astype(F32)
    v2 = v_ref[...].astype(F32)
    kc2 = kc_ref[0].astype(F32)
    vc2 = vc_ref[0].astype(F32)
    def with_ones(x2, g):
        d = _dup_head(x2, g)
        lo = _lane_lo(d.shape)
        return jnp.where(lo, d, 1.0).astype(BF16), jnp.where(lo, 1.0, d).astype(BF16)

    kd, kcd, v_lo, v_hi = [], [], [], []
    for g in range(GQA_KV_HEADS):
        kd.append(_dup_head(kn2, g).astype(BF16))
        kcd.append(_dup_head(kc2, g).astype(BF16))
        (a0, a1), (b0, b1) = with_ones(v2, g), with_ones(vc2, g)
        v_lo.append([a0, b0])
        v_hi.append([a1, b1])
    for p in range(GQA_HEADS // 2):
        g = (2 * p) // GQA_REP
        ps = slice(p * PAIR_W, (p + 1) * PAIR_W)
        q2 = q_ref[:, ps].astype(F32)
        o_ref[:, ps] = _pair_attn(q2, [kd[g], kcd[g]], v_lo[g], values_hi=v_hi[g]).astype(BF16)


def _rope_tables(seq_len):
    t = jnp.arange(seq_len)
    n_freq = HEAD_DIM // 4
    inv = ROPE_THETA ** (-jnp.arange(n_freq, dtype=F32) / n_freq)
    ang_r = (t // GRID_W).astype(F32)[:, None] * inv
    ang_c = (t % GRID_W).astype(F32)[:, None] * inv
    cr, sr, cc, sc = jnp.cos(ang_r), jnp.sin(ang_r), jnp.cos(ang_c), jnp.sin(ang_c)
    cos = jnp.concatenate([cr, cr, cc, cc] * 2, axis=-1)
    sin = jnp.concatenate([-sr, sr, -sc, sc] * 2, axis=-1)
    return cos, sin


def _lat_gqa(z_att, kc, vc, seq_len):
    n = z_att.shape[0]
    nb = n // seq_len
    tq = min(GQA_QBLOCK, seq_len)
    nqb = seq_len // tq
    p = kc.shape[1]
    qcol = (3 * NA_W) // GQA_QW
    kcol = (3 * NA_W + GQA_QW) // GQA_KW
    return pl.pallas_call(
        _lat_gqa_kernel,
        grid=(nb, nqb),
        in_specs=[
            pl.BlockSpec((tq, GQA_QW), lambda b, i: (b * nqb + i, qcol)),
            pl.BlockSpec((seq_len, GQA_KW), lambda b, i: (b, kcol)),
            pl.BlockSpec((seq_len, GQA_KW), lambda b, i: (b, kcol + 1)),
            pl.BlockSpec((1, p, GQA_KW), lambda b, i: (b, 0, 0)),
            pl.BlockSpec((1, p, GQA_KW), lambda b, i: (b, 0, 0)),
        ],
        out_specs=pl.BlockSpec((tq, GQA_QW), lambda b, i: (b * nqb + i, 0)),
        out_shape=jax.ShapeDtypeStruct((n, GQA_QW), BF16),
        compiler_params=_cparams(("arbitrary", "arbitrary")),
        name="lat_gqa",
    )(z_att, z_att, z_att, kc, vc)


def _layer_norm(x, g, b):
    mu = jnp.mean(x, axis=-1, keepdims=True)
    xc = x - mu
    var = jnp.mean(xc * xc, axis=-1, keepdims=True)
    return xc * lax.rsqrt(var + LN_EPS) * g + b


def _merge_kernel(x_ref, mod_ref, cp_ref, cpp_ref, cpn_ref, *rest, seq_len, n_att):
    att_refs = rest[:n_att]
    (wg_ref, wb_ref, wo_ref, pw_ref, cw_ref, cb_ref, ps_ref, lg_ref, lb_ref, wr_ref,
     x1_ref, xm2_ref, aff_ref, ucv_ref, upl_ref) = rest[n_att:]
    t = x_ref.shape[0]
    halo = POOL_HALO
    i = pl.program_id(0)
    pos0 = (i * t) % seq_len
    has_prev = pos0 > 0
    has_next = pos0 + t < seq_len

    shift1, scale1, gate1 = mod_ref[0, 0:1, :], mod_ref[0, 1:2, :], mod_ref[0, 2:3, :]
    shift2, scale2 = mod_ref[0, 3:4, :], mod_ref[0, 4:5, :]

    bw = BRANCH_W
    ucv_ref[halo:halo + t, :] = cp_ref[:, bw:2 * bw] * cp_ref[:, 2 * bw:3 * bw]
    upl_ref[halo:halo + t, :] = cp_ref[:, 3 * bw:4 * bw]
    ucv_ref[0:halo, :] = jnp.where(has_prev, cpp_ref[:, bw:2 * bw] * cpp_ref[:, 2 * bw:3 * bw], 0.0)
    upl_ref[0:halo, :] = jnp.where(has_prev, cpp_ref[:, 3 * bw:4 * bw], 0.0)
    ucv_ref[halo + t:, :] = jnp.where(has_next, cpn_ref[:, bw:2 * bw] * cpn_ref[:, 2 * bw:3 * bw], 0.0)
    upl_ref[halo + t:, :] = jnp.where(has_next, cpn_ref[:, 3 * bw:4 * bw], 0.0)

    wr = wr_ref[...]
    wh = wr.astype(BF16)
    wl = (wr - wh.astype(F32)).astype(BF16)

    th = min(t, MERGE_SUB)
    blocks = [dict(r0=r0, rows=slice(r0, r0 + th)) for r0 in range(0, t, th)]

    def prologue(b):
        r0, rows = b["r0"], b["rows"]
        x = x_ref[rows, :]
        b["x"] = x
        b["xm"] = (x * (1.0 + scale1) + shift1).astype(BF16)
        conv = (cw_ref[0:1, :] * ucv_ref[halo - 1 + r0:halo - 1 + r0 + th, :]
                + cw_ref[1:2, :] * ucv_ref[halo + r0:halo + r0 + th, :]
                + cw_ref[2:3, :] * ucv_ref[halo + 1 + r0:halo + 1 + r0 + th, :]) + cb_ref[...]
        o_conv = (cp_ref[rows, 0:bw] * conv).astype(BF16)
        pos = pos0 + r0 + lax.broadcasted_iota(I32, (th, 1), 0)
        mixed = []
        for g, win in enumerate(POOL_WINDOWS):
            gs = slice(g * POOL_GW, (g + 1) * POOL_GW)
            acc = None
            for dlt in range(-(win // 2), win // 2):
                term = upl_ref[halo + r0 + dlt:halo + r0 + dlt + th, gs]
                acc = term if acc is None else acc + term
            lo = jnp.maximum(pos - win // 2, 0)
            hi = jnp.minimum(pos + win // 2, seq_len)
            cnt = (hi - lo).astype(F32)
            pooled = acc / cnt - upl_ref[halo + r0:halo + r0 + th, gs]
            mixed.append(jnp.dot(pooled.astype(BF16), pw_ref[g], preferred_element_type=F32))
        o_pool = (jnp.concatenate(mixed, axis=-1) * ps_ref[...]).astype(BF16)
        if n_att == 1:
            b["branches"] = [att_refs[0][rows, 0:bw], att_refs[0][rows, bw:2 * bw], o_conv, o_pool]
        else:
            b["branches"] = [att_refs[0][rows, :], att_refs[1][rows, :], o_conv, o_pool]
        b["merged"] = None

    def dots(b, nb):
        b["zg"] = jnp.dot(b["xm"], wg_ref[:, nb * D_MODEL:(nb + 1) * D_MODEL], preferred_element_type=F32)
        b["proj"] = jnp.dot(b["branches"][nb], wb_ref[nb], preferred_element_type=F32)

    def gate(b):
        term = _sigmoid(b["zg"]) * b["proj"]
        b["merged"] = term if b["merged"] is None else b["merged"] + term

    def out_proj(b):
        b["y"] = jnp.dot(b["merged"].astype(BF16), wo_ref[...], preferred_element_type=F32)

    def tail(b):
        rows = b["rows"]
        x1 = _layer_norm(DEEPNORM_ALPHA * b["x"] + gate1 * b["y"], lg_ref[...], lb_ref[...])
        x1_ref[rows, :] = x1
        xm2 = x1 * (1.0 + scale2) + shift2
        xm2_ref[rows, :] = xm2
        xh = xm2.astype(BF16)
        xl = (xm2 - xh.astype(F32)).astype(BF16)
        logits = (jnp.dot(xh, wh, preferred_element_type=F32) + jnp.dot(xl, wh, preferred_element_type=F32)
                  + jnp.dot(xh, wl, preferred_element_type=F32))
        m = logits.max(axis=-1, keepdims=True)
        e = jnp.exp(logits - m)
        aff_ref[rows, :] = e / e.sum(axis=-1, keepdims=True)

    for b in blocks:
        prologue(b)
        for nb in range(N_BRANCH):
            dots(b, nb)
            gate(b)
        out_proj(b)
        tail(b)


def _merge(x, mod, z_cp, atts, w_gates, w_branch, w_out, pool_w, conv_w, conv_b, pool_scale, ln_g, ln_b,
           w_router, seq_len):
    n = x.shape[0]
    t = min(MERGE_TILE, seq_len)
    per = n // mod.shape[0] // t
    hb = t // POOL_HALO
    nhb = n // POOL_HALO
    row = lambda a: a.reshape(1, -1)
    const2 = lambda i: (0, 0)
    const3 = lambda i: (0, 0, 0)
    once = lambda shp, imap: pl.BlockSpec(shp, imap, pipeline_mode=pl.Buffered(1))
    att_specs = [pl.BlockSpec((t, a.shape[1]), lambda i: (i, 0)) for a in atts]
    return pl.pallas_call(
        functools.partial(_merge_kernel, seq_len=seq_len, n_att=len(atts)),
        grid=(n // t,),
        in_specs=[
            pl.BlockSpec((t, D_MODEL), lambda i: (i, 0)),
            pl.BlockSpec((1, N_MOD, D_MODEL), lambda i: (i // per, 0, 0)),
            pl.BlockSpec((t, CP_W), lambda i: (i, 0)),
            pl.BlockSpec((POOL_HALO, CP_W), lambda i: (jnp.maximum(i * hb - 1, 0), 0)),
            pl.BlockSpec((POOL_HALO, CP_W), lambda i: (jnp.minimum((i + 1) * hb, nhb - 1), 0)),
            *att_specs,
            once((D_MODEL, GATE_W), const2),
            once((N_BRANCH, BRANCH_W, D_MODEL), const3),
            once((D_MODEL, D_MODEL), const2),
            pl.BlockSpec((len(POOL_WINDOWS), POOL_GW, POOL_GW), const3),
            pl.BlockSpec((CONV_K, BRANCH_W), const2),
            pl.BlockSpec((1, BRANCH_W), const2),
            pl.BlockSpec((1, BRANCH_W), const2),
            pl.BlockSpec((1, D_MODEL), const2),
            pl.BlockSpec((1, D_MODEL), const2),
            pl.BlockSpec((D_MODEL, N_EXPERTS), const2),
        ],
        out_specs=[
            pl.BlockSpec((t, D_MODEL), lambda i: (i, 0)),
            pl.BlockSpec((t, D_MODEL), lambda i: (i, 0)),
            pl.BlockSpec((t, N_EXPERTS), lambda i: (i, 0)),
        ],
        out_shape=[jax.ShapeDtypeStruct((n, D_MODEL), F32), jax.ShapeDtypeStruct((n, D_MODEL), F32),
                   jax.ShapeDtypeStruct((n, N_EXPERTS), F32)],
        scratch_shapes=[pltpu.VMEM((t + 2 * POOL_HALO, BRANCH_W), F32),
                        pltpu.VMEM((t + 2 * POOL_HALO, BRANCH_W), F32)],
        compiler_params=_cparams(("arbitrary",)),
        name="merge",
    )(x, mod, z_cp, z_cp, z_cp, *atts, w_gates, w_branch, w_out, pool_w, conv_w, row(conv_b), row(pool_scale),
      row(ln_g), row(ln_b), w_router)


def _topk_kernel(aff_ref, idx_ref, dst_ref, gp_ref, cnt_ref, thr_ref, rank_ref, *, cap, plane_stride):
    ne, nc, _ = aff_ref.shape
    capf = float(cap)
    bits_all = pltpu.bitcast(aff_ref[...], I32)

    def search(i, cur):
        cand = cur | (jnp.int32(1) << (30 - i))
        cnt = jnp.sum((bits_all >= cand).astype(F32), axis=(1, 2), keepdims=True)
        return jnp.where(cnt >= capf, cand, cur)

    thr = lax.fori_loop(0, 31, search, jnp.zeros((ne, 1, 1), I32))
    thr_ref[...] = jnp.broadcast_to(thr, thr_ref.shape)

    jj = lax.broadcasted_iota(I32, (CHUNK, CHUNK), 0)
    kk = lax.broadcasted_iota(I32, (CHUNK, CHUNK), 1)
    ut_incl = (jj <= kk).astype(BF16)
    lt_incl = (kk <= jj).astype(BF16)
    cc = lax.broadcasted_iota(I32, (nc, nc), 0)
    dd = lax.broadcasted_iota(I32, (nc, nc), 1)
    cl_excl = (dd < cc).astype(BF16)
    cl_incl = (dd <= cc).astype(BF16)
    eye = (jj == kk).astype(BF16)
    wide = 4 if cap % (4 * CHUNK) == 0 else 1
    lane_w = lax.broadcasted_iota(I32, (1, wide * CHUNK), 1)
    sub_cw = lax.broadcasted_iota(I32, (nc, wide * CHUNK), 0).astype(F32)
    sub_jw = lax.broadcasted_iota(I32, (CHUNK, wide * CHUNK), 0).astype(F32)
    rank_ref[...] = jnp.zeros(rank_ref.shape, F32)
    gp_ref[...] = jnp.full(gp_ref.shape, -1.0, F32)

    def per_expert(e, carry):
        a = aff_ref[e]
        b = pltpu.bitcast(a, I32)
        t = thr_ref[e][0:1, :]
        gt = b > t
        eq = b == t
        need = capf - jnp.sum(gt.astype(F32), keepdims=True)
        eqf = eq.astype(F32)
        incl_eq = jnp.dot(eqf.astype(BF16), ut_incl, preferred_element_type=F32)
        tot_eq = jnp.broadcast_to(incl_eq[:, CHUNK - 1:CHUNK], (nc, CHUNK))
        cum_eq = jnp.dot(cl_excl, tot_eq.astype(BF16), preferred_element_type=F32)
        rank = cum_eq + incl_eq - eqf
        sel = jnp.logical_or(gt, jnp.logical_and(eq, rank < need))
        self_ = sel.astype(F32)

        plane = rank_ref[...]
        rank_ref[...] = plane + self_
        for k in range(ne):
            gp_ref[k] = jnp.where(jnp.logical_and(sel, plane == float(k)), a, gp_ref[k])
        plane_tb = _dot_nt(eye, plane.astype(BF16)).astype(BF16)

        selb = self_.astype(BF16)
        incl = jnp.dot(selb, ut_incl, preferred_element_type=F32)
        tot = jnp.broadcast_to(incl[:, CHUNK - 1:CHUNK], (nc, CHUNK))
        cum_incl = jnp.dot(cl_incl, tot.astype(BF16), preferred_element_type=F32)
        incl_tb = _dot_nt(lt_incl, selb).astype(BF16)

        def per_block(sb, c2):
            s_row = (sb * (wide * CHUNK) + lane_w).astype(F32)
            m = cum_w <= s_row
            c_of_s = jnp.sum(m.astype(F32), axis=0, keepdims=True)
            excl_s = jnp.max(jnp.where(m, cum_w, 0.0), axis=0, keepdims=True)
            onehot_t = (sub_cw == c_of_s).astype(BF16)
            rows_t = jnp.dot(incl_tb, onehot_t, preferred_element_type=F32)
            s_local = s_row - excl_s
            t_local = jnp.sum((rows_t <= s_local).astype(F32), axis=0, keepdims=True)
            tok = (c_of_s * float(CHUNK) + t_local).astype(I32)
            plane_rows = jnp.dot(plane_tb, onehot_t, preferred_element_type=F32)
            plane_s = jnp.sum(jnp.where(sub_jw == t_local, plane_rows, 0.0), axis=0, keepdims=True)
            dst = plane_s.astype(I32) * plane_stride + tok
            for u in range(wide):
                idx_ref[e, pl.ds(sb * wide + u, 1), :] = tok[:, u * CHUNK:(u + 1) * CHUNK]
                dst_ref[e, pl.ds(sb * wide + u, 1), :] = dst[:, u * CHUNK:(u + 1) * CHUNK]
            return c2

        cum_w = jnp.concatenate([cum_incl] * wide, axis=1)
        lax.fori_loop(0, cap // (wide * CHUNK), per_block, 0)
        return carry

    lax.fori_loop(0, ne, per_expert, 0)
    cnt_ref[...] = jnp.broadcast_to(jnp.max(rank_ref[...], axis=1, keepdims=True), cnt_ref.shape).astype(I32)


def _topk(aff3, cap, plane_stride):
    ne, nc, _ = aff3.shape
    full3 = lambda shp: pl.BlockSpec(shp, lambda i: (0, 0, 0))
    return pl.pallas_call(
        functools.partial(_topk_kernel, cap=cap, plane_stride=plane_stride),
        grid=(1,),
        in_specs=[full3((ne, nc, CHUNK))],
        out_specs=[full3((ne, cap // CHUNK, CHUNK)), full3((ne, cap // CHUNK, CHUNK)), full3((ne, nc, CHUNK)),
                   pl.BlockSpec((nc, CHUNK), lambda i: (0, 0))],
        out_shape=[jax.ShapeDtypeStruct((ne, cap // CHUNK, CHUNK), I32),
                   jax.ShapeDtypeStruct((ne, cap // CHUNK, CHUNK), I32),
                   jax.ShapeDtypeStruct((ne, nc, CHUNK), F32),
                   jax.ShapeDtypeStruct((nc, CHUNK), I32)],
        scratch_shapes=[pltpu.VMEM((ne, SUBLANES, CHUNK), I32), pltpu.VMEM((nc, CHUNK), F32)],
        compiler_params=_cparams(("arbitrary",)),
        name="topk",
    )(aff3)


def _ffn_kernel(idx_ref, idxn_ref, dst_ref, x_hbm, wg_ref, wu_ref, wd_ref, y_in_hbm, y_hbm, xs_ref, ys_ref, wgu_ref,
                wdn_ref, gsem_ref, ssem_ref, *, steps):
    tm = xs_ref.shape[1]
    s = pl.program_id(0) * pl.num_programs(1) + pl.program_id(1)

    @pl.when(pl.program_id(1) == 0)
    def _():
        wgu_ref[0] = wg_ref[0].astype(BF16)
        wgu_ref[1] = wu_ref[0].astype(BF16)
        wdn_ref[...] = wd_ref[0].astype(BF16)

    def gather_rows(idx_smem, buf_slot):
        for j in range(tm):
            pltpu.make_async_copy(x_hbm.at[pl.ds(idx_smem[0, 0, j], 1), :], xs_ref.at[buf_slot, pl.ds(j, 1), :],
                                  gsem_ref.at[buf_slot]).start()

    def gathered(buf_slot):
        return pltpu.make_async_copy(x_hbm.at[pl.ds(0, tm), :], xs_ref.at[buf_slot], gsem_ref.at[buf_slot])

    def scattered(buf_slot):
        return pltpu.make_async_copy(ys_ref.at[buf_slot], y_hbm.at[pl.ds(0, tm), :], ssem_ref.at[buf_slot])

    @pl.when(s == 0)
    def _():
        gather_rows(idx_ref, 0)

    def step(slot):
        @pl.when(s + 1 < steps)
        def _():
            gather_rows(idxn_ref, 1 - slot)

        gathered(slot).wait()
        xs = xs_ref[slot].astype(BF16)
        hg = jnp.dot(xs, wgu_ref[0], preferred_element_type=F32)
        hu = jnp.dot(xs, wgu_ref[1], preferred_element_type=F32)
        hdn = (hg * _sigmoid(hg) * hu).astype(BF16)
        ye = jnp.dot(hdn, wdn_ref[...], preferred_element_type=F32)

        @pl.when(s >= 2)
        def _():
            scattered(slot).wait()

        ys_ref[slot] = _pack_bf16_pair(ye[:, :PACK_W], ye[:, PACK_W:])
        for j in range(tm):
            pltpu.make_async_copy(ys_ref.at[slot, pl.ds(j, 1), :], y_hbm.at[pl.ds(dst_ref[0, 0, j], 1), :],
                                  ssem_ref.at[slot]).start(priority=j % 2)

        @pl.when(s == steps - 1)
        def _():
            scattered(slot).wait()
            if steps > 1:
                scattered(1 - slot).wait()

    for parity in range(2):
        pl.when(s % 2 == parity)(functools.partial(step, parity))


def _ffn(idx_tiles, dst_tiles, xm2, w_gate, w_up, w_down, layer, cap, ybuf):
    tm = idx_tiles.shape[2]
    nt = cap // tm
    steps = N_EXPERTS * nt
    wspec = lambda: pl.BlockSpec((None, 1, D_MODEL, EXPERT_FF), lambda e, i: (layer, e, 0, 0))
    return pl.pallas_call(
        functools.partial(_ffn_kernel, steps=steps),
        grid=(N_EXPERTS, nt),
        in_specs=[
            pl.BlockSpec((1, 1, tm), lambda e, i: (e * nt + i, 0, 0), memory_space=pltpu.SMEM),
            pl.BlockSpec((1, 1, tm), lambda e, i: (jnp.minimum(e * nt + i + 1, steps - 1), 0, 0),
                         memory_space=pltpu.SMEM),
            pl.BlockSpec((1, 1, tm), lambda e, i: (e * nt + i, 0, 0), memory_space=pltpu.SMEM),
            pl.BlockSpec(memory_space=pl.ANY),
            wspec(), wspec(),
            pl.BlockSpec((None, 1, EXPERT_FF, D_MODEL), lambda e, i: (layer, e, 0, 0)),
            pl.BlockSpec(memory_space=pl.ANY),
        ],
        out_specs=pl.BlockSpec(memory_space=pl.ANY),
        out_shape=jax.ShapeDtypeStruct(ybuf.shape, U32),
        input_output_aliases={7: 0},
        scratch_shapes=[pltpu.VMEM((2, tm, D_MODEL), F32), pltpu.VMEM((2, tm, PACK_W), U32),
                        pltpu.VMEM((2, D_MODEL, EXPERT_FF), BF16), pltpu.VMEM((EXPERT_FF, D_MODEL), BF16),
                        pltpu.SemaphoreType.DMA((2,)), pltpu.SemaphoreType.DMA((2,))],
        compiler_params=_cparams(("arbitrary", "arbitrary")),
        name="ffn",
    )(idx_tiles, idx_tiles, dst_tiles, xm2, w_gate, w_up, w_down, ybuf)


def _combine_kernel(kmax_ref, y_hbm, x1_ref, mod_ref, gp_ref, lg_ref, lb_ref, o_ref, planes_ref, acc_ref, sem_ref,
                    *, n_planes):
    t = x1_ref.shape[0]
    i = pl.program_id(0)
    ntile = pl.num_programs(0)
    slot = i % 2

    def for_planes(tile, buf_slot, fn):
        for k in range(n_planes):
            @pl.when(k < kmax_ref[tile])
            def _(k=k):
                fn(pltpu.make_async_copy(y_hbm.at[k, pl.ds(tile * t, t), :], planes_ref.at[buf_slot, k],
                                         sem_ref.at[buf_slot]))

    @pl.when(i == 0)
    def _():
        for_planes(0, 0, lambda cp: cp.start())

    @pl.when(i + 1 < ntile)
    def _():
        for_planes(i + 1, 1 - slot, lambda cp: cp.start())

    for_planes(i, slot, lambda cp: cp.wait())

    acc_ref[...] = jnp.zeros(acc_ref.shape, F32)
    for k in range(n_planes):
        @pl.when(k < kmax_ref[i])
        def _(k=k):
            w = gp_ref[:, k:k + 1]
            lo, hi = _unpack_bf16_pair(planes_ref[slot, k])
            acc_ref[:, :PACK_W] += jnp.where(w >= 0.0, w * lo, 0.0)
            acc_ref[:, PACK_W:] += jnp.where(w >= 0.0, w * hi, 0.0)

    gate2 = mod_ref[0, 5:6, :]
    o_ref[...] = _layer_norm(DEEPNORM_ALPHA * x1_ref[...] + gate2 * acc_ref[...], lg_ref[...], lb_ref[...])


def _combine(kmax, y_planes, x1, mod, gp_tok, ln_g, ln_b, seq_len):
    n = x1.shape[0]
    n_planes = y_planes.shape[0]
    t = min(TOK_TILE, n // mod.shape[0])
    per = n // mod.shape[0] // t
    row = lambda a: a.reshape(1, -1)
    grid_spec = pltpu.PrefetchScalarGridSpec(
        num_scalar_prefetch=1,
        grid=(n // t,),
        in_specs=[
            pl.BlockSpec(memory_space=pl.ANY),
            pl.BlockSpec((t, D_MODEL), lambda i, *_: (i, 0)),
            pl.BlockSpec((1, N_MOD, D_MODEL), lambda i, *_: (i // per, 0, 0)),
            pl.BlockSpec((t, n_planes), lambda i, *_: (i, 0)),
            pl.BlockSpec((1, D_MODEL), lambda i, *_: (0, 0)),
            pl.BlockSpec((1, D_MODEL), lambda i, *_: (0, 0)),
        ],
        out_specs=pl.BlockSpec((t, D_MODEL), lambda i, *_: (i, 0)),
        scratch_shapes=[pltpu.VMEM((2, n_planes, t, PACK_W), U32), pltpu.VMEM((t, D_MODEL), F32),
                        pltpu.SemaphoreType.DMA((2,))],
    )
    return pl.pallas_call(
        functools.partial(_combine_kernel, n_planes=n_planes),
        grid_spec=grid_spec,
        out_shape=jax.ShapeDtypeStruct((n, D_MODEL), F32),
        compiler_params=_cparams(("arbitrary",)),
        name="combine",
    )(kmax, y_planes, x1, mod, gp_tok, row(ln_g), row(ln_b))


def _moe(x1, xm2, aff, mod, w_gate, w_up, w_down, layer, ln_g, ln_b, seq_len, ybuf):
    n = x1.shape[0]
    cap = EC_CAPACITY * n // N_EXPERTS
    nc = n // CHUNK
    t = min(TOK_TILE, n // mod.shape[0])
    stride = ybuf.shape[0] // N_EXPERTS
    aff3 = aff.T.reshape(N_EXPERTS, nc, CHUNK)
    idx3, dst3, gp3, cnt = _topk(aff3, cap, stride)
    gp_tok = gp3.reshape(N_EXPERTS, n).T
    kmax = jnp.max(cnt[:, 0].reshape(n // t, t // CHUNK), axis=1)
    tm = min(FFN_TILE, cap)
    idx_tiles = idx3.reshape(N_EXPERTS * cap // tm, 1, tm)
    dst_tiles = dst3.reshape(N_EXPERTS * cap // tm, 1, tm)
    ybuf = _ffn(idx_tiles, dst_tiles, xm2, w_gate, w_up, w_down, layer, cap, ybuf)
    y = _combine(kmax, ybuf.reshape(N_EXPERTS, stride, PACK_W), x1, mod, gp_tok, ln_g, ln_b, seq_len)
    return y, ybuf


def kernel(x_prompt, x_sample, cache_na_k, cache_na_v, cache_gqa_k, cache_gqa_v, c, c_ctx, w_ada, b_ada, w_in,
           na_rpb, qn_g, kn_g, conv_w, conv_b, pool_w, pool_scale, w_branch, w_out, ln1_g, ln1_b, ln2_g, ln2_b,
           w_router, w_gate, w_up, w_down):
    depth = w_in.shape[0]
    bp, lp, _ = x_prompt.shape
    bs, ls, _ = x_sample.shape
    past = cache_na_k.shape[2]

    mods = _adaln(jnp.concatenate([c_ctx[None, :], c], axis=0), w_ada, b_ada)
    mods = mods.reshape(depth, 1 + bs, N_MOD, D_MODEL)
    cos, sin = _rope_tables(ls)
    tbs = _na_bias_tables(na_rpb)

    yp = x_prompt.reshape(bp * lp, D_MODEL)
    ys = x_sample.reshape(bs * ls, D_MODEL)
    ybuf_p = jnp.zeros((N_EXPERTS * bp * lp, PACK_W), U32)
    ybuf_s = jnp.zeros((N_EXPERTS * bs * ls, PACK_W), U32)
    na_k_l, na_v_l, g_k_l, g_v_l = [], [], [], []
    for l in range(depth):
        mod_c = mods[l, 0:1]
        mod_s = mods[l, 1:]
        w_a = w_in[l, :, :ATT_W + CP_W].astype(BF16)
        w_gates = w_in[l, :, ATT_W + CP_W:].astype(BF16)
        w_br = w_branch[l].astype(BF16)
        w_o = w_out[l].astype(BF16)
        p_w = pool_w[l].astype(BF16)
        moe_w = (w_gate, w_up, w_down, l)
        merge_w = (w_gates, w_br, w_o, p_w, conv_w[l], conv_b[l], pool_scale[l], ln1_g[l], ln1_b[l], w_router[l])

        qn2 = jnp.tile(qn_g[l], 2).reshape(1, PAIR_W)
        kn2 = jnp.tile(kn_g[l], 2).reshape(1, PAIR_W)
        cache = lambda a, w: a[:, l].reshape(bs, past, w).astype(BF16)

        z_att, z_cp, na_k, na_v, g_k, g_v = _inproj(yp, mod_c, w_a, bp * lp, kn2)
        o_att = _ctx_attn(z_att, qn2, kn2, lp)
        na_k_l.append(na_k.reshape(bp, lp, NA_HEADS, HEAD_DIM))
        na_v_l.append(na_v.reshape(bp, lp, NA_HEADS, HEAD_DIM))
        g_k_l.append(g_k.reshape(bp, lp, GQA_KV_HEADS, HEAD_DIM))
        g_v_l.append(g_v.reshape(bp, lp, GQA_KV_HEADS, HEAD_DIM))
        x1, xm2, aff = _merge(yp, mod_c, z_cp, [o_att], *merge_w, lp)
        yp, ybuf_p = _moe(x1, xm2, aff, mod_c, *moe_w, ln2_g[l], ln2_b[l], lp, ybuf_p)

        z_att, z_cp = _inproj(ys, mod_s, w_a, ls, kn2, qn2, cos, sin)
        o_na = _lat_na(z_att, cache(cache_na_k, NA_W), cache(cache_na_v, NA_W), tbs[l], ls)
        o_gqa = _lat_gqa(z_att, cache(cache_gqa_k, GQA_KW), cache(cache_gqa_v, GQA_KW), ls)
        x1, xm2, aff = _merge(ys, mod_s, z_cp, [o_na, o_gqa], *merge_w, ls)
        ys, ybuf_s = _moe(x1, xm2, aff, mod_s, *moe_w, ln2_g[l], ln2_b[l], ls, ybuf_s)

    return (yp.reshape(bp, lp, D_MODEL), ys.reshape(bs, ls, D_MODEL),
            jnp.stack(na_k_l, axis=1), jnp.stack(na_v_l, axis=1),
            jnp.stack(g_k_l, axis=1), jnp.stack(g_v_l, axis=1))
```
